```python
import math
import jax
import jax.numpy as jnp
from jax import lax
import numpy as np

D_MODEL = 1024
BATCH = 16
SEQ = 4096
DEPTH = 4

GRID_W = 64
CTX_LEN = 256
N_MIXERS = 3
NORM_EPS = 1e-6
ROPE_BASE = 10000.0
QBLOCK = 128
DA_HEAD = 64
DA_HEADS = D_MODEL // (2 * DA_HEAD)
WA_HEAD = 64
WA_Q_HEADS = D_MODEL // WA_HEAD
WA_KV_HEADS = WA_Q_HEADS // 4
WINDOW = 128
RET_KEY = 256
RET_HEADS = D_MODEL // RET_KEY
RET_VAL = 2 * RET_KEY
RET_CHUNK = 128
FFN_DIM = ((8 * D_MODEL // 3 + 255) // 256) * 256
N_EXPERTS = 8
TOP_K = 2
EXPERT_DIM = 7 * D_MODEL // 2
MOE_BLOCK = 256

kernel_name = 'hybrid_dit_diffattn_swa_retention_moe'


def rms_norm(x, g):
    x32 = x.astype(jnp.float32)
    y = x32 * lax.rsqrt(jnp.mean(x32 * x32, axis=-1, keepdims=True) + NORM_EPS)
    return (y * g.astype(jnp.float32)).astype(x.dtype)


def modulate(x, g, shift, scale):
    return rms_norm(x, g) * (1 + scale) + shift


def grid_positions(n_tokens):
    rows = n_tokens // GRID_W
    row = jnp.repeat(jnp.arange(rows, dtype=jnp.int32), GRID_W)
    col = jnp.tile(jnp.arange(GRID_W, dtype=jnp.int32), rows)
    return row, col


def rope_angles(pos, dim):
    inv = ROPE_BASE ** (-jnp.arange(0, dim, 2, dtype=jnp.float32) / dim)
    ang = pos.astype(jnp.float32)[:, None] * inv[None, :]
    return jnp.concatenate([ang, ang], axis=-1)


def apply_rotary(x, ang):
    shape = (1, x.shape[1]) + (1,) * (x.ndim - 3) + (x.shape[-1],)
    cos = jnp.cos(ang).reshape(shape).astype(x.dtype)
    sin = jnp.sin(ang).reshape(shape).astype(x.dtype)
    x1, x2 = jnp.split(x, 2, axis=-1)
    return x * cos + jnp.concatenate([-x2, x1], axis=-1) * sin


def axial_rotary(x, row, col):
    half = x.shape[-1] // 2
    return jnp.concatenate([apply_rotary(x[..., :half], rope_angles(row, half)),
                            apply_rotary(x[..., half:], rope_angles(col, half))], axis=-1)


def swiglu(h, w_gu, w_down):
    g, u = jnp.split(h @ w_gu, 2, axis=-1)
    return (jax.nn.silu(g) * u) @ w_down


def diff_attend(q, k, v, lam):
    s = jnp.einsum('bqhmd,bkhmd->bhmqk', q, k).astype(jnp.float32) * (q.shape[-1] ** -0.5)
    p = jax.nn.softmax(s, axis=-1)
    a = p[:, :, 0] - lam * p[:, :, 1]
    return jnp.einsum('bhqk,bkhe->bqhe', a.astype(v.dtype), v)


def diff_attention_mixer(a, ac, w_in, w_out, lam_p, subln_g, lam_init, need_ctx):
    B, L, _ = a.shape

    def project(t):
        shp = t.shape[:2]
        q, k, v = jnp.split(t @ w_in, 3, axis=-1)
        return (q.reshape(shp + (DA_HEADS, 2, DA_HEAD)), k.reshape(shp + (DA_HEADS, 2, DA_HEAD)),
                v.reshape(shp + (DA_HEADS, 2 * DA_HEAD)))

    q, k, v = project(a)
    qc, kc, vc = project(ac)
    row, col = grid_positions(L)
    q = axial_rotary(q, row, col)
    k = axial_rotary(k, row, col)
    lp = lam_p.astype(jnp.float32)
    lam = jnp.exp(jnp.sum(lp[0] * lp[1])) - jnp.exp(jnp.sum(lp[2] * lp[3])) + lam_init
    keys = jnp.concatenate([kc, k], axis=1)
    vals = jnp.concatenate([vc, v], axis=1)
    nb = L // QBLOCK
    qb = q.reshape(B, nb, QBLOCK, DA_HEADS, 2, DA_HEAD).swapaxes(0, 1)
    ob = lax.map(lambda qi: diff_attend(qi, keys, vals, lam), qb)
    o = ob.swapaxes(0, 1).reshape(B, L, DA_HEADS, 2 * DA_HEAD)

    def finish(t):
        t = rms_norm(t, subln_g) * (1.0 - lam_init)
        return t.reshape(t.shape[:2] + (D_MODEL,)) @ w_out

    y = finish(o)
    yc = finish(diff_attend(qc, kc, vc, lam)) if need_ctx else None
    return y, yc


def sink_attend(q, k, v, mask, sink):
    s = jnp.einsum('bqgrd,bkgd->bgrqk', q, k).astype(jnp.float32) * (q.shape[-1] ** -0.5)
    if mask is not None:
        s = jnp.where(mask, s, -1e30)
    sink_col = jnp.broadcast_to(sink.astype(jnp.float32).reshape(1, WA_KV_HEADS, -1, 1, 1), s.shape[:-1] + (1,))
    p = jax.nn.softmax(jnp.concatenate([s, sink_col], axis=-1), axis=-1)[..., :-1]
    return jnp.einsum('bgrqk,bkgd->bqgrd', p.astype(v.dtype), v)


def window_gqa_mixer(a, ac, w_in, w_out, sink, need_ctx):
    B, L, _ = a.shape
    rep = WA_Q_HEADS // WA_KV_HEADS

    def project(t):
        shp = t.shape[:2]
        q, k, v = jnp.split(t @ w_in, [WA_Q_HEADS * WA_HEAD, (WA_Q_HEADS + WA_KV_HEADS) * WA_HEAD], axis=-1)
        return (q.reshape(shp + (WA_KV_HEADS, rep, WA_HEAD)), k.reshape(shp + (WA_KV_HEADS, WA_HEAD)),
                v.reshape(shp + (WA_KV_HEADS, WA_HEAD)))

    q, k, v = project(a)
    qc, kc, vc = project(ac)
    row, col = grid_positions(L)
    q = axial_rotary(q, row, col)
    k = axial_rotary(k, row, col)
    pad = ((0, 0), (WINDOW, WINDOW), (0, 0), (0, 0))
    kp = jnp.pad(k, pad)
    vp = jnp.pad(v, pad)
    span = QBLOCK + 2 * WINDOW
    n_ctx = kc.shape[1]
    ctx_ok = jnp.ones((QBLOCK, n_ctx), dtype=bool)

    def block(args):
        qi, i = args
        start = i * QBLOCK
        kw = lax.dynamic_slice_in_dim(kp, start, span, axis=1)
        vw = lax.dynamic_slice_in_dim(vp, start, span, axis=1)
        qpos = start + jnp.arange(QBLOCK)
        kpos = start - WINDOW + jnp.arange(span)
        win = (jnp.abs(qpos[:, None] - kpos[None, :]) <= WINDOW) & (kpos >= 0)[None, :] & (kpos < L)[None, :]
        mask = jnp.concatenate([ctx_ok, win], axis=1)
        return sink_attend(qi, jnp.concatenate([kc, kw], axis=1), jnp.concatenate([vc, vw], axis=1), mask, sink)

    nb = L // QBLOCK
    qb = q.reshape(B, nb, QBLOCK, WA_KV_HEADS, rep, WA_HEAD).swapaxes(0, 1)
    o = lax.map(block, (qb, jnp.arange(nb))).swapaxes(0, 1).reshape(B, L, D_MODEL)
    y = o @ w_out
    yc = sink_attend(qc, kc, vc, None, sink).reshape(B, n_ctx, D_MODEL) @ w_out if need_ctx else None
    return y, yc


def retention_scan(q, k, v, log_g, state0, include_diag):
    B, H, L, _ = q.shape
    dv = v.shape[-1]
    n = L // RET_CHUNK
    idx = jnp.arange(RET_CHUNK, dtype=jnp.float32)
    rel = idx[:, None] - idx[None, :]
    keep = (rel >= 0) if include_diag else (rel > 0)
    lg = log_g[:, None, None]
    intra_decay = jnp.where(keep[None], jnp.exp(jnp.where(keep[None], rel[None] * lg, 0.0)), 0.0)
    q_decay = jnp.exp((idx + 1.0)[None, :] * log_g[:, None])[..., None]
    k_decay = jnp.exp((RET_CHUNK - 1.0 - idx)[None, :] * log_g[:, None])[..., None]
    chunk_decay = jnp.exp(RET_CHUNK * log_g)[:, None, None]

    def to_chunks(t):
        return t.reshape(B, H, n, RET_CHUNK, t.shape[-1]).transpose(2, 0, 1, 3, 4)

    def step(state, xs):
        qi, ki, vi = xs
        inner = jnp.einsum('bhcd,bhmd->bhcm', qi, ki) * intra_decay
        o = jnp.einsum('bhcm,bhmv->bhcv', inner, vi) + jnp.einsum('bhcd,bhdv->bhcv', qi, state) * q_decay
        state = state * chunk_decay + jnp.einsum('bhmd,bhmv->bhdv', ki * k_decay, vi)
        return state, o

    state, o = lax.scan(step, state0, (to_chunks(q), to_chunks(k), to_chunks(v)))
    return o.transpose(1, 2, 0, 3, 4).reshape(B, H, L, dv), state


def retention_mixer(a, ac, w_in, w_out, decay_logit, need_ctx):
    B, L, _ = a.shape
    qk_w = RET_HEADS * RET_KEY
    v_w = RET_HEADS * RET_VAL

    def project(t, pos):
        shp = t.shape[:2]
        q, k, v, g = jnp.split(t @ w_in, [qk_w, 2 * qk_w, 2 * qk_w + v_w], axis=-1)
        q = q.reshape(shp + (RET_HEADS, RET_KEY))
        k = k.reshape(shp + (RET_HEADS, RET_KEY)) * (RET_KEY ** -0.5)
        if pos is not None:
            ang = rope_angles(pos, RET_KEY)
            q = apply_rotary(q, ang)
            k = apply_rotary(k, ang)
        v = v.reshape(shp + (RET_HEADS, RET_VAL))
        heads = lambda t_: t_.transpose(0, 2, 1, 3).astype(jnp.float32)
        return heads(q), heads(k), heads(v), g

    q, k, v, g = project(a, jnp.arange(L))
    qc, kc, vc, gc = project(ac, None)
    log_g = jax.nn.log_sigmoid(decay_logit.astype(jnp.float32))
    zero = jnp.zeros((B, RET_HEADS, RET_KEY, RET_VAL), jnp.float32)
    rev = lambda t: jnp.flip(t, axis=2)
    oc_f, s_f = retention_scan(qc, kc, vc, log_g[0], zero, True)
    oc_b, s_b = retention_scan(rev(qc), rev(kc), rev(vc), log_g[1], zero, False)
    o_f, _ = retention_scan(q, k, v, log_g[0], s_f, True)
    o_b, _ = retention_scan(rev(q), rev(k), rev(v), log_g[1], s_b, False)

    def finish(o, gate):
        o = o * lax.rsqrt(jnp.mean(o * o, axis=-1, keepdims=True) + NORM_EPS)
        o = o.transpose(0, 2, 1, 3).reshape(gate.shape).astype(gate.dtype)
        return (jax.nn.silu(gate) * o) @ w_out

    y = finish(o_f + rev(o_b), g)
    yc = finish(oc_f + rev(oc_b), gc) if need_ctx else None
    return y, yc


def moe_swiglu(h, w_router, w_gu, w_down):
    B, L, D = h.shape
    T = B * L
    xt = h.reshape(T, D)
    logits = (xt @ w_router).astype(jnp.float32)
    top_val, top_idx = lax.top_k(logits, TOP_K)
    gates = jax.nn.softmax(top_val, axis=-1)
    n_assign = T * TOP_K
    flat_e = top_idx.reshape(-1)
    flat_tok = jnp.arange(n_assign) // TOP_K
    order = jnp.argsort(flat_e)
    sorted_e = flat_e[order]
    sorted_tok = flat_tok[order]
    sorted_gate = gates.reshape(-1)[order]
    counts = jnp.bincount(flat_e, length=N_EXPERTS)
    padded = (counts + MOE_BLOCK - 1) // MOE_BLOCK * MOE_BLOCK
    start = jnp.cumsum(counts) - counts
    pstart = jnp.cumsum(padded) - padded
    dest = pstart[sorted_e] + jnp.arange(n_assign) - start[sorted_e]
    n_blocks = -(-n_assign // MOE_BLOCK) + N_EXPERTS
    buf = jnp.zeros((n_blocks * MOE_BLOCK, D), h.dtype).at[dest].set(xt[sorted_tok])
    block_expert = jnp.minimum(jnp.searchsorted(jnp.cumsum(padded), jnp.arange(n_blocks) * MOE_BLOCK, side='right'), N_EXPERTS - 1)

    def run_block(args):
        xb, e = args
        g, u = jnp.split(xb @ w_gu[e], 2, axis=-1)
        return (jax.nn.silu(g) * u) @ w_down[e]

    yb = lax.map(run_block, (buf.reshape(n_blocks, MOE_BLOCK, D), block_expert)).reshape(-1, D)
    y_sorted = yb[dest] * sorted_gate[:, None].astype(h.dtype)
    y = jnp.zeros((T, D), h.dtype).at[sorted_tok].add(y_sorted)
    return y.reshape(B, L, D)


def setup_inputs(seed: int = 0) -> dict:
    key = jax.random.key(seed)
    ks = iter(jax.random.split(key, 32))
    f32 = jnp.float32
    n_da = len(range(0, DEPTH, N_MIXERS))
    n_wa = len(range(1, DEPTH, N_MIXERS))
    n_ret = len(range(2, DEPTH, N_MIXERS))
    n_dense = len(range(0, DEPTH, 2))
    n_moe = len(range(1, DEPTH, 2))

    def normal(shape, scale):
        return jax.random.normal(next(ks), shape, f32) * scale

    def dense(shape, fan_in, gain=1.0):
        return normal(shape, gain * fan_in ** -0.5)

    wa_width = (WA_Q_HEADS + 2 * WA_KV_HEADS) * WA_HEAD
    ret_width = 2 * RET_HEADS * RET_KEY + 2 * RET_HEADS * RET_VAL
    base_logit = jnp.log(jnp.exp2(5.0 + jnp.arange(RET_HEADS, dtype=f32)) - 1.0)
    return {
        'x': normal((BATCH, SEQ, D_MODEL), 1.0),
        'c': normal((BATCH, D_MODEL), 1.0),
        'ctx': normal((BATCH, CTX_LEN, D_MODEL), 1.0),
        'c_ctx': normal((D_MODEL,), 1.0),
        'mod_w': dense((DEPTH, D_MODEL, 6 * D_MODEL), D_MODEL, 0.5),
        'mod_b': normal((DEPTH, 6 * D_MODEL), 0.02),
        'norm_g': 1.0 + normal((DEPTH, 4, D_MODEL), 0.05),
        'da_w_in': dense((n_da, D_MODEL, 3 * D_MODEL), D_MODEL),
        'da_w_out': dense((n_da, D_MODEL, D_MODEL), D_MODEL),
        'da_lambda': normal((n_da, 4, DA_HEAD), 0.1),
        'da_subln_g': 1.0 + normal((n_da, 2 * DA_HEAD), 0.05),
        'wa_w_in': dense((n_wa, D_MODEL, wa_width), D_MODEL),
        'wa_w_out': dense((n_wa, D_MODEL, D_MODEL), D_MODEL),
        'wa_sink': normal((n_wa, WA_Q_HEADS), 1.0),
        'ret_w_in': dense((n_ret, D_MODEL, ret_width), D_MODEL),
        'ret_w_out': dense((n_ret, RET_HEADS * RET_VAL, D_MODEL), RET_HEADS * RET_VAL),
        'ret_decay_logit': base_logit[None, None, :] + normal((n_ret, 2, RET_HEADS), 0.1),
        'ffn_w_gu': dense((n_dense, D_MODEL, 2 * FFN_DIM), D_MODEL),
        'ffn_w_down': dense((n_dense, FFN_DIM, D_MODEL), FFN_DIM),
        'moe_router': dense((n_moe, D_MODEL, N_EXPERTS), D_MODEL),
        'moe_w_gu': dense((n_moe, N_EXPERTS, D_MODEL, 2 * EXPERT_DIM), D_MODEL),
        'moe_w_down': dense((n_moe, N_EXPERTS, EXPERT_DIM, D_MODEL), EXPERT_DIM),
    }


def reference(x, c, ctx, c_ctx, mod_w, mod_b, norm_g, da_w_in, da_w_out, da_lambda, da_subln_g,
              wa_w_in, wa_w_out, wa_sink, ret_w_in, ret_w_out, ret_decay_logit,
              ffn_w_gu, ffn_w_down, moe_router, moe_w_gu, moe_w_down):
    h_ctx = ctx
    mixer_count = [0] * N_MIXERS
    i_dense = 0
    i_moe = 0
    for layer in range(DEPTH):
        last = layer == DEPTH - 1
        mod = (jax.nn.silu(c) @ mod_w[layer] + mod_b[layer])[:, None, :]
        mod_c = (jax.nn.silu(c_ctx) @ mod_w[layer] + mod_b[layer])[None, None, :]
        sh1, sc1, g1, sh2, sc2, g2 = jnp.split(mod, 6, axis=-1)
        csh1, csc1, cg1, csh2, csc2, cg2 = jnp.split(mod_c, 6, axis=-1)

        a = modulate(x, norm_g[layer, 0], sh1, sc1)
        ac = modulate(h_ctx, norm_g[layer, 0], csh1, csc1)
        kind = layer % N_MIXERS
        j = mixer_count[kind]
        mixer_count[kind] += 1
        if kind == 0:
            lam_init = 0.8 - 0.6 * math.exp(-0.3 * layer)
            y, yc = diff_attention_mixer(a, ac, da_w_in[j], da_w_out[j], da_lambda[j], da_subln_g[j], lam_init, not last)
        elif kind == 1:
            y, yc = window_gqa_mixer(a, ac, wa_w_in[j], wa_w_out[j], wa_sink[j], not last)
        else:
            y, yc = retention_mixer(a, ac, ret_w_in[j], ret_w_out[j], ret_decay_logit[j], not last)
        x = x + g1 * rms_norm(y, norm_g[layer, 1])
        if not last:
            h_ctx = h_ctx + cg1 * rms_norm(yc, norm_g[layer, 1])

        a = modulate(x, norm_g[layer, 2], sh2, sc2)
        if layer % 2 == 0:
            w_gu, w_dn = ffn_w_gu[i_dense], ffn_w_down[i_dense]
            i_dense += 1
            ffn = lambda t: swiglu(t, w_gu, w_dn)
        else:
            w_r, w_gu, w_dn = moe_router[i_moe], moe_w_gu[i_moe], moe_w_down[i_moe]
            i_moe += 1
            ffn = lambda t: moe_swiglu(t, w_r, w_gu, w_dn)
        x = x + g2 * rms_norm(ffn(a), norm_g[layer, 3])
        if not last:
            ac = modulate(h_ctx, norm_g[layer, 2], csh2, csc2)
            h_ctx = h_ctx + cg2 * rms_norm(ffn(ac), norm_g[layer, 3])
    return x
```

```python
import functools
import math

import jax
import jax.numpy as jnp
from jax import lax
from jax.experimental import pallas as pl
from jax.experimental.pallas import tpu as pltpu

F32 = jnp.float32
BF16 = jnp.bfloat16

D_MODEL = 1024
N_MIXERS = 3
NORM_EPS = 1e-6
ROPE_BASE = 10000.0
GRID_W = 64
DA_HEAD = 64
DA_HEADS = D_MODEL // (2 * DA_HEAD)
WA_HEAD = 64
WA_Q_HEADS = D_MODEL // WA_HEAD
WA_KV_HEADS = WA_Q_HEADS // 4
WINDOW = 128
RET_KEY = 256
RET_HEADS = D_MODEL // RET_KEY
RET_VAL = 2 * RET_KEY
RET_CHUNK = 128
N_EXPERTS = 8
TOP_K = 2

LANES = 128
V7X_VMEM_LIMIT = 56 * 1024 * 1024
ROUTER_PAD = LANES
MOE_ROWS = 1024
MOE_FC = 512


def _cparams(sem):
    return pltpu.CompilerParams(dimension_semantics=sem, vmem_limit_bytes=V7X_VMEM_LIMIT)


def _pick(n, prefs):
    for p in prefs:
        if n % p == 0:
            return p
    return n


def _sigmoid(x):
    return 1.0 / (1.0 + jnp.exp(-x))


def _rms(y):
    return y * lax.rsqrt(jnp.mean(y * y, axis=-1, keepdims=True) + NORM_EPS)


def _dot(a, b):
    return jnp.dot(a, b, preferred_element_type=F32)


def _dot_nt(a, b):
    return lax.dot_general(a, b, (((1,), (1,)), ((), ())), preferred_element_type=F32)


def _dot_tn(a, b):
    return lax.dot_general(a, b, (((0,), (0,)), ((), ())), preferred_element_type=F32)


def _mod_kernel(c_ref, w_ref, b_ref, o_ref):
    cs = c_ref[...]
    s = cs * _sigmoid(cs)
    w = w_ref[0]
    s_hi = s.astype(BF16)
    s_lo = (s - s_hi.astype(F32)).astype(BF16)
    w_hi = w.astype(BF16)
    w_lo = (w - w_hi.astype(F32)).astype(BF16)
    o_ref[0] = _dot(s_hi, w_hi) + _dot(s_lo, w_hi) + _dot(s_hi, w_lo) + b_ref[0]


def _mod_vectors(cs, mod_w, mod_b):
    depth, d, n = mod_w.shape
    r = cs.shape[0]
    tn = _pick(n, (1024, 512, 256, 128))
    return pl.pallas_call(
        _mod_kernel,
        grid=(depth, n // tn),
        in_specs=[
            pl.BlockSpec((r, d), lambda l, j: (0, 0)),
            pl.BlockSpec((1, d, tn), lambda l, j: (l, 0, j)),
            pl.BlockSpec((1, 1, tn), lambda l, j: (l, 0, j)),
        ],
        out_specs=pl.BlockSpec((1, r, tn), lambda l, j: (l, 0, j)),
        out_shape=jax.ShapeDtypeStruct((depth, r, n), F32),
        compiler_params=_cparams(("parallel", "parallel")),
        name="mod_vectors",
    )(cs, mod_w, mod_b.reshape(depth, 1, n))


def _modulated(x_ref, g_ref, sh_ref, sc_ref):
    x = x_ref[...]
    return _rms(x) * g_ref[...] * (1.0 + sc_ref[0]) + sh_ref[0]


def _rotate(acc, cos_ref, sa_ref, sb_ref, shift, cw):
    cos, sa, sb = cos_ref[...], sa_ref[...], sb_ref[...]
    outs = []
    for c in range(acc.shape[1] // cw):
        y = acc[:, c * cw:(c + 1) * cw]
        up = pltpu.roll(y, cw - shift, 1)
        dn = pltpu.roll(y, shift, 1)
        outs.append(y * cos + up * sa + dn * sb)
    return outs[0] if len(outs) == 1 else jnp.concatenate(outs, axis=1)


def _proj_kernel(*refs, mode, n_rot_tiles, shift, cw):
    if mode == "swiglu":
        x_ref, g_ref, sh_ref, sc_ref, wg_ref, wu_ref, o_ref, a_scr = refs
    elif mode == "rot":
        x_ref, g_ref, sh_ref, sc_ref, w_ref, cos_ref, sa_ref, sb_ref, o_ref, a_scr = refs
    else:
        x_ref, g_ref, sh_ref, sc_ref, w_ref, o_ref, a_scr = refs
    j = pl.program_id(1)

    @pl.when(j == 0)
    def _():
        a_scr[...] = _modulated(x_ref, g_ref, sh_ref, sc_ref).astype(BF16)

    a = a_scr[...]
    if mode == "swiglu":
        g = _dot(a, wg_ref[...])
        u = _dot(a, wu_ref[...])
        o_ref[...] = (g * _sigmoid(g) * u).astype(o_ref.dtype)
    elif mode == "rot":
        acc = _dot(a, w_ref[...])

        @pl.when(j < n_rot_tiles)
        def _():
            o_ref[...] = _rotate(acc, cos_ref, sa_ref, sb_ref, shift, cw).astype(o_ref.dtype)

        @pl.when(j >= n_rot_tiles)
        def _():
            o_ref[...] = acc.astype(o_ref.dtype)
    else:
        o_ref[...] = _dot(a, w_ref[...]).astype(o_ref.dtype)


def _norm_proj(x, g, mod, sh_idx, rows_per_mod, weights, *, tm, tn, rot=None, name):
    t, d = x.shape
    swiglu = isinstance(weights, tuple)
    n = weights[0].shape[1] if swiglu else weights.shape[1]
    assert t % tm == 0 and n % tn == 0
    if rows_per_mod is None:
        mod_row = lambda i: 0
    else:
        assert rows_per_mod % tm == 0
        tiles_per_mod = rows_per_mod // tm
        mod_row = lambda i: i // tiles_per_mod
    in_specs = [
        pl.BlockSpec((tm, d), lambda i, j: (i, 0)),
        pl.BlockSpec((1, d), lambda i, j: (0, 0)),
        pl.BlockSpec((1, 1, d), lambda i, j: (mod_row(i), 0, sh_idx)),
        pl.BlockSpec((1, 1, d), lambda i, j: (mod_row(i), 0, sh_idx + 1)),
    ]
    args = [x, g.reshape(1, d), mod, mod]
    w_spec = pl.BlockSpec((d, tn), lambda i, j: (0, j))
    n_rot_tiles, shift, cw = 0, 0, 0
    if swiglu:
        mode = "swiglu"
        in_specs += [w_spec, w_spec]
        args += list(weights)
    elif rot is not None:
        mode = "rot"
        cos, sa, sb, n_rot_cols, shift, seq_len = rot
        cw = cos.shape[1]
        assert n_rot_cols % tn == 0 and tn % cw == 0 and seq_len % tm == 0
        n_rot_tiles = n_rot_cols // tn
        tiles_per_seq = seq_len // tm
        t_spec = pl.BlockSpec((tm, cw), lambda i, j: (i % tiles_per_seq, 0))
        in_specs += [w_spec, t_spec, t_spec, t_spec]
        args += [weights, cos, sa, sb]
    else:
        mode = "plain"
        in_specs += [w_spec]
        args += [weights]
    return pl.pallas_call(
        functools.partial(_proj_kernel, mode=mode, n_rot_tiles=n_rot_tiles, shift=shift, cw=cw),
        grid=(t // tm, n // tn),
        in_specs=in_specs,
        out_specs=pl.BlockSpec((tm, tn), lambda i, j: (i, j)),
        out_shape=jax.ShapeDtypeStruct((t, n), BF16),
        scratch_shapes=[pltpu.VMEM((tm, d), BF16)],
        compiler_params=_cparams(("parallel", "arbitrary")),
        name=name,
    )(*args)


def _resid_kernel(*refs, mode):
    if mode == "ret":
        of_ref, ob_ref, gt_ref, w_ref, x_ref, gate_ref, g_ref, o_ref = refs
        o = of_ref[...].astype(F32) + ob_ref[...].astype(F32)
        parts = []
        for h in range(RET_HEADS):
            parts.append(_rms(o[:, h * RET_VAL:(h + 1) * RET_VAL]))
        o = jnp.concatenate(parts, axis=1)
        gt = gt_ref[...].astype(F32)
        y = _dot((gt * _sigmoid(gt) * o).astype(BF16), w_ref[...])
    elif mode == "matmul":
        a_ref, w_ref, x_ref, gate_ref, g_ref, o_ref = refs
        y = _dot(a_ref[...], w_ref[...])
    else:
        y_ref, x_ref, gate_ref, g_ref, o_ref = refs
        y = y_ref[...].astype(F32)
    o_ref[...] = x_ref[...] + gate_ref[0] * (_rms(y) * g_ref[...])


def _resid(x, g, mod, gate_idx, rows_per_mod, *, tm, name, a=None, w=None, y=None, ret=None):
    t, d = x.shape
    assert t % tm == 0
    if rows_per_mod is None:
        mod_row = lambda i: 0
    else:
        assert rows_per_mod % tm == 0
        tiles_per_mod = rows_per_mod // tm
        mod_row = lambda i: i // tiles_per_mod
    row = lambda width: pl.BlockSpec((tm, width), lambda i: (i, 0))
    tail_specs = [row(d), pl.BlockSpec((1, 1, d), lambda i: (mod_row(i), 0, gate_idx)),
                  pl.BlockSpec((1, d), lambda i: (0, 0))]
    tail_args = [x, mod, g.reshape(1, d)]
    if ret is not None:
        mode = "ret"
        o_f, o_b, qkvg = ret
        vw = RET_HEADS * RET_VAL
        gate_col = qkvg.shape[1] // vw - 1
        in_specs = [row(vw), row(vw), pl.BlockSpec((tm, vw), lambda i: (i, gate_col)),
                    pl.BlockSpec(w.shape, lambda i: (0, 0))]
        args = [o_f, o_b, qkvg, w]
    elif a is not None:
        mode = "matmul"
        in_specs = [row(a.shape[1]), pl.BlockSpec(w.shape, lambda i: (0, 0))]
        args = [a, w]
    else:
        mode = "plain"
        in_specs = [row(d)]
        args = [y]
    return pl.pallas_call(
        functools.partial(_resid_kernel, mode=mode),
        grid=(t // tm,),
        in_specs=in_specs + tail_specs,
        out_specs=row(d),
        out_shape=jax.ShapeDtypeStruct((t, d), F32),
        compiler_params=_cparams(("parallel",)),
        name=name,
    )(*args, *tail_args)


def _da_kernel(*refs, n_seg, post_scale):
    lam_ref, q_ref = refs[0], refs[1]
    k_refs = refs[2:2 + n_seg]
    v_refs = refs[2 + n_seg:2 + 2 * n_seg]
    sg_ref, o_ref = refs[2 + 2 * n_seg], refs[3 + 2 * n_seg]
    q = q_ref[0]
    lane = lax.broadcasted_iota(jnp.int32, (1, 2 * DA_HEAD), 1)
    scale = jnp.asarray(DA_HEAD ** -0.5, BF16)
    probs, coefs = [], []
    for m in range(2):
        sel = (lane < DA_HEAD) if m == 0 else (lane >= DA_HEAD)
        qm = jnp.where(sel, q, jnp.zeros_like(q)) * scale
        ss = [_dot_nt(qm, kr[0]) for kr in k_refs]
        mx = functools.reduce(jnp.maximum, [jnp.max(s, axis=-1, keepdims=True) for s in ss])
        ps = [jnp.exp(s - mx) for s in ss]
        l = functools.reduce(jnp.add, [jnp.sum(p, axis=-1, keepdims=True) for p in ps])
        probs.append(ps)
        coefs.append(1.0 / l)
    c0 = coefs[0]
    c1 = coefs[1] * lam_ref[0, 0]
    o = None
    for s_i in range(n_seg):
        a = (probs[0][s_i] * c0 - probs[1][s_i] * c1).astype(BF16)
        part = _dot(a, v_refs[s_i][0])
        o = part if o is None else o + part
    o_ref[0] = (_rms(o) * sg_ref[...] * post_scale).astype(o_ref.dtype)


def _diff_attention(q_arr, kv_arrs, lam, subln_g, lam_init, *, tq, name):
    b, lq, _ = q_arr.shape
    hw = 2 * DA_HEAD
    n_seg = len(kv_arrs)
    assert lq % tq == 0
    in_specs = [pl.BlockSpec(memory_space=pltpu.SMEM),
                pl.BlockSpec((1, tq, hw), lambda bi, h, i: (bi, i, h))]
    in_specs += [pl.BlockSpec((1, kv.shape[1], hw), lambda bi, h, i: (bi, 0, DA_HEADS + h)) for kv in kv_arrs]
    in_specs += [pl.BlockSpec((1, kv.shape[1], hw), lambda bi, h, i: (bi, 0, 2 * DA_HEADS + h)) for kv in kv_arrs]
    in_specs += [pl.BlockSpec((1, hw), lambda bi, h, i: (0, 0))]
    return pl.pallas_call(
        functools.partial(_da_kernel, n_seg=n_seg, post_scale=1.0 - lam_init),
        grid=(b, DA_HEADS, lq // tq),
        in_specs=in_specs,
        out_specs=pl.BlockSpec((1, tq, hw), lambda bi, h, i: (bi, i, h)),
        out_shape=jax.ShapeDtypeStruct((b, lq, D_MODEL), BF16),
        compiler_params=_cparams(("parallel", "parallel", "arbitrary")),
        name=name,
    )(lam.reshape(1, 1), q_arr, *kv_arrs, *kv_arrs, subln_g.reshape(1, hw))


def _wa_kernel(*refs, windowed, seq_len):
    if windowed:
        sink_ref, q_ref, kc_ref, vc_ref, kp_ref, kq_ref, kn_ref, vp_ref, vq_ref, vn_ref, o_ref = refs
        k_all = jnp.concatenate([kc_ref[0], kp_ref[0], kq_ref[0], kn_ref[0]], axis=0)
        v_all = jnp.concatenate([vc_ref[0], vp_ref[0], vq_ref[0], vn_ref[0]], axis=0)
    else:
        sink_ref, q_ref, kc_ref, vc_ref, o_ref = refs
        k_all, v_all = kc_ref[0], vc_ref[0]
    tq = q_ref.shape[1]
    n_keys = k_all.shape[0]
    n_ctx = kc_ref.shape[1]
    i = pl.program_id(1)
    if windowed:
        qpos = i * tq + lax.broadcasted_iota(jnp.int32, (tq, n_keys), 0)
        col = lax.broadcasted_iota(jnp.int32, (tq, n_keys), 1)
        kpos = (i - 1) * tq + col - n_ctx
        ok = (col < n_ctx) | ((jnp.abs(qpos - kpos) <= WINDOW) & (kpos >= 0) & (kpos < seq_len))
    lane = lax.broadcasted_iota(jnp.int32, (1, LANES), 1)
    lo = lane < WA_HEAD
    scale = jnp.asarray(WA_HEAD ** -0.5, BF16)
    rep = WA_Q_HEADS // WA_KV_HEADS
    for g in range(WA_KV_HEADS):
        kg = k_all[:, (g // 2) * LANES:(g // 2 + 1) * LANES]
        vg = v_all[:, (g // 2) * LANES:(g // 2 + 1) * LANES]
        zero = jnp.zeros_like(kg)
        if g % 2 == 0:
            kg_same, vg_same = jnp.where(lo, kg, zero), jnp.where(lo, vg, zero)
        else:
            kg_same, vg_same = jnp.where(lo, zero, kg), jnp.where(lo, zero, vg)
        kg_swap = pltpu.roll(kg_same.astype(F32), WA_HEAD, 1).astype(BF16)
        vg_swap = pltpu.roll(vg_same.astype(F32), WA_HEAD, 1).astype(BF16)
        k_half = (kg_same, kg_swap) if g % 2 == 0 else (kg_swap, kg_same)
        v_half = (vg_same, vg_swap) if g % 2 == 0 else (vg_swap, vg_same)
        for pair in range(rep // 2):
            chunk = g * (rep // 2) + pair
            q2 = q_ref[0, :, chunk * LANES:(chunk + 1) * LANES] * scale
            out = None
            for half in range(2):
                head = 2 * chunk + half
                s = _dot_nt(q2, k_half[half])
                if windowed:
                    s = jnp.where(ok, s, -1e30)
                snk = sink_ref[head]
                mx = jnp.maximum(jnp.max(s, axis=-1, keepdims=True), snk)
                p = jnp.exp(s - mx)
                l = jnp.sum(p, axis=-1, keepdims=True) + jnp.exp(snk - mx)
                part = _dot((p * (1.0 / l)).astype(BF16), v_half[half])
                out = part if out is None else out + part
            o_ref[0, :, chunk * LANES:(chunk + 1) * LANES] = out.astype(o_ref.dtype)


def _window_attention(q_arr, ctx_arr, sink, *, windowed, name):
    b, lq, _ = q_arr.shape
    n_ctx = ctx_arr.shape[1]
    qw = WA_Q_HEADS * WA_HEAD
    kw = WA_KV_HEADS * WA_HEAD
    k_col, v_col = qw // kw, qw // kw + 1
    tq = 128 if windowed else lq
    nq = lq // tq
    in_specs = [pl.BlockSpec(memory_space=pltpu.SMEM),
                pl.BlockSpec((1, tq, qw), lambda bi, i: (bi, i, 0)),
                pl.BlockSpec((1, n_ctx, kw), lambda bi, i: (bi, 0, k_col)),
                pl.BlockSpec((1, n_ctx, kw), lambda bi, i: (bi, 0, v_col))]
    args = [sink, q_arr, ctx_arr, ctx_arr]
    if windowed:
        assert tq == WINDOW
        for col in (k_col, v_col):
            in_specs += [pl.BlockSpec((1, tq, kw), lambda bi, i, col=col: (bi, jnp.maximum(i - 1, 0), col)),
                         pl.BlockSpec((1, tq, kw), lambda bi, i, col=col: (bi, i, col)),
                         pl.BlockSpec((1, tq, kw), lambda bi, i, col=col: (bi, jnp.minimum(i + 1, nq - 1), col))]
            args += [q_arr, q_arr, q_arr]
    return pl.pallas_call(
        functools.partial(_wa_kernel, windowed=windowed, seq_len=lq),
        grid=(b, nq),
        in_specs=in_specs,
        out_specs=pl.BlockSpec((1, tq, qw), lambda bi, i: (bi, i, 0)),
        out_shape=jax.ShapeDtypeStruct((b, lq, qw), BF16),
        compiler_params=_cparams(("parallel", "parallel")),
        name=name,
    )(*args)


def _ret_kernel(cd_ref, qcf, kcf, vcf, qlf, klf, vlf, qcb, kcb, vcb, qlb, klb, vlb, dtab_ref, qd_ref, kd_ref,
                ocf, olf, ocb, olb, s_scr, *, n_ctx_chunks):
    j = pl.program_id(1)
    is_ctx = j < n_ctx_chunks

    @pl.when(j == 0)
    def _():
        s_scr[...] = jnp.zeros_like(s_scr)

    kscale = jnp.asarray(RET_KEY ** -0.5, BF16)
    dirs = ((qcf, kcf, vcf, qlf, klf, vlf, ocf, olf), (qcb, kcb, vcb, qlb, klb, vlb, ocb, olb))
    for d, (qc, kc, vc, ql, kl, vl, oc, ol) in enumerate(dirs):
        q = jnp.where(is_ctx, qc[0], ql[0])
        k = jnp.where(is_ctx, kc[0], kl[0]) * kscale
        v = jnp.where(is_ctx, vc[0], vl[0])
        outs = []
        for h in range(RET_HEADS):
            qh = q[:, h * RET_KEY:(h + 1) * RET_KEY]
            kh = k[:, h * RET_KEY:(h + 1) * RET_KEY]
            vh = v[:, h * RET_VAL:(h + 1) * RET_VAL]
            state = s_scr[d, h]
            inner = _dot_nt(qh, kh) * dtab_ref[d, h]
            o = _dot(inner.astype(BF16), vh) + _dot(qh, state.astype(BF16)) * qd_ref[d, h]
            kdec = (kh.astype(F32) * kd_ref[d, h]).astype(BF16)
            s_scr[d, h] = state * cd_ref[d, h] + _dot_tn(kdec, vh)
            outs.append(o)
        o_all = jnp.concatenate(outs, axis=1).astype(oc.dtype)

        @pl.when(is_ctx)
        def _():
            oc[0] = o_all

        @pl.when(jnp.logical_not(is_ctx))
        def _():
            ol[0] = o_all


def _retention(ctx_arr, lat_arr, log_g):
    b, lc, _ = ctx_arr.shape
    l = lat_arr.shape[1]
    c = RET_CHUNK
    nc, nl = lc // c, l // c
    qw = RET_HEADS * RET_KEY
    vw = RET_HEADS * RET_VAL
    idx = jnp.arange(c, dtype=F32)
    rel = idx[:, None] - idx[None, :]
    lg = log_g[:, :, None, None]
    keep_f = (rel >= 0)[None]
    keep_b = (rel < 0)[None]
    d_f = jnp.where(keep_f, jnp.exp(jnp.where(keep_f, rel[None] * lg[0], 0.0)), 0.0)
    d_b = jnp.where(keep_b, jnp.exp(jnp.where(keep_b, -rel[None] * lg[1], 0.0)), 0.0)
    dtab = jnp.stack([d_f, d_b])
    col = idx[None, :, None]
    qd = jnp.stack([jnp.exp((col + 1.0) * lg[0]), jnp.exp((c - col) * lg[1])])
    kd = jnp.stack([jnp.exp((c - 1.0 - col) * lg[0]), jnp.exp(col * lg[1])])
    cd = jnp.exp(c * log_g)

    f_ctx = lambda bi, j: (bi, jnp.minimum(j, nc - 1))
    f_lat = lambda bi, j: (bi, jnp.maximum(j - nc, 0))
    b_ctx = lambda bi, j: (bi, jnp.maximum(nc - 1 - j, 0))
    b_lat = lambda bi, j: (bi, jnp.minimum(nc + nl - 1 - j, nl - 1))

    def qkv_specs(row_map):
        return [pl.BlockSpec((1, c, qw), lambda bi, j: row_map(bi, j) + (0,)),
                pl.BlockSpec((1, c, qw), lambda bi, j: row_map(bi, j) + (1,)),
                pl.BlockSpec((1, c, vw), lambda bi, j: row_map(bi, j) + (1,))]

    full = lambda shape: pl.BlockSpec(shape, lambda bi, j: (0,) * len(shape))
    in_specs = ([pl.BlockSpec(memory_space=pltpu.SMEM)]
                + qkv_specs(f_ctx) + qkv_specs(f_lat) + qkv_specs(b_ctx) + qkv_specs(b_lat)
                + [full(dtab.shape), full(qd.shape), full(kd.shape)])
    out_spec = lambda row_map: pl.BlockSpec((1, c, vw), lambda bi, j: row_map(bi, j) + (0,))
    out_sds = lambda n: jax.ShapeDtypeStruct((b, n, vw), BF16)
    return pl.pallas_call(
        functools.partial(_ret_kernel, n_ctx_chunks=nc),
        grid=(b, nc + nl),
        in_specs=in_specs,
        out_specs=[out_spec(f_ctx), out_spec(f_lat), out_spec(b_ctx), out_spec(b_lat)],
        out_shape=[out_sds(lc), out_sds(l), out_sds(lc), out_sds(l)],
        scratch_shapes=[pltpu.VMEM((2, RET_HEADS, RET_KEY, RET_VAL), F32)],
        compiler_params=_cparams(("parallel", "arbitrary")),
        name="retention_scan",
    )(cd, ctx_arr, ctx_arr, ctx_arr, lat_arr, lat_arr, lat_arr, ctx_arr, ctx_arr, ctx_arr, lat_arr, lat_arr,
      lat_arr, dtab, qd, kd)


def _router_kernel(x_ref, g_ref, sh_ref, sc_ref, wr_ref, a_ref, lg_ref):
    a = _modulated(x_ref, g_ref, sh_ref, sc_ref)
    a_hi = a.astype(BF16)
    a_ref[...] = a_hi
    a_lo = (a - a_hi.astype(F32)).astype(BF16)
    w = wr_ref[...]
    w_hi = w.astype(BF16)
    w_lo = (w - w_hi.astype(F32)).astype(BF16)
    lg_ref[...] = _dot(a_hi, w_hi) + _dot(a_lo, w_hi) + _dot(a_hi, w_lo)


def _router(x, g, mod, sh_idx, rows_per_mod, w_router_pad, *, tm, name):
    t, d = x.shape
    if rows_per_mod is None:
        mod_row = lambda i: 0
    else:
        tiles_per_mod = rows_per_mod // tm
        mod_row = lambda i: i // tiles_per_mod
    return pl.pallas_call(
        _router_kernel,
        grid=(t // tm,),
        in_specs=[pl.BlockSpec((tm, d), lambda i: (i, 0)),
                  pl.BlockSpec((1, d), lambda i: (0, 0)),
                  pl.BlockSpec((1, 1, d), lambda i: (mod_row(i), 0, sh_idx)),
                  pl.BlockSpec((1, 1, d), lambda i: (mod_row(i), 0, sh_idx + 1)),
                  pl.BlockSpec((d, ROUTER_PAD), lambda i: (0, 0))],
        out_specs=[pl.BlockSpec((tm, d), lambda i: (i, 0)), pl.BlockSpec((tm, ROUTER_PAD), lambda i: (i, 0))],
        out_shape=[jax.ShapeDtypeStruct((t, d), BF16), jax.ShapeDtypeStruct((t, ROUTER_PAD), F32)],
        compiler_params=_cparams(("parallel",)),
        name=name,
    )(x, g.reshape(1, d), mod, mod, w_router_pad)


def _experts_kernel(be_ref, nb_ref, x_ref, wg_ref, wu_ref, wd_ref, o_ref, acc, *, nf):
    blk = pl.program_id(0)
    f = pl.program_id(1)
    used = blk < nb_ref[0]

    @pl.when(used)
    def _():
        x = x_ref[...]
        g = _dot(x, wg_ref[0])
        u = _dot(x, wu_ref[0])
        part = _dot((g * _sigmoid(g) * u).astype(BF16), wd_ref[0])

        @pl.when(f == 0)
        def _():
            acc[...] = part

        @pl.when(f > 0)
        def _():
            acc[...] += part

    @pl.when(f == nf - 1)
    def _():
        o_ref[...] = jnp.where(used, acc[...], 0.0).astype(o_ref.dtype)


def _experts(buf, block_expert, n_used, w_gu, w_down):
    rows, d = buf.shape
    nb = rows // MOE_ROWS
    e, _, two_f = w_gu.shape
    fdim = two_f // 2
    fc = _pick(fdim, (MOE_FC, 256, 128))
    nf = fdim // fc
    grid_spec = pltpu.PrefetchScalarGridSpec(
        num_scalar_prefetch=2,
        grid=(nb, nf),
        in_specs=[pl.BlockSpec((MOE_ROWS, d), lambda i, f, be, nu: (i, 0)),
                  pl.BlockSpec((1, d, fc), lambda i, f, be, nu: (be[i], 0, f)),
                  pl.BlockSpec((1, d, fc), lambda i, f, be, nu: (be[i], 0, nf + f)),
                  pl.BlockSpec((1, fc, d), lambda i, f, be, nu: (be[i], f, 0))],
        out_specs=pl.BlockSpec((MOE_ROWS, d), lambda i, f, be, nu: (i, 0)),
        scratch_shapes=[pltpu.VMEM((MOE_ROWS, d), F32)],
    )
    return pl.pallas_call(
        functools.partial(_experts_kernel, nf=nf),
        grid_spec=grid_spec,
        out_shape=jax.ShapeDtypeStruct((rows, d), BF16),
        compiler_params=_cparams(("parallel", "arbitrary")),
        name="moe_experts",
    )(block_expert, n_used, buf, w_gu, w_gu, w_down)


def _moe(a_all, logits, w_gu, w_down):
    t, d = a_all.shape
    e = w_gu.shape[0]
    top_val, top_idx = lax.top_k(logits, TOP_K)
    gates = jax.nn.softmax(top_val, axis=-1)
    onehot = jnp.sum((top_idx[..., None] == jnp.arange(e)[None, None, :]).astype(jnp.int32), axis=1)
    counts = jnp.sum(onehot, axis=0)
    rank = jnp.cumsum(onehot, axis=0) - onehot
    padded = (counts + MOE_ROWS - 1) // MOE_ROWS * MOE_ROWS
    pend = jnp.cumsum(padded)
    pstart = pend - padded
    dest = pstart[top_idx] + jnp.take_along_axis(rank, top_idx, axis=1)
    nb = -(-(t * TOP_K) // MOE_ROWS) + e
    tok = jnp.repeat(jnp.arange(t, dtype=jnp.int32), TOP_K)
    row_src = jnp.zeros((nb * MOE_ROWS,), jnp.int32).at[dest.reshape(-1)].set(tok)
    buf = a_all[row_src]
    block_expert = jnp.minimum(jnp.searchsorted(pend, jnp.arange(nb) * MOE_ROWS, side="right"), e - 1).astype(jnp.int32)
    n_used = (pend[-1] // MOE_ROWS).astype(jnp.int32).reshape(1)
    yb = _experts(buf, block_expert, n_used, w_gu, w_down)
    return (yb[dest[:, 0]].astype(F32) * gates[:, 0:1] + yb[dest[:, 1]].astype(F32) * gates[:, 1:2])


def _rope_angles(pos, dim):
    inv = ROPE_BASE ** (-jnp.arange(0, dim, 2, dtype=F32) / dim)
    ang = pos.astype(F32)[:, None] * inv[None, :]
    return jnp.concatenate([ang, ang], axis=-1)


def _signed_sin(sin, shift):
    low = (jnp.arange(sin.shape[1]) % (2 * shift)) < shift
    return jnp.where(low[None, :], -sin, 0.0), jnp.where(low[None, :], 0.0, sin)


def _axial_tables(n_tokens, head_dim):
    rows = n_tokens // GRID_W
    row = jnp.repeat(jnp.arange(rows, dtype=jnp.int32), GRID_W)
    col = jnp.tile(jnp.arange(GRID_W, dtype=jnp.int32), rows)
    half = head_dim // 2
    ang = jnp.concatenate([_rope_angles(row, half), _rope_angles(col, half)], axis=-1)
    ang = jnp.tile(ang, (1, LANES // head_dim))
    sa, sb = _signed_sin(jnp.sin(ang), half // 2)
    return jnp.cos(ang), sa, sb, half // 2


def _ret_tables(n_tokens):
    ang = _rope_angles(jnp.arange(n_tokens), RET_KEY)
    sa, sb = _signed_sin(jnp.sin(ang), RET_KEY // 2)
    return jnp.cos(ang), sa, sb, RET_KEY // 2


def kernel(x, c, ctx, c_ctx, mod_w, mod_b, norm_g, da_w_in, da_w_out, da_lambda, da_subln_g, wa_w_in, wa_w_out,
           wa_sink, ret_w_in, ret_w_out, ret_decay_logit, ffn_w_gu, ffn_w_down, moe_router, moe_w_gu, moe_w_down):
    b, l, d = x.shape
    lc = ctx.shape[1]
    depth = mod_w.shape[0]
    t, tc = b * l, b * lc
    tm = _pick(l, (1024, 512, 256, 128))
    tmc = _pick(tc, (1024, 512, 256, 128))

    n_cond = -(-(b + 1) // 8) * 8
    cs = jnp.zeros((n_cond, d), F32).at[:b].set(c).at[b].set(c_ctx)
    mod_all = _mod_vectors(cs, mod_w, mod_b)

    xs = x.reshape(t, d)
    hs = ctx.reshape(tc, d)
    mixer_count = [0] * N_MIXERS
    i_dense = 0
    i_moe = 0
    for layer in range(depth):
        last = layer == depth - 1
        mod = mod_all[layer, :b].reshape(b, 1, 6 * d)
        mod_c = mod_all[layer, b:b + 1].reshape(1, 1, 6 * d)
        ng = norm_g[layer]
        kind = layer % N_MIXERS
        jm = mixer_count[kind]
        mixer_count[kind] += 1

        if kind == 0:
            lam_init = 0.8 - 0.6 * math.exp(-0.3 * layer)
            w_in = da_w_in[jm].astype(BF16)
            cos, sa, sb, shift = _axial_tables(l, DA_HEAD)
            qkv = _norm_proj(xs, ng[0], mod, 0, l, w_in, tm=tm, tn=512, rot=(cos, sa, sb, 2 * d, shift, l),
                             name="da_in_proj").reshape(b, l, 3 * d)
            qkv_c = _norm_proj(hs, ng[0], mod_c, 0, None, w_in, tm=tmc, tn=512, name="da_in_proj_ctx").reshape(b, lc, 3 * d)
            lp = da_lambda[jm].astype(F32)
            lam = jnp.exp(jnp.sum(lp[0] * lp[1])) - jnp.exp(jnp.sum(lp[2] * lp[3])) + lam_init
            o = _diff_attention(qkv, [qkv_c, qkv], lam, da_subln_g[jm], lam_init, tq=_pick(l, (128,)), name="diff_attn")
            w_out = da_w_out[jm].astype(BF16)
            xs = _resid(xs, ng[1], mod, 2, l, tm=tm, a=o.reshape(t, d), w=w_out, name="da_out_proj")
            if not last:
                oc = _diff_attention(qkv_c, [qkv_c], lam, da_subln_g[jm], lam_init, tq=_pick(lc, (128,)), name="diff_attn_ctx")
                hs = _resid(hs, ng[1], mod_c, 2, None, tm=tmc, a=oc.reshape(tc, d), w=w_out, name="da_out_proj_ctx")
        elif kind == 1:
            w_in = wa_w_in[jm].astype(BF16)
            n_qkv = w_in.shape[1]
            cos, sa, sb, shift = _axial_tables(l, WA_HEAD)
            n_rot = (WA_Q_HEADS + WA_KV_HEADS) * WA_HEAD
            qkv = _norm_proj(xs, ng[0], mod, 0, l, w_in, tm=tm, tn=256, rot=(cos, sa, sb, n_rot, shift, l),
                             name="wa_in_proj").reshape(b, l, n_qkv)
            qkv_c = _norm_proj(hs, ng[0], mod_c, 0, None, w_in, tm=tmc, tn=256, name="wa_in_proj_ctx").reshape(b, lc, n_qkv)
            o = _window_attention(qkv, qkv_c, wa_sink[jm], windowed=True, name="window_attn")
            w_out = wa_w_out[jm].astype(BF16)
            xs = _resid(xs, ng[1], mod, 2, l, tm=tm, a=o.reshape(t, d), w=w_out, name="wa_out_proj")
            if not last:
                oc = _window_attention(qkv_c, qkv_c, wa_sink[jm], windowed=False, name="window_attn_ctx")
                hs = _resid(hs, ng[1], mod_c, 2, None, tm=tmc, a=oc.reshape(tc, d), w=w_out, name="wa_out_proj_ctx")
        else:
            w_in = ret_w_in[jm].astype(BF16)
            n_qkv = w_in.shape[1]
            cos, sa, sb, shift = _ret_tables(l)
            n_rot = 2 * RET_HEADS * RET_KEY
            qkv = _norm_proj(xs, ng[0], mod, 0, l, w_in, tm=tm, tn=512, rot=(cos, sa, sb, n_rot, shift, l),
                             name="ret_in_proj")
            qkv_c = _norm_proj(hs, ng[0], mod_c, 0, None, w_in, tm=tmc, tn=512, name="ret_in_proj_ctx")
            log_g = jax.nn.log_sigmoid(ret_decay_logit[jm].astype(F32))
            ocf, olf, ocb, olb = _retention(qkv_c.reshape(b, lc, n_qkv), qkv.reshape(b, l, n_qkv), log_g)
            w_out = ret_w_out[jm].astype(BF16)
            vw = RET_HEADS * RET_VAL
            xs = _resid(xs, ng[1], mod, 2, l, tm=_pick(l, (512, 256, 128)), w=w_out,
                        ret=(olf.reshape(t, vw), olb.reshape(t, vw), qkv), name="ret_out_proj")
            if not last:
                hs = _resid(hs, ng[1], mod_c, 2, None, tm=_pick(tc, (512, 256, 128)), w=w_out,
                            ret=(ocf.reshape(tc, vw), ocb.reshape(tc, vw), qkv_c), name="ret_out_proj_ctx")

        if layer % 2 == 0:
            w_gu = ffn_w_gu[i_dense].astype(BF16)
            w_dn = ffn_w_down[i_dense].astype(BF16)
            i_dense += 1
            f = w_dn.shape[0]
            w_g, w_u = w_gu[:, :f], w_gu[:, f:]
            tn = _pick(f, (1408, 1024, 512, 256, 128))
            tf = _pick(l, (512, 256, 128))
            act = _norm_proj(xs, ng[2], mod, 3, l, (w_g, w_u), tm=tf, tn=tn, name="ffn_up")
            xs = _resid(xs, ng[3], mod, 5, l, tm=tf, a=act, w=w_dn, name="ffn_down")
            if not last:
                tfc = _pick(tc, (512, 256, 128))
                act_c = _norm_proj(hs, ng[2], mod_c, 3, None, (w_g, w_u), tm=tfc, tn=tn, name="ffn_up_ctx")
                hs = _resid(hs, ng[3], mod_c, 5, None, tm=tfc, a=act_c, w=w_dn, name="ffn_down_ctx")
        else:
            w_r = jnp.zeros((d, ROUTER_PAD), F32).at[:, :N_EXPERTS].set(moe_router[i_moe])
            w_gu = moe_w_gu[i_moe].astype(BF16)
            w_dn = moe_w_down[i_moe].astype(BF16)
            i_moe += 1
            a_x, lg_x = _router(xs, ng[2], mod, 3, l, w_r, tm=tm, name="moe_router")
            if not last:
                a_c, lg_c = _router(hs, ng[2], mod_c, 3, None, w_r, tm=tmc, name="moe_router_ctx")
                a_all = jnp.concatenate([a_x, a_c], axis=0)
                lg_all = jnp.concatenate([lg_x, lg_c], axis=0)
            else:
                a_all, lg_all = a_x, lg_x
            y_all = _moe(a_all, lg_all[:, :N_EXPERTS], w_gu, w_dn)
            xs = _resid(xs, ng[3], mod, 5, l, tm=tm, y=y_all[:t], name="moe_combine")
            if not last:
                hs = _resid(hs, ng[3], mod_c, 5, None, tm=tmc, y=y_all[t:], name="moe_combine_ctx")
    return xs.reshape(b, l, d)
```

```python
import functools
import math

import jax
import jax.numpy as jnp
from jax import lax
from jax.experimental import pallas as pl
from jax.experimental.pallas import tpu as pltpu

F32 = jnp.float32
BF16 = jnp.bfloat16

D_MODEL = 1024
N_MIXERS = 3
NORM_EPS = 1e-6
ROPE_BASE = 10000.0
GRID_W = 64
DA_HEAD = 64
DA_HEADS = D_MODEL // (2 * DA_HEAD)
WA_HEAD = 64
WA_Q_HEADS = D_MODEL // WA_HEAD
WA_KV_HEADS = WA_Q_HEADS // 4
WINDOW = 128
RET_KEY = 256
RET_HEADS = D_MODEL // RET_KEY
RET_VAL = 2 * RET_KEY
RET_CHUNK = 128
N_EXPERTS = 8
TOP_K = 2

LANES = 128
V7X_VMEM_LIMIT = 56 * 1024 * 1024
ROUTER_PAD = LANES
MOE_ROWS = 1024
MOE_FC = 512


def _cparams(sem):
    return pltpu.CompilerParams(dimension_semantics=sem, vmem_limit_bytes=V7X_VMEM_LIMIT)


def _pick(n, prefs):
    for p in prefs:
        if n % p == 0:
            return p
    return n


def _sigmoid(x):
    return 1.0 / (1.0 + jnp.exp(-x))


def _rms(y):
    return y * lax.rsqrt(jnp.mean(y * y, axis=-1, keepdims=True) + NORM_EPS)


def _dot(a, b):
    return jnp.dot(a, b, preferred_element_type=F32)


def _dot_nt(a, b):
    return lax.dot_general(a, b, (((1,), (1,)), ((), ())), preferred_element_type=F32)


def _dot_tn(a, b):
    return lax.dot_general(a, b, (((0,), (0,)), ((), ())), preferred_element_type=F32)


def _mod_kernel(c_ref, w_ref, b_ref, o_ref):
    cs = c_ref[...]
    s = cs * _sigmoid(cs)
    w = w_ref[0]
    s_hi = s.astype(BF16)
    s_lo = (s - s_hi.astype(F32)).astype(BF16)
    w_hi = w.astype(BF16)
    w_lo = (w - w_hi.astype(F32)).astype(BF16)
    o_ref[0] = _dot(s_hi, w_hi) + _dot(s_lo, w_hi) + _dot(s_hi, w_lo) + b_ref[0]


def _mod_vectors(cs, mod_w, mod_b):
    depth, d, n = mod_w.shape
    r = cs.shape[0]
    tn = _pick(n, (1024, 512, 256, 128))
    return pl.pallas_call(
        _mod_kernel,
        grid=(depth, n // tn),
        in_specs=[
            pl.BlockSpec((r, d), lambda l, j: (0, 0)),
            pl.BlockSpec((1, d, tn), lambda l, j: (l, 0, j)),
            pl.BlockSpec((1, 1, tn), lambda l, j: (l, 0, j)),
        ],
        out_specs=pl.BlockSpec((1, r, tn), lambda l, j: (l, 0, j)),
        out_shape=jax.ShapeDtypeStruct((depth, r, n), F32),
        compiler_params=_cparams(("parallel", "parallel")),
        name="mod_vectors",
    )(cs, mod_w, mod_b.reshape(depth, 1, n))


def _modulated(x_ref, g_ref, sh_ref, sc_ref):
    x = x_ref[...]
    return _rms(x) * g_ref[...] * (1.0 + sc_ref[0]) + sh_ref[0]


def _rotate(acc, cos_ref, sa_ref, sb_ref, shift, cw):
    cos, sa, sb = cos_ref[...], sa_ref[...], sb_ref[...]
    outs = []
    for c in range(acc.shape[1] // cw):
        y = acc[:, c * cw:(c + 1) * cw]
        up = pltpu.roll(y, cw - shift, 1)
        dn = pltpu.roll(y, shift, 1)
        outs.append(y * cos + up * sa + dn * sb)
    return outs[0] if len(outs) == 1 else jnp.concatenate(outs, axis=1)


def _proj_kernel(*refs, mode, n_rot_tiles, shift, cw):
    if mode == "swiglu":
        x_ref, g_ref, sh_ref, sc_ref, wg_ref, wu_ref, o_ref, a_scr = refs
    elif mode == "rot":
        x_ref, g_ref, sh_ref, sc_ref, w_ref, cos_ref, sa_ref, sb_ref, o_ref, a_scr = refs
    else:
        x_ref, g_ref, sh_ref, sc_ref, w_ref, o_ref, a_scr = refs
    j = pl.program_id(1)

    @pl.when(j == 0)
    def _():
        a_scr[...] = _modulated(x_ref, g_ref, sh_ref, sc_ref).astype(BF16)

    a = a_scr[...]
    if mode == "swiglu":
        g = _dot(a, wg_ref[...])
        u = _dot(a, wu_ref[...])
        o_ref[...] = (g * _sigmoid(g) * u).astype(o_ref.dtype)
    elif mode == "rot":
        acc = _dot(a, w_ref[...])

        @pl.when(j < n_rot_tiles)
        def _():
            o_ref[...] = _rotate(acc, cos_ref, sa_ref, sb_ref, shift, cw).astype(o_ref.dtype)

        @pl.when(j >= n_rot_tiles)
        def _():
            o_ref[...] = acc.astype(o_ref.dtype)
    else:
        o_ref[...] = _dot(a, w_ref[...]).astype(o_ref.dtype)


def _norm_proj(x, g, mod, sh_idx, rows_per_mod, weights, *, tm, tn, rot=None, name):
    t, d = x.shape
    swiglu = isinstance(weights, tuple)
    n = weights[0].shape[1] if swiglu else weights.shape[1]
    assert t % tm == 0 and n % tn == 0
    if rows_per_mod is None:
        mod_row = lambda i: 0
    else:
        assert rows_per_mod % tm == 0
        tiles_per_mod = rows_per_mod // tm
        mod_row = lambda i: i // tiles_per_mod
    in_specs = [
        pl.BlockSpec((tm, d), lambda i, j: (i, 0)),
        pl.BlockSpec((1, d), lambda i, j: (0, 0)),
        pl.BlockSpec((1, 1, d), lambda i, j: (mod_row(i), 0, sh_idx)),
        pl.BlockSpec((1, 1, d), lambda i, j: (mod_row(i), 0, sh_idx + 1)),
    ]
    args = [x, g.reshape(1, d), mod, mod]
    w_spec = pl.BlockSpec((d, tn), lambda i, j: (0, j))
    n_rot_tiles, shift, cw = 0, 0, 0
    if swiglu:
        mode = "swiglu"
        in_specs += [w_spec, w_spec]
        args += list(weights)
    elif rot is not None:
        mode = "rot"
        cos, sa, sb, n_rot_cols, shift, seq_len = rot
        cw = cos.shape[1]
        assert n_rot_cols % tn == 0 and tn % cw == 0 and seq_len % tm == 0
        n_rot_tiles = n_rot_cols // tn
        tiles_per_seq = seq_len // tm
        t_spec = pl.BlockSpec((tm, cw), lambda i, j: (i % tiles_per_seq, 0))
        in_specs += [w_spec, t_spec, t_spec, t_spec]
        args += [weights, cos, sa, sb]
    else:
        mode = "plain"
        in_specs += [w_spec]
        args += [weights]
    return pl.pallas_call(
        functools.partial(_proj_kernel, mode=mode, n_rot_tiles=n_rot_tiles, shift=shift, cw=cw),
        grid=(t // tm, n // tn),
        in_specs=in_specs,
        out_specs=pl.BlockSpec((tm, tn), lambda i, j: (i, j)),
        out_shape=jax.ShapeDtypeStruct((t, n), BF16),
        scratch_shapes=[pltpu.VMEM((tm, d), BF16)],
        compiler_params=_cparams(("parallel", "arbitrary")),
        name=name,
    )(*args)


def _resid_kernel(*refs, mode):
    if mode == "ret":
        of_ref, ob_ref, gt_ref, w_ref, x_ref, gate_ref, g_ref, o_ref = refs
        o = of_ref[...].astype(F32) + ob_ref[...].astype(F32)
        parts = []
        for h in range(RET_HEADS):
            parts.append(_rms(o[:, h * RET_VAL:(h + 1) * RET_VAL]))
        o = jnp.concatenate(parts, axis=1)
        gt = gt_ref[...].astype(F32)
        y = _dot((gt * _sigmoid(gt) * o).astype(BF16), w_ref[...])
    elif mode == "matmul":
        a_ref, w_ref, x_ref, gate_ref, g_ref, o_ref = refs
        y = _dot(a_ref[...], w_ref[...])
    else:
        y_ref, x_ref, gate_ref, g_ref, o_ref = refs
        y = y_ref[...].astype(F32)
    o_ref[...] = x_ref[...] + gate_ref[0] * (_rms(y) * g_ref[...])


def _resid(x, g, mod, gate_idx, rows_per_mod, *, tm, name, a=None, w=None, y=None, ret=None):
    t, d = x.shape
    assert t % tm == 0
    if rows_per_mod is None:
        mod_row = lambda i: 0
    else:
        assert rows_per_mod % tm == 0
        tiles_per_mod = rows_per_mod // tm
        mod_row = lambda i: i // tiles_per_mod
    row = lambda width: pl.BlockSpec((tm, width), lambda i: (i, 0))
    tail_specs = [row(d), pl.BlockSpec((1, 1, d), lambda i: (mod_row(i), 0, gate_idx)),
                  pl.BlockSpec((1, d), lambda i: (0, 0))]
    tail_args = [x, mod, g.reshape(1, d)]
    if ret is not None:
        mode = "ret"
        o_f, o_b, qkvg = ret
        vw = RET_HEADS * RET_VAL
        gate_col = qkvg.shape[1] // vw - 1
        in_specs = [row(vw), row(vw), pl.BlockSpec((tm, vw), lambda i: (i, gate_col)),
                    pl.BlockSpec(w.shape, lambda i: (0, 0))]
        args = [o_f, o_b, qkvg, w]
    elif a is not None:
        mode = "matmul"
        in_specs = [row(a.shape[1]), pl.BlockSpec(w.shape, lambda i: (0, 0))]
        args = [a, w]
    else:
        mode = "plain"
        in_specs = [row(d)]
        args = [y]
    return pl.pallas_call(
        functools.partial(_resid_kernel, mode=mode),
        grid=(t // tm,),
        in_specs=in_specs + tail_specs,
        out_specs=row(d),
        out_shape=jax.ShapeDtypeStruct((t, d), F32),
        compiler_params=_cparams(("parallel",)),
        name=name,
    )(*args, *tail_args)


DA_SUM_ROWS = 16
DA_TQ = 256
DA_EXP_ROWS = 64


def _da_kernel(*refs, n_seg, post_scale, n_tiles, nq):
    lam_ref, q_ref = refs[0], refs[1]
    k_refs = refs[2:2 + n_seg]
    v_refs = refs[2 + n_seg:2 + 2 * n_seg]
    sg_ref, o_ref, k_scr, vt_scr, s_a, s_b, p_a, p_b, mx_a, mx_b = refs[2 + 2 * n_seg:]
    hw = 2 * DA_HEAD
    n_keys = vt_scr.shape[1]
    tq = q_ref.shape[1]
    g = pl.program_id(0)
    t_qk = jnp.minimum(g, n_tiles - 1)
    t_pv = jnp.clip(g - 2, 0, n_tiles - 1)

    @pl.when(g == 0)
    def _():
        for buf in (s_a, s_b, p_a, p_b, mx_a, mx_b):
            buf[...] = jnp.zeros_like(buf)

    @pl.when(lax.rem(t_qk, nq) == 0)
    def _():
        off = 0
        for kr in k_refs:
            n = kr.shape[1]
            k_scr[off:off + n, :] = kr[0]
            off += n

    @pl.when(lax.rem(t_pv, nq) == 0)
    def _():
        off = 0
        for vr in v_refs:
            n = vr.shape[1]
            vt_scr[0:hw, off:off + n] = vr[0].astype(F32).T.astype(BF16)
            off += n
        row = lax.broadcasted_iota(jnp.int32, (DA_SUM_ROWS, n_keys), 0)
        vt_scr[hw:hw + DA_SUM_ROWS, :] = jnp.where(row == 0, 1.0, 0.0).astype(BF16)

    lane = lax.broadcasted_iota(jnp.int32, (1, hw), 1)
    qscale = DA_HEAD ** -0.5 * math.log2(math.e)

    def stages(s_new, mx_new, s_old, mx_old, p_old, p_older):
        accs = []
        for m in range(2):
            accs.append(_dot(vt_scr[...], p_older[m]))
        q = q_ref[0].astype(F32)
        for m in range(2):
            sel = (lane < DA_HEAD) if m == 0 else (lane >= DA_HEAD)
            qm = jnp.where(sel, q * qscale, 0.0).astype(BF16)
            s_t = _dot_nt(k_scr[...], qm)
            s_new[m] = s_t
            part = [None] * 4
            for r in range(n_keys // 8):
                tile = s_t[r * 8:(r + 1) * 8, :]
                j = r % len(part)
                part[j] = tile if part[j] is None else jnp.maximum(part[j], tile)
            mx_new[m] = functools.reduce(jnp.maximum, part)
            mx8 = jnp.broadcast_to(jnp.max(mx_old[m], axis=0, keepdims=True), (8, tq))
            for c in range(n_keys // DA_EXP_ROWS):
                rows = slice(c * DA_EXP_ROWS, (c + 1) * DA_EXP_ROWS)
                zero = jnp.minimum(jnp.abs(s_t[c * DA_EXP_ROWS:c * DA_EXP_ROWS + 8, :]), 0.0)
                d = s_old[m, rows, :].reshape(DA_EXP_ROWS // 8, 8, tq) - (mx8 + zero)[None]
                p_old[m, rows, :] = jnp.exp2(d.reshape(DA_EXP_ROWS, tq).astype(BF16))
        a0, a1 = accs
        o_t = a0[:hw] * (1.0 / a0[hw:hw + 1]) - a1[:hw] * (lam_ref[0, 0] / a1[hw:hw + 1])
        o_t = o_t * lax.rsqrt(jnp.mean(o_t * o_t, axis=0, keepdims=True) + NORM_EPS) * (sg_ref[...] * post_scale)
        o_ref[0] = o_t.T.astype(o_ref.dtype)

    @pl.when(lax.rem(g, 2) == 0)
    def _():
        stages(s_a, mx_a, s_b, mx_b, p_b, p_a)

    @pl.when(lax.rem(g, 2) == 1)
    def _():
        stages(s_b, mx_b, s_a, mx_a, p_a, p_b)


def _diff_attention(q_arr, kv_arrs, lam, subln_g, lam_init, *, tq, name):
    b, lq, _ = q_arr.shape
    hw = 2 * DA_HEAD
    n_seg = len(kv_arrs)
    n_keys = sum(kv.shape[1] for kv in kv_arrs)
    assert lq % tq == 0 and n_keys % DA_EXP_ROWS == 0
    nq = lq // tq
    n_tiles = b * DA_HEADS * nq

    def tile(t):
        return t // (nq * DA_HEADS), lax.rem(t // nq, DA_HEADS), lax.rem(t, nq)

    def qk_tile(g):
        return tile(jnp.minimum(g, n_tiles - 1))

    def pv_tile(g):
        return tile(jnp.clip(g - 2, 0, n_tiles - 1))

    in_specs = [pl.BlockSpec(memory_space=pltpu.SMEM),
                pl.BlockSpec((1, tq, hw), lambda g: (qk_tile(g)[0], qk_tile(g)[2], qk_tile(g)[1]))]
    in_specs += [pl.BlockSpec((1, kv.shape[1], hw), lambda g: (qk_tile(g)[0], 0, DA_HEADS + qk_tile(g)[1]))
                 for kv in kv_arrs]
    in_specs += [pl.BlockSpec((1, kv.shape[1], hw), lambda g: (pv_tile(g)[0], 0, 2 * DA_HEADS + pv_tile(g)[1]))
                 for kv in kv_arrs]
    in_specs += [pl.BlockSpec((hw, 1), lambda g: (0, 0))]
    s_buf = pltpu.VMEM((2, n_keys, tq), F32)
    p_buf = pltpu.VMEM((2, n_keys, tq), BF16)
    mx_buf = pltpu.VMEM((2, 8, tq), F32)
    return pl.pallas_call(
        functools.partial(_da_kernel, n_seg=n_seg, post_scale=1.0 - lam_init, n_tiles=n_tiles, nq=nq),
        grid=(n_tiles + 2,),
        in_specs=in_specs,
        out_specs=pl.BlockSpec((1, tq, hw), lambda g: (pv_tile(g)[0], pv_tile(g)[2], pv_tile(g)[1])),
        out_shape=jax.ShapeDtypeStruct((b, lq, D_MODEL), BF16),
        scratch_shapes=[pltpu.VMEM((n_keys, hw), BF16), pltpu.VMEM((hw + DA_SUM_ROWS, n_keys), BF16),
                        s_buf, s_buf, p_buf, p_buf, mx_buf, mx_buf],
        compiler_params=_cparams(("arbitrary",)),
        name=name,
    )(lam.reshape(1, 1), q_arr, *kv_arrs, *kv_arrs, subln_g.reshape(hw, 1))


def _wa_kernel(*refs, windowed, seq_len):
    if windowed:
        sink_ref, q_ref, kc_ref, vc_ref, kp_ref, kq_ref, kn_ref, vp_ref, vq_ref, vn_ref, o_ref = refs
        k_all = jnp.concatenate([kc_ref[0], kp_ref[0], kq_ref[0], kn_ref[0]], axis=0)
        v_all = jnp.concatenate([vc_ref[0], vp_ref[0], vq_ref[0], vn_ref[0]], axis=0)
    else:
        sink_ref, q_ref, kc_ref, vc_ref, o_ref = refs
        k_all, v_all = kc_ref[0], vc_ref[0]
    tq = q_ref.shape[1]
    n_keys = k_all.shape[0]
    n_ctx = kc_ref.shape[1]
    i = pl.program_id(1)
    if windowed:
        qpos = i * tq + lax.broadcasted_iota(jnp.int32, (tq, n_keys), 0)
        col = lax.broadcasted_iota(jnp.int32, (tq, n_keys), 1)
        kpos = (i - 1) * tq + col - n_ctx
        ok = (col < n_ctx) | ((jnp.abs(qpos - kpos) <= WINDOW) & (kpos >= 0) & (kpos < seq_len))
    lane = lax.broadcasted_iota(jnp.int32, (1, LANES), 1)
    lo = lane < WA_HEAD
    scale = jnp.asarray(WA_HEAD ** -0.5, BF16)
    rep = WA_Q_HEADS // WA_KV_HEADS
    for g in range(WA_KV_HEADS):
        kg = k_all[:, (g // 2) * LANES:(g // 2 + 1) * LANES]
        vg = v_all[:, (g // 2) * LANES:(g // 2 + 1) * LANES]
        zero = jnp.zeros_like(kg)
        if g % 2 == 0:
            kg_same, vg_same = jnp.where(lo, kg, zero), jnp.where(lo, vg, zero)
        else:
            kg_same, vg_same = jnp.where(lo, zero, kg), jnp.where(lo, zero, vg)
        kg_swap = pltpu.roll(kg_same.astype(F32), WA_HEAD, 1).astype(BF16)
        vg_swap = pltpu.roll(vg_same.astype(F32), WA_HEAD, 1).astype(BF16)
        k_half = (kg_same, kg_swap) if g % 2 == 0 else (kg_swap, kg_same)
        v_half = (vg_same, vg_swap) if g % 2 == 0 else (vg_swap, vg_same)
        for pair in range(rep // 2):
            chunk = g * (rep // 2) + pair
            q2 = q_ref[0, :, chunk * LANES:(chunk + 1) * LANES] * scale
            out = None
            for half in range(2):
                head = 2 * chunk + half
                s = _dot_nt(q2, k_half[half])
                if windowed:
                    s = jnp.where(ok, s, -1e30)
                snk = sink_ref[head]
                mx = jnp.maximum(jnp.max(s, axis=-1, keepdims=True), snk)
                p = jnp.exp(s - mx)
                l = jnp.sum(p, axis=-1, keepdims=True) + jnp.exp(snk - mx)
                part = _dot((p * (1.0 / l)).astype(BF16), v_half[half])
                out = part if out is None else out + part
            o_ref[0, :, chunk * LANES:(chunk + 1) * LANES] = out.astype(o_ref.dtype)


def _window_attention(q_arr, ctx_arr, sink, *, windowed, name):
    b, lq, _ = q_arr.shape
    n_ctx = ctx_arr.shape[1]
    qw = WA_Q_HEADS * WA_HEAD
    kw = WA_KV_HEADS * WA_HEAD
    k_col, v_col = qw // kw, qw // kw + 1
    tq = 128 if windowed else lq
    nq = lq // tq
    in_specs = [pl.BlockSpec(memory_space=pltpu.SMEM),
                pl.BlockSpec((1, tq, qw), lambda bi, i: (bi, i, 0)),
                pl.BlockSpec((1, n_ctx, kw), lambda bi, i: (bi, 0, k_col)),
                pl.BlockSpec((1, n_ctx, kw), lambda bi, i: (bi, 0, v_col))]
    args = [sink, q_arr, ctx_arr, ctx_arr]
    if windowed:
        assert tq == WINDOW
        for col in (k_col, v_col):
            in_specs += [pl.BlockSpec((1, tq, kw), lambda bi, i, col=col: (bi, jnp.maximum(i - 1, 0), col)),
                         pl.BlockSpec((1, tq, kw), lambda bi, i, col=col: (bi, i, col)),
                         pl.BlockSpec((1, tq, kw), lambda bi, i, col=col: (bi, jnp.minimum(i + 1, nq - 1), col))]
            args += [q_arr, q_arr, q_arr]
    return pl.pallas_call(
        functools.partial(_wa_kernel, windowed=windowed, seq_len=lq),
        grid=(b, nq),
        in_specs=in_specs,
        out_specs=pl.BlockSpec((1, tq, qw), lambda bi, i: (bi, i, 0)),
        out_shape=jax.ShapeDtypeStruct((b, lq, qw), BF16),
        compiler_params=_cparams(("parallel", "parallel")),
        name=name,
    )(*args)


def _ret_kernel(cd_ref, qcf, kcf, vcf, qlf, klf, vlf, qcb, kcb, vcb, qlb, klb, vlb, dtab_ref, qd_ref, kd_ref,
                ocf, olf, ocb, olb, s_scr, *, n_ctx_chunks):
    j = pl.program_id(1)
    is_ctx = j < n_ctx_chunks

    @pl.when(j == 0)
    def _():
        s_scr[...] = jnp.zeros_like(s_scr)

    kscale = jnp.asarray(RET_KEY ** -0.5, BF16)
    dirs = ((qcf, kcf, vcf, qlf, klf, vlf, ocf, olf), (qcb, kcb, vcb, qlb, klb, vlb, ocb, olb))
    for d, (qc, kc, vc, ql, kl, vl, oc, ol) in enumerate(dirs):
        q = jnp.where(is_ctx, qc[0], ql[0])
        k = jnp.where(is_ctx, kc[0], kl[0]) * kscale
        v = jnp.where(is_ctx, vc[0], vl[0])
        outs = []
        for h in range(RET_HEADS):
            qh = q[:, h * RET_KEY:(h + 1) * RET_KEY]
            kh = k[:, h * RET_KEY:(h + 1) * RET_KEY]
            vh = v[:, h * RET_VAL:(h + 1) * RET_VAL]
            state = s_scr[d, h]
            inner = _dot_nt(qh, kh) * dtab_ref[d, h]
            o = _dot(inner.astype(BF16), vh) + _dot(qh, state.astype(BF16)) * qd_ref[d, h]
            kdec = (kh.astype(F32) * kd_ref[d, h]).astype(BF16)
            s_scr[d, h] = state * cd_ref[d, h] + _dot_tn(kdec, vh)
            outs.append(o)
        o_all = jnp.concatenate(outs, axis=1).astype(oc.dtype)

        @pl.when(is_ctx)
        def _():
            oc[0] = o_all

        @pl.when(jnp.logical_not(is_ctx))
        def _():
            ol[0] = o_all


def _retention(ctx_arr, lat_arr, log_g):
    b, lc, _ = ctx_arr.shape
    l = lat_arr.shape[1]
    c = RET_CHUNK
    nc, nl = lc // c, l // c
    qw = RET_HEADS * RET_KEY
    vw = RET_HEADS * RET_VAL
    idx = jnp.arange(c, dtype=F32)
    rel = idx[:, None] - idx[None, :]
    lg = log_g[:, :, None, None]
    keep_f = (rel >= 0)[None]
    keep_b = (rel < 0)[None]
    d_f = jnp.where(keep_f, jnp.exp(jnp.where(keep_f, rel[None] * lg[0], 0.0)), 0.0)
    d_b = jnp.where(keep_b, jnp.exp(jnp.where(keep_b, -rel[None] * lg[1], 0.0)), 0.0)
    dtab = jnp.stack([d_f, d_b])
    col = idx[None, :, None]
    qd = jnp.stack([jnp.exp((col + 1.0) * lg[0]), jnp.exp((c - col) * lg[1])])
    kd = jnp.stack([jnp.exp((c - 1.0 - col) * lg[0]), jnp.exp(col * lg[1])])
    cd = jnp.exp(c * log_g)

    f_ctx = lambda bi, j: (bi, jnp.minimum(j, nc - 1))
    f_lat = lambda bi, j: (bi, jnp.maximum(j - nc, 0))
    b_ctx = lambda bi, j: (bi, jnp.maximum(nc - 1 - j, 0))
    b_lat = lambda bi, j: (bi, jnp.minimum(nc + nl - 1 - j, nl - 1))

    def qkv_specs(row_map):
        return [pl.BlockSpec((1, c, qw), lambda bi, j: row_map(bi, j) + (0,)),
                pl.BlockSpec((1, c, qw), lambda bi, j: row_map(bi, j) + (1,)),
                pl.BlockSpec((1, c, vw), lambda bi, j: row_map(bi, j) + (1,))]

    full = lambda shape: pl.BlockSpec(shape, lambda bi, j: (0,) * len(shape))
    in_specs = ([pl.BlockSpec(memory_space=pltpu.SMEM)]
                + qkv_specs(f_ctx) + qkv_specs(f_lat) + qkv_specs(b_ctx) + qkv_specs(b_lat)
                + [full(dtab.shape), full(qd.shape), full(kd.shape)])
    out_spec = lambda row_map: pl.BlockSpec((1, c, vw), lambda bi, j: row_map(bi, j) + (0,))
    out_sds = lambda n: jax.ShapeDtypeStruct((b, n, vw), BF16)
    return pl.pallas_call(
        functools.partial(_ret_kernel, n_ctx_chunks=nc),
        grid=(b, nc + nl),
        in_specs=in_specs,
        out_specs=[out_spec(f_ctx), out_spec(f_lat), out_spec(b_ctx), out_spec(b_lat)],
        out_shape=[out_sds(lc), out_sds(l), out_sds(lc), out_sds(l)],
        scratch_shapes=[pltpu.VMEM((2, RET_HEADS, RET_KEY, RET_VAL), F32)],
        compiler_params=_cparams(("parallel", "arbitrary")),
        name="retention_scan",
    )(cd, ctx_arr, ctx_arr, ctx_arr, lat_arr, lat_arr, lat_arr, ctx_arr, ctx_arr, ctx_arr, lat_arr, lat_arr,
      lat_arr, dtab, qd, kd)


def _router_kernel(x_ref, g_ref, sh_ref, sc_ref, wr_ref, a_ref, lg_ref):
    a = _modulated(x_ref, g_ref, sh_ref, sc_ref)
    a_hi = a.astype(BF16)
    a_ref[...] = a_hi
    a_lo = (a - a_hi.astype(F32)).astype(BF16)
    w = wr_ref[...]
    w_hi = w.astype(BF16)
    w_lo = (w - w_hi.astype(F32)).astype(BF16)
    lg_ref[...] = _dot(a_hi, w_hi) + _dot(a_lo, w_hi) + _dot(a_hi, w_lo)


def _router(x, g, mod, sh_idx, rows_per_mod, w_router_pad, *, tm, name):
    t, d = x.shape
    if rows_per_mod is None:
        mod_row = lambda i: 0
    else:
        tiles_per_mod = rows_per_mod // tm
        mod_row = lambda i: i // tiles_per_mod
    return pl.pallas_call(
        _router_kernel,
        grid=(t // tm,),
        in_specs=[pl.BlockSpec((tm, d), lambda i: (i, 0)),
                  pl.BlockSpec((1, d), lambda i: (0, 0)),
                  pl.BlockSpec((1, 1, d), lambda i: (mod_row(i), 0, sh_idx)),
                  pl.BlockSpec((1, 1, d), lambda i: (mod_row(i), 0, sh_idx + 1)),
                  pl.BlockSpec((d, ROUTER_PAD), lambda i: (0, 0))],
        out_specs=[pl.BlockSpec((tm, d), lambda i: (i, 0)), pl.BlockSpec((tm, ROUTER_PAD), lambda i: (i, 0))],
        out_shape=[jax.ShapeDtypeStruct((t, d), BF16), jax.ShapeDtypeStruct((t, ROUTER_PAD), F32)],
        compiler_params=_cparams(("parallel",)),
        name=name,
    )(x, g.reshape(1, d), mod, mod, w_router_pad)


def _experts_kernel(be_ref, nb_ref, x_ref, wg_ref, wu_ref, wd_ref, o_ref, acc, *, nf):
    blk = pl.program_id(0)
    f = pl.program_id(1)
    used = blk < nb_ref[0]

    @pl.when(used)
    def _():
        x = x_ref[...]
        g = _dot(x, wg_ref[0])
        u = _dot(x, wu_ref[0])
        part = _dot((g * _sigmoid(g) * u).astype(BF16), wd_ref[0])

        @pl.when(f == 0)
        def _():
            acc[...] = part

        @pl.when(f > 0)
        def _():
            acc[...] += part

    @pl.when(f == nf - 1)
    def _():
        o_ref[...] = jnp.where(used, acc[...], 0.0).astype(o_ref.dtype)


def _experts(buf, block_expert, n_used, w_gu, w_down):
    rows, d = buf.shape
    nb = rows // MOE_ROWS
    e, _, two_f = w_gu.shape
    fdim = two_f // 2
    fc = _pick(fdim, (MOE_FC, 256, 128))
    nf = fdim // fc
    grid_spec = pltpu.PrefetchScalarGridSpec(
        num_scalar_prefetch=2,
        grid=(nb, nf),
        in_specs=[pl.BlockSpec((MOE_ROWS, d), lambda i, f, be, nu: (i, 0)),
                  pl.BlockSpec((1, d, fc), lambda i, f, be, nu: (be[i], 0, f)),
                  pl.BlockSpec((1, d, fc), lambda i, f, be, nu: (be[i], 0, nf + f)),
                  pl.BlockSpec((1, fc, d), lambda i, f, be, nu: (be[i], f, 0))],
        out_specs=pl.BlockSpec((MOE_ROWS, d), lambda i, f, be, nu: (i, 0)),
        scratch_shapes=[pltpu.VMEM((MOE_ROWS, d), F32)],
    )
    return pl.pallas_call(
        functools.partial(_experts_kernel, nf=nf),
        grid_spec=grid_spec,
        out_shape=jax.ShapeDtypeStruct((rows, d), BF16),
        compiler_params=_cparams(("parallel", "arbitrary")),
        name="moe_experts",
    )(block_expert, n_used, buf, w_gu, w_gu, w_down)


def _moe(a_all, logits, w_gu, w_down):
    t, d = a_all.shape
    e = w_gu.shape[0]
    top_val, top_idx = lax.top_k(logits, TOP_K)
    gates = jax.nn.softmax(top_val, axis=-1)
    onehot = jnp.sum((top_idx[..., None] == jnp.arange(e)[None, None, :]).astype(jnp.int32), axis=1)
    counts = jnp.sum(onehot, axis=0)
    rank = jnp.cumsum(onehot, axis=0) - onehot
    padded = (counts + MOE_ROWS - 1) // MOE_ROWS * MOE_ROWS
    pend = jnp.cumsum(padded)
    pstart = pend - padded
    dest = pstart[top_idx] + jnp.take_along_axis(rank, top_idx, axis=1)
    nb = -(-(t * TOP_K) // MOE_ROWS) + e
    tok = jnp.repeat(jnp.arange(t, dtype=jnp.int32), TOP_K)
    row_src = jnp.zeros((nb * MOE_ROWS,), jnp.int32).at[dest.reshape(-1)].set(tok)
    buf = a_all[row_src]
    block_expert = jnp.minimum(jnp.searchsorted(pend, jnp.arange(nb) * MOE_ROWS, side="right"), e - 1).astype(jnp.int32)
    n_used = (pend[-1] // MOE_ROWS).astype(jnp.int32).reshape(1)
    yb = _experts(buf, block_expert, n_used, w_gu, w_down)
    return (yb[dest[:, 0]].astype(F32) * gates[:, 0:1] + yb[dest[:, 1]].astype(F32) * gates[:, 1:2])


def _rope_angles(pos, dim):
    inv = ROPE_BASE ** (-jnp.arange(0, dim, 2, dtype=F32) / dim)
    ang = pos.astype(F32)[:, None] * inv[None, :]
    return jnp.concatenate([ang, ang], axis=-1)


def _signed_sin(sin, shift):
    low = (jnp.arange(sin.shape[1]) % (2 * shift)) < shift
    return jnp.where(low[None, :], -sin, 0.0), jnp.where(low[None, :], 0.0, sin)


def _axial_tables(n_tokens, head_dim):
    rows = n_tokens // GRID_W
    row = jnp.repeat(jnp.arange(rows, dtype=jnp.int32), GRID_W)
    col = jnp.tile(jnp.arange(GRID_W, dtype=jnp.int32), rows)
    half = head_dim // 2
    ang = jnp.concatenate([_rope_angles(row, half), _rope_angles(col, half)], axis=-1)
    ang = jnp.tile(ang, (1, LANES // head_dim))
    sa, sb = _signed_sin(jnp.sin(ang), half // 2)
    return jnp.cos(ang), sa, sb, half // 2


def _ret_tables(n_tokens):
    ang = _rope_angles(jnp.arange(n_tokens), RET_KEY)
    sa, sb = _signed_sin(jnp.sin(ang), RET_KEY // 2)
    return jnp.cos(ang), sa, sb, RET_KEY // 2


def kernel(x, c, ctx, c_ctx, mod_w, mod_b, norm_g, da_w_in, da_w_out, da_lambda, da_subln_g, wa_w_in, wa_w_out,
           wa_sink, ret_w_in, ret_w_out, ret_decay_logit, ffn_w_gu, ffn_w_down, moe_router, moe_w_gu, moe_w_down):
    b, l, d = x.shape
    lc = ctx.shape[1]
    depth = mod_w.shape[0]
    t, tc = b * l, b * lc
    tm = _pick(l, (1024, 512, 256, 128))
    tmc = _pick(tc, (1024, 512, 256, 128))

    n_cond = -(-(b + 1) // 8) * 8
    cs = jnp.zeros((n_cond, d), F32).at[:b].set(c).at[b].set(c_ctx)
    mod_all = _mod_vectors(cs, mod_w, mod_b)

    xs = x.reshape(t, d)
    hs = ctx.reshape(tc, d)
    mixer_count = [0] * N_MIXERS
    i_dense = 0
    i_moe = 0
    for layer in range(depth):
        last = layer == depth - 1
        mod = mod_all[layer, :b].reshape(b, 1, 6 * d)
        mod_c = mod_all[layer, b:b + 1].reshape(1, 1, 6 * d)
        ng = norm_g[layer]
        kind = layer % N_MIXERS
        jm = mixer_count[kind]
        mixer_count[kind] += 1

        if kind == 0:
            lam_init = 0.8 - 0.6 * math.exp(-0.3 * layer)
            w_in = da_w_in[jm].astype(BF16)
            cos, sa, sb, shift = _axial_tables(l, DA_HEAD)
            qkv = _norm_proj(xs, ng[0], mod, 0, l, w_in, tm=tm, tn=512, rot=(cos, sa, sb, 2 * d, shift, l),
                             name="da_in_proj").reshape(b, l, 3 * d)
            qkv_c = _norm_proj(hs, ng[0], mod_c, 0, None, w_in, tm=tmc, tn=512, name="da_in_proj_ctx").reshape(b, lc, 3 * d)
            lp = da_lambda[jm].astype(F32)
            lam = jnp.exp(jnp.sum(lp[0] * lp[1])) - jnp.exp(jnp.sum(lp[2] * lp[3])) + lam_init
            o = _diff_attention(qkv, [qkv_c, qkv], lam, da_subln_g[jm], lam_init, tq=_pick(l, (DA_TQ, 128)), name="diff_attn")
            w_out = da_w_out[jm].astype(BF16)
            xs = _resid(xs, ng[1], mod, 2, l, tm=tm, a=o.reshape(t, d), w=w_out, name="da_out_proj")
            if not last:
                oc = _diff_attention(qkv_c, [qkv_c], lam, da_subln_g[jm], lam_init, tq=_pick(lc, (DA_TQ, 128)), name="diff_attn_ctx")
                hs = _resid(hs, ng[1], mod_c, 2, None, tm=tmc, a=oc.reshape(tc, d), w=w_out, name="da_out_proj_ctx")
        elif kind == 1:
            w_in = wa_w_in[jm].astype(BF16)
            n_qkv = w_in.shape[1]
            cos, sa, sb, shift = _axial_tables(l, WA_HEAD)
            n_rot = (WA_Q_HEADS + WA_KV_HEADS) * WA_HEAD
            qkv = _norm_proj(xs, ng[0], mod, 0, l, w_in, tm=tm, tn=256, rot=(cos, sa, sb, n_rot, shift, l),
                             name="wa_in_proj").reshape(b, l, n_qkv)
            qkv_c = _norm_proj(hs, ng[0], mod_c, 0, None, w_in, tm=tmc, tn=256, name="wa_in_proj_ctx").reshape(b, lc, n_qkv)
            o = _window_attention(qkv, qkv_c, wa_sink[jm], windowed=True, name="window_attn")
            w_out = wa_w_out[jm].astype(BF16)
            xs = _resid(xs, ng[1], mod, 2, l, tm=tm, a=o.reshape(t, d), w=w_out, name="wa_out_proj")
            if not last:
                oc = _window_attention(qkv_c, qkv_c, wa_sink[jm], windowed=False, name="window_attn_ctx")
                hs = _resid(hs, ng[1], mod_c, 2, None, tm=tmc, a=oc.reshape(tc, d), w=w_out, name="wa_out_proj_ctx")
        else:
            w_in = ret_w_in[jm].astype(BF16)
            n_qkv = w_in.shape[1]
            cos, sa, sb, shift = _ret_tables(l)
            n_rot = 2 * RET_HEADS * RET_KEY
            qkv = _norm_proj(xs, ng[0], mod, 0, l, w_in, tm=tm, tn=512, rot=(cos, sa, sb, n_rot, shift, l),
                             name="ret_in_proj")
            qkv_c = _norm_proj(hs, ng[0], mod_c, 0, None, w_in, tm=tmc, tn=512, name="ret_in_proj_ctx")
            log_g = jax.nn.log_sigmoid(ret_decay_logit[jm].astype(F32))
            ocf, olf, ocb, olb = _retention(qkv_c.reshape(b, lc, n_qkv), qkv.reshape(b, l, n_qkv), log_g)
            w_out = ret_w_out[jm].astype(BF16)
            vw = RET_HEADS * RET_VAL
            xs = _resid(xs, ng[1], mod, 2, l, tm=_pick(l, (512, 256, 128)), w=w_out,
                        ret=(olf.reshape(t, vw), olb.reshape(t, vw), qkv), name="ret_out_proj")
            if not last:
                hs = _resid(hs, ng[1], mod_c, 2, None, tm=_pick(tc, (512, 256, 128)), w=w_out,
                            ret=(ocf.reshape(tc, vw), ocb.reshape(tc, vw), qkv_c), name="ret_out_proj_ctx")

        if layer % 2 == 0:
            w_gu = ffn_w_gu[i_dense].astype(BF16)
            w_dn = ffn_w_down[i_dense].astype(BF16)
            i_dense += 1
            f = w_dn.shape[0]
            w_g, w_u = w_gu[:, :f], w_gu[:, f:]
            tn = _pick(f, (1408, 1024, 512, 256, 128))
            tf = _pick(l, (512, 256, 128))
            act = _norm_proj(xs, ng[2], mod, 3, l, (w_g, w_u), tm=tf, tn=tn, name="ffn_up")
            xs = _resid(xs, ng[3], mod, 5, l, tm=tf, a=act, w=w_dn, name="ffn_down")
            if not last:
                tfc = _pick(tc, (512, 256, 128))
                act_c = _norm_proj(hs, ng[2], mod_c, 3, None, (w_g, w_u), tm=tfc, tn=tn, name="ffn_up_ctx")
                hs = _resid(hs, ng[3], mod_c, 5, None, tm=tfc, a=act_c, w=w_dn, name="ffn_down_ctx")
        else:
            w_r = jnp.zeros((d, ROUTER_PAD), F32).at[:, :N_EXPERTS].set(moe_router[i_moe])
            w_gu = moe_w_gu[i_moe].astype(BF16)
            w_dn = moe_w_down[i_moe].astype(BF16)
            i_moe += 1
            a_x, lg_x = _router(xs, ng[2], mod, 3, l, w_r, tm=tm, name="moe_router")
            if not last:
                a_c, lg_c = _router(hs, ng[2], mod_c, 3, None, w_r, tm=tmc, name="moe_router_ctx")
                a_all = jnp.concatenate([a_x, a_c], axis=0)
                lg_all = jnp.concatenate([lg_x, lg_c], axis=0)
            else:
                a_all, lg_all = a_x, lg_x
            y_all = _moe(a_all, lg_all[:, :N_EXPERTS], w_gu, w_dn)
            xs = _resid(xs, ng[3], mod, 5, l, tm=tm, y=y_all[:t], name="moe_combine")
            if not last:
                hs = _resid(hs, ng[3], mod_c, 5, None, tm=tmc, y=y_all[t:], name="moe_combine_ctx")
    return xs.reshape(b, l, d)
```

```python
import functools
import math

import jax
import jax.numpy as jnp
from jax import lax
from jax.experimental import pallas as pl
from jax.experimental.pallas import tpu as pltpu

F32 = jnp.float32
BF16 = jnp.bfloat16

D_MODEL = 1024
N_MIXERS = 3
NORM_EPS = 1e-6
ROPE_BASE = 10000.0
GRID_W = 64
DA_HEAD = 64
DA_HEADS = D_MODEL // (2 * DA_HEAD)
WA_HEAD = 64
WA_Q_HEADS = D_MODEL // WA_HEAD
WA_KV_HEADS = WA_Q_HEADS // 4
WINDOW = 128
RET_KEY = 256
RET_HEADS = D_MODEL // RET_KEY
RET_VAL = 2 * RET_KEY
RET_CHUNK = 128
N_EXPERTS = 8
TOP_K = 2

LANES = 128
V7X_VMEM_LIMIT = 56 * 1024 * 1024
ROUTER_PAD = LANES
MOE_ROWS = 1024
MOE_FC = 512
MOE_ROW_CHUNK = 256
PROJ_COLS = 512


def _cparams(sem):
    return pltpu.CompilerParams(dimension_semantics=sem, vmem_limit_bytes=V7X_VMEM_LIMIT)


def _pick(n, prefs):
    for p in prefs:
        if n % p == 0:
            return p
    return n


def _sigmoid(x):
    return 1.0 / (1.0 + jnp.exp(-x))


def _rms(y):
    return y * lax.rsqrt(jnp.mean(y * y, axis=-1, keepdims=True) + NORM_EPS)


def _dot(a, b):
    return jnp.dot(a, b, preferred_element_type=F32)


def _dot_nt(a, b):
    return lax.dot_general(a, b, (((1,), (1,)), ((), ())), preferred_element_type=F32)


def _dot_tn(a, b):
    return lax.dot_general(a, b, (((0,), (0,)), ((), ())), preferred_element_type=F32)


def _mod_kernel(c_ref, w_ref, b_ref, o_ref):
    cs = c_ref[...]
    s = cs * _sigmoid(cs)
    w = w_ref[0]
    s_hi = s.astype(BF16)
    s_lo = (s - s_hi.astype(F32)).astype(BF16)
    w_hi = w.astype(BF16)
    w_lo = (w - w_hi.astype(F32)).astype(BF16)
    o_ref[0] = _dot(s_hi, w_hi) + _dot(s_lo, w_hi) + _dot(s_hi, w_lo) + b_ref[0]


def _mod_vectors(cs, mod_w, mod_b):
    depth, d, n = mod_w.shape
    r = cs.shape[0]
    tn = _pick(n, (1024, 512, 256, 128))
    return pl.pallas_call(
        _mod_kernel,
        grid=(depth, n // tn),
        in_specs=[
            pl.BlockSpec((r, d), lambda l, j: (0, 0)),
            pl.BlockSpec((1, d, tn), lambda l, j: (l, 0, j)),
            pl.BlockSpec((1, 1, tn), lambda l, j: (l, 0, j)),
        ],
        out_specs=pl.BlockSpec((1, r, tn), lambda l, j: (l, 0, j)),
        out_shape=jax.ShapeDtypeStruct((depth, r, n), F32),
        compiler_params=_cparams(("parallel", "parallel")),
        name="mod_vectors",
    )(cs, mod_w, mod_b.reshape(depth, 1, n))


def _modulated(x_ref, g_ref, sh_ref, sc_ref):
    x = x_ref[...]
    return _rms(x) * g_ref[...] * (1.0 + sc_ref[0]) + sh_ref[0]


def _rotate(acc, cos, sa, sb, shift, cw):
    outs = []
    for c in range(acc.shape[1] // cw):
        y = acc[:, c * cw:(c + 1) * cw]
        up = pltpu.roll(y, cw - shift, 1)
        dn = pltpu.roll(y, shift, 1)
        outs.append(y * cos + up * sa + dn * sb)
    return outs[0] if len(outs) == 1 else jnp.concatenate(outs, axis=1)


def _col_chunks(width, cuts):
    edges = sorted({0, width} | set(range(PROJ_COLS, width, PROJ_COLS)) | {c for c in cuts if 0 < c < width})
    return list(zip(edges[:-1], edges[1:]))


def _proj_kernel(*refs, mode, n_rot_cols, shift, cw, n_col_tiles):
    if mode == "swiglu":
        x_ref, g_ref, sh_ref, sc_ref, wg_ref, wu_ref, o_ref, a_scr = refs
    elif mode == "rot":
        x_ref, g_ref, sh_ref, sc_ref, w_ref, cos_ref, sa_ref, sb_ref, o_ref, a_scr = refs
    else:
        x_ref, g_ref, sh_ref, sc_ref, w_ref, o_ref, a_scr = refs
    j = pl.program_id(1)
    tn = o_ref.shape[1]

    @pl.when(j == 0)
    def _():
        a_scr[...] = _modulated(x_ref, g_ref, sh_ref, sc_ref).astype(BF16)

    def tile(jt):
        a = a_scr[...]
        for lo, hi in _col_chunks(tn, [n_rot_cols - jt * tn]):
            if mode == "swiglu":
                g = _dot(a, wg_ref[:, lo:hi])
                res = g * _sigmoid(g) * _dot(a, wu_ref[:, lo:hi])
            else:
                res = _dot(a, w_ref[:, lo:hi])
                if mode == "rot" and jt * tn + hi <= n_rot_cols:
                    res = _rotate(res, cos_ref[...], sa_ref[...], sb_ref[...], shift, cw)
            o_ref[:, lo:hi] = res.astype(o_ref.dtype)

    if mode != "rot":
        tile(0)
    else:
        n_rot_tiles = -(-n_rot_cols // tn)
        for jt in range(min(n_rot_tiles + 1, n_col_tiles)):
            @pl.when((j >= jt) if jt == n_rot_tiles else (j == jt))
            def _(jt=jt):
                tile(jt)


def _norm_proj(x, g, mod, sh_idx, rows_per_mod, weights, *, tm, tn, rot=None, name):
    t, d = x.shape
    swiglu = isinstance(weights, tuple)
    n = weights[0].shape[1] if swiglu else weights.shape[1]
    assert t % tm == 0 and n % tn == 0
    if rows_per_mod is None:
        mod_row = lambda i: 0
    else:
        assert rows_per_mod % tm == 0
        tiles_per_mod = rows_per_mod // tm
        mod_row = lambda i: i // tiles_per_mod
    in_specs = [
        pl.BlockSpec((tm, d), lambda i, j: (i, 0)),
        pl.BlockSpec((1, d), lambda i, j: (0, 0)),
        pl.BlockSpec((1, 1, d), lambda i, j: (mod_row(i), 0, sh_idx)),
        pl.BlockSpec((1, 1, d), lambda i, j: (mod_row(i), 0, sh_idx + 1)),
    ]
    args = [x, g.reshape(1, d), mod, mod]
    w_spec = pl.BlockSpec((d, tn), lambda i, j: (0, j))
    n_rot_cols, shift, cw = 0, 0, 0
    if swiglu:
        mode = "swiglu"
        in_specs += [w_spec, w_spec]
        args += list(weights)
    elif rot is not None:
        mode = "rot"
        cos, sa, sb, n_rot_cols, shift, seq_len = rot
        cw = cos.shape[1]
        assert n_rot_cols % cw == 0 and PROJ_COLS % cw == 0 and seq_len % tm == 0
        tiles_per_seq = seq_len // tm
        t_spec = pl.BlockSpec((tm, cw), lambda i, j: (i % tiles_per_seq, 0))
        in_specs += [w_spec, t_spec, t_spec, t_spec]
        args += [weights, cos, sa, sb]
    else:
        mode = "plain"
        in_specs += [w_spec]
        args += [weights]
    return pl.pallas_call(
        functools.partial(_proj_kernel, mode=mode, n_rot_cols=n_rot_cols, shift=shift, cw=cw, n_col_tiles=n // tn),
        grid=(t // tm, n // tn),
        in_specs=in_specs,
        out_specs=pl.BlockSpec((tm, tn), lambda i, j: (i, j)),
        out_shape=jax.ShapeDtypeStruct((t, n), BF16),
        scratch_shapes=[pltpu.VMEM((tm, d), BF16)],
        compiler_params=_cparams(("parallel", "arbitrary")),
        name=name,
    )(*args)


def _resid_kernel(*refs, mode):
    if mode == "ret":
        of_ref, ob_ref, gt_ref, w_ref, x_ref, gate_ref, g_ref, o_ref = refs
        o = of_ref[...].astype(F32) + ob_ref[...].astype(F32)
        parts = []
        for h in range(RET_HEADS):
            parts.append(_rms(o[:, h * RET_VAL:(h + 1) * RET_VAL]))
        o = jnp.concatenate(parts, axis=1)
        gt = gt_ref[...].astype(F32)
        y = _dot((gt * _sigmoid(gt) * o).astype(BF16), w_ref[...])
    elif mode == "matmul":
        a_ref, w_ref, x_ref, gate_ref, g_ref, o_ref = refs
        y = _dot(a_ref[...], w_ref[...])
    else:
        y_ref, x_ref, gate_ref, g_ref, o_ref = refs
        y = y_ref[...].astype(F32)
    o_ref[...] = x_ref[...] + gate_ref[0] * (_rms(y) * g_ref[...])


def _resid(x, g, mod, gate_idx, rows_per_mod, *, tm, name, a=None, w=None, y=None, ret=None):
    t, d = x.shape
    assert t % tm == 0
    if rows_per_mod is None:
        mod_row = lambda i: 0
    else:
        assert rows_per_mod % tm == 0
        tiles_per_mod = rows_per_mod // tm
        mod_row = lambda i: i // tiles_per_mod
    row = lambda width: pl.BlockSpec((tm, width), lambda i: (i, 0))
    tail_specs = [row(d), pl.BlockSpec((1, 1, d), lambda i: (mod_row(i), 0, gate_idx)),
                  pl.BlockSpec((1, d), lambda i: (0, 0))]
    tail_args = [x, mod, g.reshape(1, d)]
    if ret is not None:
        mode = "ret"
        o_f, o_b, qkvg = ret
        vw = RET_HEADS * RET_VAL
        gate_col = qkvg.shape[1] // vw - 1
        in_specs = [row(vw), row(vw), pl.BlockSpec((tm, vw), lambda i: (i, gate_col)),
                    pl.BlockSpec(w.shape, lambda i: (0, 0))]
        args = [o_f, o_b, qkvg, w]
    elif a is not None:
        mode = "matmul"
        in_specs = [row(a.shape[1]), pl.BlockSpec(w.shape, lambda i: (0, 0))]
        args = [a, w]
    else:
        mode = "plain"
        in_specs = [row(d)]
        args = [y]
    return pl.pallas_call(
        functools.partial(_resid_kernel, mode=mode),
        grid=(t // tm,),
        in_specs=in_specs + tail_specs,
        out_specs=row(d),
        out_shape=jax.ShapeDtypeStruct((t, d), F32),
        compiler_params=_cparams(("parallel",)),
        name=name,
    )(*args, *tail_args)


DA_SUM_ROWS = 16
DA_TQ = 256
DA_EXP_ROWS = 64


def _da_kernel(*refs, n_seg, post_scale, n_tiles, nq):
    lam_ref, q_ref = refs[0], refs[1]
    k_refs = refs[2:2 + n_seg]
    v_refs = refs[2 + n_seg:2 + 2 * n_seg]
    sg_ref, o_ref, k_scr, vt_scr, s_a, s_b, p_a, p_b, mx_a, mx_b = refs[2 + 2 * n_seg:]
    hw = 2 * DA_HEAD
    n_keys = vt_scr.shape[1]
    tq = q_ref.shape[1]
    g = pl.program_id(0)
    t_qk = jnp.minimum(g, n_tiles - 1)
    t_pv = jnp.clip(g - 2, 0, n_tiles - 1)

    @pl.when(g == 0)
    def _():
        for buf in (s_a, s_b, p_a, p_b, mx_a, mx_b):
            buf[...] = jnp.zeros_like(buf)

    @pl.when(lax.rem(t_qk, nq) == 0)
    def _():
        off = 0
        for kr in k_refs:
            n = kr.shape[1]
            k_scr[off:off + n, :] = kr[0]
            off += n

    @pl.when(lax.rem(t_pv, nq) == 0)
    def _():
        off = 0
        for vr in v_refs:
            n = vr.shape[1]
            vt_scr[0:hw, off:off + n] = vr[0].astype(F32).T.astype(BF16)
            off += n
        row = lax.broadcasted_iota(jnp.int32, (DA_SUM_ROWS, n_keys), 0)
        vt_scr[hw:hw + DA_SUM_ROWS, :] = jnp.where(row == 0, 1.0, 0.0).astype(BF16)

    lane = lax.broadcasted_iota(jnp.int32, (1, hw), 1)
    qscale = DA_HEAD ** -0.5 * math.log2(math.e)

    def stages(s_new, mx_new, s_old, mx_old, p_old, p_older):
        accs = []
        for m in range(2):
            accs.append(_dot(vt_scr[...], p_older[m]))
        q = q_ref[0].astype(F32)
        for m in range(2):
            sel = (lane < DA_HEAD) if m == 0 else (lane >= DA_HEAD)
            qm = jnp.where(sel, q * qscale, 0.0).astype(BF16)
            s_t = _dot_nt(k_scr[...], qm)
            s_new[m] = s_t
            part = [None] * 4
            for r in range(n_keys // 8):
                tile = s_t[r * 8:(r + 1) * 8, :]
                j = r % len(part)
                part[j] = tile if part[j] is None else jnp.maximum(part[j], tile)
            mx_new[m] = functools.reduce(jnp.maximum, part)
            mx8 = jnp.broadcast_to(jnp.max(mx_old[m], axis=0, keepdims=True), (8, tq))
            for c in range(n_keys // DA_EXP_ROWS):
                rows = slice(c * DA_EXP_ROWS, (c + 1) * DA_EXP_ROWS)
                zero = jnp.minimum(jnp.abs(s_t[c * DA_EXP_ROWS:c * DA_EXP_ROWS + 8, :]), 0.0)
                d = s_old[m, rows, :].reshape(DA_EXP_ROWS // 8, 8, tq) - (mx8 + zero)[None]
                p_old[m, rows, :] = jnp.exp2(d.reshape(DA_EXP_ROWS, tq).astype(BF16))
        a0, a1 = accs
        o_t = a0[:hw] * (1.0 / a0[hw:hw + 1]) - a1[:hw] * (lam_ref[0, 0] / a1[hw:hw + 1])
        o_t = o_t * lax.rsqrt(jnp.mean(o_t * o_t, axis=0, keepdims=True) + NORM_EPS) * (sg_ref[...] * post_scale)
        o_ref[0] = o_t.T.astype(o_ref.dtype)

    @pl.when(lax.rem(g, 2) == 0)
    def _():
        stages(s_a, mx_a, s_b, mx_b, p_b, p_a)

    @pl.when(lax.rem(g, 2) == 1)
    def _():
        stages(s_b, mx_b, s_a, mx_a, p_a, p_b)


def _diff_attention(q_arr, kv_arrs, lam, subln_g, lam_init, *, tq, name):
    b, lq, _ = q_arr.shape
    hw = 2 * DA_HEAD
    n_seg = len(kv_arrs)
    n_keys = sum(kv.shape[1] for kv in kv_arrs)
    assert lq % tq == 0 and n_keys % DA_EXP_ROWS == 0
    nq = lq // tq
    n_tiles = b * DA_HEADS * nq

    def tile(t):
        return t // (nq * DA_HEADS), lax.rem(t // nq, DA_HEADS), lax.rem(t, nq)

    def qk_tile(g):
        return tile(jnp.minimum(g, n_tiles - 1))

    def pv_tile(g):
        return tile(jnp.clip(g - 2, 0, n_tiles - 1))

    in_specs = [pl.BlockSpec(memory_space=pltpu.SMEM),
                pl.BlockSpec((1, tq, hw), lambda g: (qk_tile(g)[0], qk_tile(g)[2], qk_tile(g)[1]))]
    in_specs += [pl.BlockSpec((1, kv.shape[1], hw), lambda g: (qk_tile(g)[0], 0, DA_HEADS + qk_tile(g)[1]))
                 for kv in kv_arrs]
    in_specs += [pl.BlockSpec((1, kv.shape[1], hw), lambda g: (pv_tile(g)[0], 0, 2 * DA_HEADS + pv_tile(g)[1]))
                 for kv in kv_arrs]
    in_specs += [pl.BlockSpec((hw, 1), lambda g: (0, 0))]
    s_buf = pltpu.VMEM((2, n_keys, tq), F32)
    p_buf = pltpu.VMEM((2, n_keys, tq), BF16)
    mx_buf = pltpu.VMEM((2, 8, tq), F32)
    return pl.pallas_call(
        functools.partial(_da_kernel, n_seg=n_seg, post_scale=1.0 - lam_init, n_tiles=n_tiles, nq=nq),
        grid=(n_tiles + 2,),
        in_specs=in_specs,
        out_specs=pl.BlockSpec((1, tq, hw), lambda g: (pv_tile(g)[0], pv_tile(g)[2], pv_tile(g)[1])),
        out_shape=jax.ShapeDtypeStruct((b, lq, D_MODEL), BF16),
        scratch_shapes=[pltpu.VMEM((n_keys, hw), BF16), pltpu.VMEM((hw + DA_SUM_ROWS, n_keys), BF16),
                        s_buf, s_buf, p_buf, p_buf, mx_buf, mx_buf],
        compiler_params=_cparams(("arbitrary",)),
        name=name,
    )(lam.reshape(1, 1), q_arr, *kv_arrs, *kv_arrs, subln_g.reshape(hw, 1))


def _wa_kernel(*refs, windowed, seq_len):
    if windowed:
        sink_ref, q_ref, kc_ref, vc_ref, kp_ref, kq_ref, kn_ref, vp_ref, vq_ref, vn_ref, o_ref = refs
        k_all = jnp.concatenate([kc_ref[0], kp_ref[0], kq_ref[0], kn_ref[0]], axis=0)
        v_all = jnp.concatenate([vc_ref[0], vp_ref[0], vq_ref[0], vn_ref[0]], axis=0)
    else:
        sink_ref, q_ref, kc_ref, vc_ref, o_ref = refs
        k_all, v_all = kc_ref[0], vc_ref[0]
    tq = q_ref.shape[1]
    n_keys = k_all.shape[0]
    n_ctx = kc_ref.shape[1]
    i = pl.program_id(1)
    if windowed:
        qpos = i * tq + lax.broadcasted_iota(jnp.int32, (tq, n_keys), 0)
        col = lax.broadcasted_iota(jnp.int32, (tq, n_keys), 1)
        kpos = (i - 1) * tq + col - n_ctx
        ok = (col < n_ctx) | ((jnp.abs(qpos - kpos) <= WINDOW) & (kpos >= 0) & (kpos < seq_len))
    lane = lax.broadcasted_iota(jnp.int32, (1, LANES), 1)
    lo = lane < WA_HEAD
    scale = jnp.asarray(WA_HEAD ** -0.5, BF16)
    rep = WA_Q_HEADS // WA_KV_HEADS
    for g in range(WA_KV_HEADS):
        kg = k_all[:, (g // 2) * LANES:(g // 2 + 1) * LANES]
        vg = v_all[:, (g // 2) * LANES:(g // 2 + 1) * LANES]
        zero = jnp.zeros_like(kg)
        if g % 2 == 0:
            kg_same, vg_same = jnp.where(lo, kg, zero), jnp.where(lo, vg, zero)
        else:
            kg_same, vg_same = jnp.where(lo, zero, kg), jnp.where(lo, zero, vg)
        kg_swap = pltpu.roll(kg_same.astype(F32), WA_HEAD, 1).astype(BF16)
        vg_swap = pltpu.roll(vg_same.astype(F32), WA_HEAD, 1).astype(BF16)
        k_half = (kg_same, kg_swap) if g % 2 == 0 else (kg_swap, kg_same)
        v_half = (vg_same, vg_swap) if g % 2 == 0 else (vg_swap, vg_same)
        for pair in range(rep // 2):
            chunk = g * (rep // 2) + pair
            q2 = q_ref[0, :, chunk * LANES:(chunk + 1) * LANES] * scale
            out = None
            for half in range(2):
                head = 2 * chunk + half
                s = _dot_nt(q2, k_half[half])
                if windowed:
                    s = jnp.where(ok, s, -1e30)
                snk = sink_ref[head]
                mx = jnp.maximum(jnp.max(s, axis=-1, keepdims=True), snk)
                p = jnp.exp(s - mx)
                l = jnp.sum(p, axis=-1, keepdims=True) + jnp.exp(snk - mx)
                part = _dot((p * (1.0 / l)).astype(BF16), v_half[half])
                out = part if out is None else out + part
            o_ref[0, :, chunk * LANES:(chunk + 1) * LANES] = out.astype(o_ref.dtype)


def _window_attention(q_arr, ctx_arr, sink, *, windowed, name):
    b, lq, _ = q_arr.shape
    n_ctx = ctx_arr.shape[1]
    qw = WA_Q_HEADS * WA_HEAD
    kw = WA_KV_HEADS * WA_HEAD
    k_col, v_col = qw // kw, qw // kw + 1
    tq = 128 if windowed else lq
    nq = lq // tq
    in_specs = [pl.BlockSpec(memory_space=pltpu.SMEM),
                pl.BlockSpec((1, tq, qw), lambda bi, i: (bi, i, 0)),
                pl.BlockSpec((1, n_ctx, kw), lambda bi, i: (bi, 0, k_col)),
                pl.BlockSpec((1, n_ctx, kw), lambda bi, i: (bi, 0, v_col))]
    args = [sink, q_arr, ctx_arr, ctx_arr]
    if windowed:
        assert tq == WINDOW
        for col in (k_col, v_col):
            in_specs += [pl.BlockSpec((1, tq, kw), lambda bi, i, col=col: (bi, jnp.maximum(i - 1, 0), col)),
                         pl.BlockSpec((1, tq, kw), lambda bi, i, col=col: (bi, i, col)),
                         pl.BlockSpec((1, tq, kw), lambda bi, i, col=col: (bi, jnp.minimum(i + 1, nq - 1), col))]
            args += [q_arr, q_arr, q_arr]
    return pl.pallas_call(
        functools.partial(_wa_kernel, windowed=windowed, seq_len=lq),
        grid=(b, nq),
        in_specs=in_specs,
        out_specs=pl.BlockSpec((1, tq, qw), lambda bi, i: (bi, i, 0)),
        out_shape=jax.ShapeDtypeStruct((b, lq, qw), BF16),
        compiler_params=_cparams(("parallel", "parallel")),
        name=name,
    )(*args)


def _ret_kernel(cd_ref, qcf, kcf, vcf, qlf, klf, vlf, qcb, kcb, vcb, qlb, klb, vlb, dtab_ref, qd_ref, kd_ref,
                ocf, olf, ocb, olb, s_scr, *, n_ctx_chunks):
    j = pl.program_id(1)
    is_ctx = j < n_ctx_chunks

    @pl.when(j == 0)
    def _():
        s_scr[...] = jnp.zeros_like(s_scr)

    kscale = jnp.asarray(RET_KEY ** -0.5, BF16)
    dirs = ((qcf, kcf, vcf, qlf, klf, vlf, ocf, olf), (qcb, kcb, vcb, qlb, klb, vlb, ocb, olb))
    for d, (qc, kc, vc, ql, kl, vl, oc, ol) in enumerate(dirs):
        q = jnp.where(is_ctx, qc[0], ql[0])
        k = jnp.where(is_ctx, kc[0], kl[0]) * kscale
        v = jnp.where(is_ctx, vc[0], vl[0])
        outs = []
        for h in range(RET_HEADS):
            qh = q[:, h * RET_KEY:(h + 1) * RET_KEY]
            kh = k[:, h * RET_KEY:(h + 1) * RET_KEY]
            vh = v[:, h * RET_VAL:(h + 1) * RET_VAL]
            state = s_scr[d, h]
            inner = _dot_nt(qh, kh) * dtab_ref[d, h]
            o = _dot(inner.astype(BF16), vh) + _dot(qh, state.astype(BF16)) * qd_ref[d, h]
            kdec = (kh.astype(F32) * kd_ref[d, h]).astype(BF16)
            s_scr[d, h] = state * cd_ref[d, h] + _dot_tn(kdec, vh)
            outs.append(o)
        o_all = jnp.concatenate(outs, axis=1).astype(oc.dtype)

        @pl.when(is_ctx)
        def _():
            oc[0] = o_all

        @pl.when(jnp.logical_not(is_ctx))
        def _():
            ol[0] = o_all


def _retention(ctx_arr, lat_arr, log_g):
    b, lc, _ = ctx_arr.shape
    l = lat_arr.shape[1]
    c = RET_CHUNK
    nc, nl = lc // c, l // c
    qw = RET_HEADS * RET_KEY
    vw = RET_HEADS * RET_VAL
    idx = jnp.arange(c, dtype=F32)
    rel = idx[:, None] - idx[None, :]
    lg = log_g[:, :, None, None]
    keep_f = (rel >= 0)[None]
    keep_b = (rel < 0)[None]
    d_f = jnp.where(keep_f, jnp.exp(jnp.where(keep_f, rel[None] * lg[0], 0.0)), 0.0)
    d_b = jnp.where(keep_b, jnp.exp(jnp.where(keep_b, -rel[None] * lg[1], 0.0)), 0.0)
    dtab = jnp.stack([d_f, d_b])
    col = idx[None, :, None]
    qd = jnp.stack([jnp.exp((col + 1.0) * lg[0]), jnp.exp((c - col) * lg[1])])
    kd = jnp.stack([jnp.exp((c - 1.0 - col) * lg[0]), jnp.exp(col * lg[1])])
    cd = jnp.exp(c * log_g)

    f_ctx = lambda bi, j: (bi, jnp.minimum(j, nc - 1))
    f_lat = lambda bi, j: (bi, jnp.maximum(j - nc, 0))
    b_ctx = lambda bi, j: (bi, jnp.maximum(nc - 1 - j, 0))
    b_lat = lambda bi, j: (bi, jnp.minimum(nc + nl - 1 - j, nl - 1))

    def qkv_specs(row_map):
        return [pl.BlockSpec((1, c, qw), lambda bi, j: row_map(bi, j) + (0,)),
                pl.BlockSpec((1, c, qw), lambda bi, j: row_map(bi, j) + (1,)),
                pl.BlockSpec((1, c, vw), lambda bi, j: row_map(bi, j) + (1,))]

    full = lambda shape: pl.BlockSpec(shape, lambda bi, j: (0,) * len(shape))
    in_specs = ([pl.BlockSpec(memory_space=pltpu.SMEM)]
                + qkv_specs(f_ctx) + qkv_specs(f_lat) + qkv_specs(b_ctx) + qkv_specs(b_lat)
                + [full(dtab.shape), full(qd.shape), full(kd.shape)])
    out_spec = lambda row_map: pl.BlockSpec((1, c, vw), lambda bi, j: row_map(bi, j) + (0,))
    out_sds = lambda n: jax.ShapeDtypeStruct((b, n, vw), BF16)
    return pl.pallas_call(
        functools.partial(_ret_kernel, n_ctx_chunks=nc),
        grid=(b, nc + nl),
        in_specs=in_specs,
        out_specs=[out_spec(f_ctx), out_spec(f_lat), out_spec(b_ctx), out_spec(b_lat)],
        out_shape=[out_sds(lc), out_sds(l), out_sds(lc), out_sds(l)],
        scratch_shapes=[pltpu.VMEM((2, RET_HEADS, RET_KEY, RET_VAL), F32)],
        compiler_params=_cparams(("parallel", "arbitrary")),
        name="retention_scan",
    )(cd, ctx_arr, ctx_arr, ctx_arr, lat_arr, lat_arr, lat_arr, ctx_arr, ctx_arr, ctx_arr, lat_arr, lat_arr,
      lat_arr, dtab, qd, kd)


def _router_kernel(x_ref, g_ref, sh_ref, sc_ref, wr_ref, a_ref, lg_ref):
    a = _modulated(x_ref, g_ref, sh_ref, sc_ref)
    a_hi = a.astype(BF16)
    a_ref[...] = a_hi
    a_lo = (a - a_hi.astype(F32)).astype(BF16)
    w = wr_ref[...]
    w_hi = w.astype(BF16)
    w_lo = (w - w_hi.astype(F32)).astype(BF16)
    lg_ref[...] = _dot(a_hi, w_hi) + _dot(a_lo, w_hi) + _dot(a_hi, w_lo)


def _router(x, g, mod, sh_idx, rows_per_mod, w_router_pad, *, tm, name):
    t, d = x.shape
    if rows_per_mod is None:
        mod_row = lambda i: 0
    else:
        tiles_per_mod = rows_per_mod // tm
        mod_row = lambda i: i // tiles_per_mod
    return pl.pallas_call(
        _router_kernel,
        grid=(t // tm,),
        in_specs=[pl.BlockSpec((tm, d), lambda i: (i, 0)),
                  pl.BlockSpec((1, d), lambda i: (0, 0)),
                  pl.BlockSpec((1, 1, d), lambda i: (mod_row(i), 0, sh_idx)),
                  pl.BlockSpec((1, 1, d), lambda i: (mod_row(i), 0, sh_idx + 1)),
                  pl.BlockSpec((d, ROUTER_PAD), lambda i: (0, 0))],
        out_specs=[pl.BlockSpec((tm, d), lambda i: (i, 0)), pl.BlockSpec((tm, ROUTER_PAD), lambda i: (i, 0))],
        out_shape=[jax.ShapeDtypeStruct((t, d), BF16), jax.ShapeDtypeStruct((t, ROUTER_PAD), F32)],
        compiler_params=_cparams(("parallel",)),
        name=name,
    )(x, g.reshape(1, d), mod, mod, w_router_pad)


def _experts_kernel(be_ref, nb_ref, x_ref, wg_ref, wu_ref, wd_ref, o_ref, acc, *, nf):
    blk = pl.program_id(0)
    f = pl.program_id(1)
    used = blk < nb_ref[0]
    rc = min(x_ref.shape[0], MOE_ROW_CHUNK)

    @pl.when((blk == 0) & (f == 0))
    def _():
        acc[...] = jnp.zeros_like(acc)

    @pl.when(used)
    def _():
        wg = wg_ref[0].astype(BF16)
        wu = wu_ref[0].astype(BF16)
        wd = wd_ref[0].astype(BF16)
        n_chunks = x_ref.shape[0] // rc

        def gate_up(c):
            x = x_ref[c * rc:(c + 1) * rc, :]
            return _dot(x, wg), _dot(x, wu)

        nxt = gate_up(0)
        for c in range(n_chunks):
            g, u = nxt
            if c + 1 < n_chunks:
                nxt = gate_up(c + 1)
            part = _dot((g * _sigmoid(g) * u).astype(BF16), wd)
            rows = slice(c * rc, (c + 1) * rc)
            acc[rows, :] = part + jnp.where(f > 0, acc[rows, :], 0.0)

    @pl.when(f == nf - 1)
    def _():
        o_ref[...] = jnp.where(used, acc[...], 0.0).astype(o_ref.dtype)


def _experts(buf, block_expert, n_used, w_gu, w_down):
    rows, d = buf.shape
    nb = rows // MOE_ROWS
    e, _, two_f = w_gu.shape
    fdim = two_f // 2
    fc = _pick(fdim, (MOE_FC, 256, 128))
    nf = fdim // fc
    grid_spec = pltpu.PrefetchScalarGridSpec(
        num_scalar_prefetch=2,
        grid=(nb, nf),
        in_specs=[pl.BlockSpec((MOE_ROWS, d), lambda i, f, be, nu: (i, 0)),
                  pl.BlockSpec((1, d, fc), lambda i, f, be, nu: (be[i], 0, f)),
                  pl.BlockSpec((1, d, fc), lambda i, f, be, nu: (be[i], 0, nf + f)),
                  pl.BlockSpec((1, fc, d), lambda i, f, be, nu: (be[i], f, 0))],
        out_specs=pl.BlockSpec((MOE_ROWS, d), lambda i, f, be, nu: (i, 0)),
        scratch_shapes=[pltpu.VMEM((MOE_ROWS, d), F32)],
    )
    return pl.pallas_call(
        functools.partial(_experts_kernel, nf=nf),
        grid_spec=grid_spec,
        out_shape=jax.ShapeDtypeStruct((rows, d), BF16),
        compiler_params=_cparams(("parallel", "arbitrary")),
        name="moe_experts",
    )(block_expert, n_used, buf, w_gu, w_gu, w_down)


def _moe(a_all, logits, w_gu, w_down):
    t, d = a_all.shape
    e = w_gu.shape[0]
    top_val, top_idx = lax.top_k(logits, TOP_K)
    gates = jax.nn.softmax(top_val, axis=-1)
    onehot = jnp.sum((top_idx[..., None] == jnp.arange(e)[None, None, :]).astype(jnp.int32), axis=1)
    counts = jnp.sum(onehot, axis=0)
    rank = jnp.cumsum(onehot, axis=0) - onehot
    padded = (counts + MOE_ROWS - 1) // MOE_ROWS * MOE_ROWS
    pend = jnp.cumsum(padded)
    pstart = pend - padded
    dest = pstart[top_idx] + jnp.take_along_axis(rank, top_idx, axis=1)
    nb = -(-(t * TOP_K) // MOE_ROWS) + e
    tok = jnp.repeat(jnp.arange(t, dtype=jnp.int32), TOP_K)
    row_src = jnp.zeros((nb * MOE_ROWS,), jnp.int32).at[dest.reshape(-1)].set(tok)
    buf = a_all[row_src]
    block_expert = jnp.minimum(jnp.searchsorted(pend, jnp.arange(nb) * MOE_ROWS, side="right"), e - 1).astype(jnp.int32)
    n_used = (pend[-1] // MOE_ROWS).astype(jnp.int32).reshape(1)
    yb = _experts(buf, block_expert, n_used, w_gu, w_down)
    return (yb[dest[:, 0]].astype(F32) * gates[:, 0:1] + yb[dest[:, 1]].astype(F32) * gates[:, 1:2])


def _rope_angles(pos, dim):
    inv = ROPE_BASE ** (-jnp.arange(0, dim, 2, dtype=F32) / dim)
    ang = pos.astype(F32)[:, None] * inv[None, :]
    return jnp.concatenate([ang, ang], axis=-1)


def _signed_sin(sin, shift):
    low = (jnp.arange(sin.shape[1]) % (2 * shift)) < shift
    return jnp.where(low[None, :], -sin, 0.0), jnp.where(low[None, :], 0.0, sin)


def _axial_tables(n_tokens, head_dim):
    rows = n_tokens // GRID_W
    row = jnp.repeat(jnp.arange(rows, dtype=jnp.int32), GRID_W)
    col = jnp.tile(jnp.arange(GRID_W, dtype=jnp.int32), rows)
    half = head_dim // 2
    ang = jnp.concatenate([_rope_angles(row, half), _rope_angles(col, half)], axis=-1)
    ang = jnp.tile(ang, (1, LANES // head_dim))
    sa, sb = _signed_sin(jnp.sin(ang), half // 2)
    return jnp.cos(ang), sa, sb, half // 2


def _ret_tables(n_tokens):
    ang = _rope_angles(jnp.arange(n_tokens), RET_KEY)
    sa, sb = _signed_sin(jnp.sin(ang), RET_KEY // 2)
    return jnp.cos(ang), sa, sb, RET_KEY // 2


def kernel(x, c, ctx, c_ctx, mod_w, mod_b, norm_g, da_w_in, da_w_out, da_lambda, da_subln_g, wa_w_in, wa_w_out,
           wa_sink, ret_w_in, ret_w_out, ret_decay_logit, ffn_w_gu, ffn_w_down, moe_router, moe_w_gu, moe_w_down):
    b, l, d = x.shape
    lc = ctx.shape[1]
    depth = mod_w.shape[0]
    t, tc = b * l, b * lc
    tm = _pick(l, (1024, 512, 256, 128))
    tmc = _pick(tc, (1024, 512, 256, 128))

    n_cond = -(-(b + 1) // 8) * 8
    cs = jnp.zeros((n_cond, d), F32).at[:b].set(c).at[b].set(c_ctx)
    mod_all = _mod_vectors(cs, mod_w, mod_b)

    xs = x.reshape(t, d)
    hs = ctx.reshape(tc, d)
    mixer_count = [0] * N_MIXERS
    i_dense = 0
    i_moe = 0
    for layer in range(depth):
        last = layer == depth - 1
        mod = mod_all[layer, :b].reshape(b, 1, 6 * d)
        mod_c = mod_all[layer, b:b + 1].reshape(1, 1, 6 * d)
        ng = norm_g[layer]
        kind = layer % N_MIXERS
        jm = mixer_count[kind]
        mixer_count[kind] += 1

        if kind == 0:
            lam_init = 0.8 - 0.6 * math.exp(-0.3 * layer)
            w_in = da_w_in[jm].astype(BF16)
            cos, sa, sb, shift = _axial_tables(l, DA_HEAD)
            qkv = _norm_proj(xs, ng[0], mod, 0, l, w_in, tm=tm, tn=3 * d, rot=(cos, sa, sb, 2 * d, shift, l),
                             name="da_in_proj").reshape(b, l, 3 * d)
            qkv_c = _norm_proj(hs, ng[0], mod_c, 0, None, w_in, tm=tmc, tn=3 * d, name="da_in_proj_ctx").reshape(b, lc, 3 * d)
            lp = da_lambda[jm].astype(F32)
            lam = jnp.exp(jnp.sum(lp[0] * lp[1])) - jnp.exp(jnp.sum(lp[2] * lp[3])) + lam_init
            o = _diff_attention(qkv, [qkv_c, qkv], lam, da_subln_g[jm], lam_init, tq=_pick(l, (DA_TQ, 128)), name="diff_attn")
            w_out = da_w_out[jm].astype(BF16)
            xs = _resid(xs, ng[1], mod, 2, l, tm=tm, a=o.reshape(t, d), w=w_out, name="da_out_proj")
            if not last:
                oc = _diff_attention(qkv_c, [qkv_c], lam, da_subln_g[jm], lam_init, tq=_pick(lc, (DA_TQ, 128)), name="diff_attn_ctx")
                hs = _resid(hs, ng[1], mod_c, 2, None, tm=tmc, a=oc.reshape(tc, d), w=w_out, name="da_out_proj_ctx")
        elif kind == 1:
            w_in = wa_w_in[jm].astype(BF16)
            n_qkv = w_in.shape[1]
            cos, sa, sb, shift = _axial_tables(l, WA_HEAD)
            n_rot = (WA_Q_HEADS + WA_KV_HEADS) * WA_HEAD
            qkv = _norm_proj(xs, ng[0], mod, 0, l, w_in, tm=tm, tn=n_qkv, rot=(cos, sa, sb, n_rot, shift, l),
                             name="wa_in_proj").reshape(b, l, n_qkv)
            qkv_c = _norm_proj(hs, ng[0], mod_c, 0, None, w_in, tm=tmc, tn=n_qkv, name="wa_in_proj_ctx").reshape(b, lc, n_qkv)
            o = _window_attention(qkv, qkv_c, wa_sink[jm], windowed=True, name="window_attn")
            w_out = wa_w_out[jm].astype(BF16)
            xs = _resid(xs, ng[1], mod, 2, l, tm=tm, a=o.reshape(t, d), w=w_out, name="wa_out_proj")
            if not last:
                oc = _window_attention(qkv_c, qkv_c, wa_sink[jm], windowed=False, name="window_attn_ctx")
                hs = _resid(hs, ng[1], mod_c, 2, None, tm=tmc, a=oc.reshape(tc, d), w=w_out, name="wa_out_proj_ctx")
        else:
            w_in = ret_w_in[jm].astype(BF16)
            n_qkv = w_in.shape[1]
            cos, sa, sb, shift = _ret_tables(l)
            n_rot = 2 * RET_HEADS * RET_KEY
            tr = _pick(l, (512, 256, 128))
            qkv = _norm_proj(xs, ng[0], mod, 0, l, w_in, tm=tr, tn=n_qkv // 2, rot=(cos, sa, sb, n_rot, shift, l),
                             name="ret_in_proj")
            qkv_c = _norm_proj(hs, ng[0], mod_c, 0, None, w_in, tm=_pick(tc, (512, 256, 128)), tn=n_qkv // 2,
                               name="ret_in_proj_ctx")
            log_g = jax.nn.log_sigmoid(ret_decay_logit[jm].astype(F32))
            ocf, olf, ocb, olb = _retention(qkv_c.reshape(b, lc, n_qkv), qkv.reshape(b, l, n_qkv), log_g)
            w_out = ret_w_out[jm].astype(BF16)
            vw = RET_HEADS * RET_VAL
            xs = _resid(xs, ng[1], mod, 2, l, tm=_pick(l, (512, 256, 128)), w=w_out,
                        ret=(olf.reshape(t, vw), olb.reshape(t, vw), qkv), name="ret_out_proj")
            if not last:
                hs = _resid(hs, ng[1], mod_c, 2, None, tm=_pick(tc, (512, 256, 128)), w=w_out,
                            ret=(ocf.reshape(tc, vw), ocb.reshape(tc, vw), qkv_c), name="ret_out_proj_ctx")

        if layer % 2 == 0:
            w_gu = ffn_w_gu[i_dense].astype(BF16)
            w_dn = ffn_w_down[i_dense].astype(BF16)
            i_dense += 1
            f = w_dn.shape[0]
            w_g, w_u = w_gu[:, :f], w_gu[:, f:]
            tn = f
            tf = _pick(l, (512, 256, 128))
            act = _norm_proj(xs, ng[2], mod, 3, l, (w_g, w_u), tm=tf, tn=tn, name="ffn_up")
            xs = _resid(xs, ng[3], mod, 5, l, tm=tf, a=act, w=w_dn, name="ffn_down")
            if not last:
                tfc = _pick(tc, (512, 256, 128))
                act_c = _norm_proj(hs, ng[2], mod_c, 3, None, (w_g, w_u), tm=tfc, tn=tn, name="ffn_up_ctx")
                hs = _resid(hs, ng[3], mod_c, 5, None, tm=tfc, a=act_c, w=w_dn, name="ffn_down_ctx")
        else:
            w_r = jnp.zeros((d, ROUTER_PAD), F32).at[:, :N_EXPERTS].set(moe_router[i_moe])
            w_gu = moe_w_gu[i_moe]
            w_dn = moe_w_down[i_moe]
            i_moe += 1
            a_x, lg_x = _router(xs, ng[2], mod, 3, l, w_r, tm=tm, name="moe_router")
            if not last:
                a_c, lg_c = _router(hs, ng[2], mod_c, 3, None, w_r, tm=tmc, name="moe_router_ctx")
                a_all = jnp.concatenate([a_x, a_c], axis=0)
                lg_all = jnp.concatenate([lg_x, lg_c], axis=0)
            else:
                a_all, lg_all = a_x, lg_x
            y_all = _moe(a_all, lg_all[:, :N_EXPERTS], w_gu, w_dn)
            xs = _resid(xs, ng[3], mod, 5, l, tm=tm, y=y_all[:t], name="moe_combine")
            if not last:
                hs = _resid(hs, ng[3], mod_c, 5, None, tm=tmc, y=y_all[t:], name="moe_combine_ctx")
    return xs.reshape(b, l, d)
```

```python
import functools
import math

import jax
import jax.numpy as jnp
from jax import lax
from jax.experimental import pallas as pl
from jax.experimental.pallas import tpu as pltpu

F32 = jnp.float32
BF16 = jnp.bfloat16

D_MODEL = 1024
N_MIXERS = 3
NORM_EPS = 1e-6
ROPE_BASE = 10000.0
GRID_W = 64
DA_HEAD = 64
DA_HEADS = D_MODEL // (2 * DA_HEAD)
WA_HEAD = 64
WA_Q_HEADS = D_MODEL // WA_HEAD
WA_KV_HEADS = WA_Q_HEADS // 4
WINDOW = 128
RET_KEY = 256
RET_HEADS = D_MODEL // RET_KEY
RET_VAL = 2 * RET_KEY
RET_CHUNK = 128
N_EXPERTS = 8
TOP_K = 2

LANES = 128
V7X_VMEM_LIMIT = 56 * 1024 * 1024
ROUTER_PAD = LANES
MOE_ROWS = 1024
MOE_FC = 512
MOE_ROW_CHUNK = 256
PROJ_COLS = 512


def _cparams(sem):
    return pltpu.CompilerParams(dimension_semantics=sem, vmem_limit_bytes=V7X_VMEM_LIMIT)


def _pick(n, prefs):
    for p in prefs:
        if n % p == 0:
            return p
    return n


def _sigmoid(x):
    return 1.0 / (1.0 + jnp.exp(-x))


def _rms(y):
    return y * lax.rsqrt(jnp.mean(y * y, axis=-1, keepdims=True) + NORM_EPS)


def _dot(a, b):
    return jnp.dot(a, b, preferred_element_type=F32)


def _dot_nt(a, b):
    return lax.dot_general(a, b, (((1,), (1,)), ((), ())), preferred_element_type=F32)


def _dot_tn(a, b):
    return lax.dot_general(a, b, (((0,), (0,)), ((), ())), preferred_element_type=F32)


def _mod_kernel(c_ref, w_ref, b_ref, o_ref):
    cs = c_ref[...]
    s = cs * _sigmoid(cs)
    w = w_ref[0]
    s_hi = s.astype(BF16)
    s_lo = (s - s_hi.astype(F32)).astype(BF16)
    w_hi = w.astype(BF16)
    w_lo = (w - w_hi.astype(F32)).astype(BF16)
    o_ref[0] = _dot(s_hi, w_hi) + _dot(s_lo, w_hi) + _dot(s_hi, w_lo) + b_ref[0]


def _mod_vectors(cs, mod_w, mod_b):
    depth, d, n = mod_w.shape
    r = cs.shape[0]
    tn = _pick(n, (1024, 512, 256, 128))
    return pl.pallas_call(
        _mod_kernel,
        grid=(depth, n // tn),
        in_specs=[
            pl.BlockSpec((r, d), lambda l, j: (0, 0)),
            pl.BlockSpec((1, d, tn), lambda l, j: (l, 0, j)),
            pl.BlockSpec((1, 1, tn), lambda l, j: (l, 0, j)),
        ],
        out_specs=pl.BlockSpec((1, r, tn), lambda l, j: (l, 0, j)),
        out_shape=jax.ShapeDtypeStruct((depth, r, n), F32),
        compiler_params=_cparams(("parallel", "parallel")),
        name="mod_vectors",
    )(cs, mod_w, mod_b.reshape(depth, 1, n))


def _modulated(x_ref, g_ref, sh_ref, sc_ref):
    x = x_ref[...]
    return _rms(x) * g_ref[...] * (1.0 + sc_ref[0]) + sh_ref[0]


def _rotate(acc, cos, sa, sb, shift, cw):
    outs = []
    for c in range(acc.shape[1] // cw):
        y = acc[:, c * cw:(c + 1) * cw]
        up = pltpu.roll(y, cw - shift, 1)
        dn = pltpu.roll(y, shift, 1)
        outs.append(y * cos + up * sa + dn * sb)
    return outs[0] if len(outs) == 1 else jnp.concatenate(outs, axis=1)


def _col_chunks(width, cuts):
    edges = sorted({0, width} | set(range(PROJ_COLS, width, PROJ_COLS)) | {c for c in cuts if 0 < c < width})
    return list(zip(edges[:-1], edges[1:]))


def _proj_kernel(*refs, mode, n_rot_cols, shift, cw, n_col_tiles):
    if mode == "swiglu":
        x_ref, g_ref, sh_ref, sc_ref, wg_ref, wu_ref, o_ref, a_scr = refs
    elif mode == "rot":
        x_ref, g_ref, sh_ref, sc_ref, w_ref, cos_ref, sa_ref, sb_ref, o_ref, a_scr = refs
    else:
        x_ref, g_ref, sh_ref, sc_ref, w_ref, o_ref, a_scr = refs
    j = pl.program_id(1)
    tn = o_ref.shape[1]

    @pl.when(j == 0)
    def _():
        a_scr[...] = _modulated(x_ref, g_ref, sh_ref, sc_ref).astype(BF16)

    def tile(jt):
        a = a_scr[...]
        for lo, hi in _col_chunks(tn, [n_rot_cols - jt * tn]):
            if mode == "swiglu":
                g = _dot(a, wg_ref[:, lo:hi])
                res = g * _sigmoid(g) * _dot(a, wu_ref[:, lo:hi])
            else:
                res = _dot(a, w_ref[:, lo:hi])
                if mode == "rot" and jt * tn + hi <= n_rot_cols:
                    res = _rotate(res, cos_ref[...], sa_ref[...], sb_ref[...], shift, cw)
            o_ref[:, lo:hi] = res.astype(o_ref.dtype)

    if mode != "rot":
        tile(0)
    else:
        n_rot_tiles = -(-n_rot_cols // tn)
        for jt in range(min(n_rot_tiles + 1, n_col_tiles)):
            @pl.when((j >= jt) if jt == n_rot_tiles else (j == jt))
            def _(jt=jt):
                tile(jt)


def _norm_proj(x, g, mod, sh_idx, rows_per_mod, weights, *, tm, tn, rot=None, name):
    t, d = x.shape
    swiglu = isinstance(weights, tuple)
    n = weights[0].shape[1] if swiglu else weights.shape[1]
    assert t % tm == 0 and n % tn == 0
    if rows_per_mod is None:
        mod_row = lambda i: 0
    else:
        assert rows_per_mod % tm == 0
        tiles_per_mod = rows_per_mod // tm
        mod_row = lambda i: i // tiles_per_mod
    in_specs = [
        pl.BlockSpec((tm, d), lambda i, j: (i, 0)),
        pl.BlockSpec((1, d), lambda i, j: (0, 0)),
        pl.BlockSpec((1, 1, d), lambda i, j: (mod_row(i), 0, sh_idx)),
        pl.BlockSpec((1, 1, d), lambda i, j: (mod_row(i), 0, sh_idx + 1)),
    ]
    args = [x, g.reshape(1, d), mod, mod]
    w_spec = pl.BlockSpec((d, tn), lambda i, j: (0, j))
    n_rot_cols, shift, cw = 0, 0, 0
    if swiglu:
        mode = "swiglu"
        in_specs += [w_spec, w_spec]
        args += list(weights)
    elif rot is not None:
        mode = "rot"
        cos, sa, sb, n_rot_cols, shift, seq_len = rot
        cw = cos.shape[1]
        assert n_rot_cols % cw == 0 and PROJ_COLS % cw == 0 and seq_len % tm == 0
        tiles_per_seq = seq_len // tm
        t_spec = pl.BlockSpec((tm, cw), lambda i, j: (i % tiles_per_seq, 0))
        in_specs += [w_spec, t_spec, t_spec, t_spec]
        args += [weights, cos, sa, sb]
    else:
        mode = "plain"
        in_specs += [w_spec]
        args += [weights]
    return pl.pallas_call(
        functools.partial(_proj_kernel, mode=mode, n_rot_cols=n_rot_cols, shift=shift, cw=cw, n_col_tiles=n // tn),
        grid=(t // tm, n // tn),
        in_specs=in_specs,
        out_specs=pl.BlockSpec((tm, tn), lambda i, j: (i, j)),
        out_shape=jax.ShapeDtypeStruct((t, n), BF16),
        scratch_shapes=[pltpu.VMEM((tm, d), BF16)],
        compiler_params=_cparams(("parallel", "arbitrary")),
        name=name,
    )(*args)


def _resid_kernel(*refs, mode):
    if mode == "ret":
        of_ref, ob_ref, gt_ref, w_ref, x_ref, gate_ref, g_ref, o_ref = refs
        o = of_ref[...].astype(F32) + ob_ref[...].astype(F32)
        parts = []
        for h in range(RET_HEADS):
            parts.append(_rms(o[:, h * RET_VAL:(h + 1) * RET_VAL]))
        o = jnp.concatenate(parts, axis=1)
        gt = gt_ref[...].astype(F32)
        y = _dot((gt * _sigmoid(gt) * o).astype(BF16), w_ref[...])
    elif mode == "matmul":
        a_ref, w_ref, x_ref, gate_ref, g_ref, o_ref = refs
        y = _dot(a_ref[...], w_ref[...])
    else:
        y0_ref, y1_ref, route_ref, x_ref, gate_ref, g_ref, o_ref = refs
        route = route_ref[...]
        y = y0_ref[...].astype(F32) * route[:, 0:1] + y1_ref[...].astype(F32) * route[:, 1:2]
    o_ref[...] = x_ref[...] + gate_ref[0] * (_rms(y) * g_ref[...])


def _resid(x, g, mod, gate_idx, rows_per_mod, *, tm, name, a=None, w=None, mix=None, ret=None):
    t, d = x.shape
    assert t % tm == 0
    if rows_per_mod is None:
        mod_row = lambda i: 0
    else:
        assert rows_per_mod % tm == 0
        tiles_per_mod = rows_per_mod // tm
        mod_row = lambda i: i // tiles_per_mod
    row = lambda width: pl.BlockSpec((tm, width), lambda i: (i, 0))
    tail_specs = [row(d), pl.BlockSpec((1, 1, d), lambda i: (mod_row(i), 0, gate_idx)),
                  pl.BlockSpec((1, d), lambda i: (0, 0))]
    tail_args = [x, mod, g.reshape(1, d)]
    if ret is not None:
        mode = "ret"
        o_f, o_b, qkvg = ret
        vw = RET_HEADS * RET_VAL
        gate_col = qkvg.shape[1] // vw - 1
        in_specs = [row(vw), row(vw), pl.BlockSpec((tm, vw), lambda i: (i, gate_col)),
                    pl.BlockSpec(w.shape, lambda i: (0, 0))]
        args = [o_f, o_b, qkvg, w]
    elif a is not None:
        mode = "matmul"
        in_specs = [row(a.shape[1]), pl.BlockSpec(w.shape, lambda i: (0, 0))]
        args = [a, w]
    else:
        mode = "mix"
        in_specs = [row(d), row(d), row(ROUTER_PAD)]
        args = list(mix)
    return pl.pallas_call(
        functools.partial(_resid_kernel, mode=mode),
        grid=(t // tm,),
        in_specs=in_specs + tail_specs,
        out_specs=row(d),
        out_shape=jax.ShapeDtypeStruct((t, d), F32),
        compiler_params=_cparams(("parallel",)),
        name=name,
    )(*args, *tail_args)


DA_SUM_ROWS = 16
DA_TQ = 256
DA_EXP_ROWS = 64


def _da_kernel(*refs, n_seg, post_scale, n_tiles, nq):
    lam_ref, q_ref = refs[0], refs[1]
    k_refs = refs[2:2 + n_seg]
    v_refs = refs[2 + n_seg:2 + 2 * n_seg]
    sg_ref, o_ref, k_scr, vt_scr, s_a, s_b, p_a, p_b, mx_a, mx_b = refs[2 + 2 * n_seg:]
    hw = 2 * DA_HEAD
    n_keys = vt_scr.shape[1]
    tq = q_ref.shape[1]
    g = pl.program_id(0)
    t_qk = jnp.minimum(g, n_tiles - 1)
    t_pv = jnp.clip(g - 2, 0, n_tiles - 1)

    @pl.when(g == 0)
    def _():
        for buf in (s_a, s_b, p_a, p_b, mx_a, mx_b):
            buf[...] = jnp.zeros_like(buf)

    @pl.when(lax.rem(t_qk, nq) == 0)
    def _():
        off = 0
        for kr in k_refs:
            n = kr.shape[1]
            k_scr[off:off + n, :] = kr[0]
            off += n

    @pl.when(lax.rem(t_pv, nq) == 0)
    def _():
        off = 0
        for vr in v_refs:
            n = vr.shape[1]
            vt_scr[0:hw, off:off + n] = vr[0].astype(F32).T.astype(BF16)
            off += n
        row = lax.broadcasted_iota(jnp.int32, (DA_SUM_ROWS, n_keys), 0)
        vt_scr[hw:hw + DA_SUM_ROWS, :] = jnp.where(row == 0, 1.0, 0.0).astype(BF16)

    lane = lax.broadcasted_iota(jnp.int32, (1, hw), 1)
    qscale = DA_HEAD ** -0.5 * math.log2(math.e)

    def stages(s_new, mx_new, s_old, mx_old, p_old, p_older):
        accs = []
        for m in range(2):
            accs.append(_dot(vt_scr[...], p_older[m]))
        q = q_ref[0].astype(F32)
        for m in range(2):
            sel = (lane < DA_HEAD) if m == 0 else (lane >= DA_HEAD)
            qm = jnp.where(sel, q * qscale, 0.0).astype(BF16)
            s_t = _dot_nt(k_scr[...], qm)
            s_new[m] = s_t
            part = [None] * 4
            for r in range(n_keys // 8):
                tile = s_t[r * 8:(r + 1) * 8, :]
                j = r % len(part)
                part[j] = tile if part[j] is None else jnp.maximum(part[j], tile)
            mx_new[m] = functools.reduce(jnp.maximum, part)
            mx8 = jnp.broadcast_to(jnp.max(mx_old[m], axis=0, keepdims=True), (8, tq))
            for c in range(n_keys // DA_EXP_ROWS):
                rows = slice(c * DA_EXP_ROWS, (c + 1) * DA_EXP_ROWS)
                zero = jnp.minimum(jnp.abs(s_t[c * DA_EXP_ROWS:c * DA_EXP_ROWS + 8, :]), 0.0)
                d = s_old[m, rows, :].reshape(DA_EXP_ROWS // 8, 8, tq) - (mx8 + zero)[None]
                p_old[m, rows, :] = jnp.exp2(d.reshape(DA_EXP_ROWS, tq).astype(BF16))
        a0, a1 = accs
        o_t = a0[:hw] * (1.0 / a0[hw:hw + 1]) - a1[:hw] * (lam_ref[0, 0] / a1[hw:hw + 1])
        o_t = o_t * lax.rsqrt(jnp.mean(o_t * o_t, axis=0, keepdims=True) + NORM_EPS) * (sg_ref[...] * post_scale)
        o_ref[0] = o_t.T.astype(o_ref.dtype)

    @pl.when(lax.rem(g, 2) == 0)
    def _():
        stages(s_a, mx_a, s_b, mx_b, p_b, p_a)

    @pl.when(lax.rem(g, 2) == 1)
    def _():
        stages(s_b, mx_b, s_a, mx_a, p_a, p_b)


def _diff_attention(q_arr, kv_arrs, lam, subln_g, lam_init, *, tq, name):
    b, lq, _ = q_arr.shape
    hw = 2 * DA_HEAD
    n_seg = len(kv_arrs)
    n_keys = sum(kv.shape[1] for kv in kv_arrs)
    assert lq % tq == 0 and n_keys % DA_EXP_ROWS == 0
    nq = lq // tq
    n_tiles = b * DA_HEADS * nq

    def tile(t):
        return t // (nq * DA_HEADS), lax.rem(t // nq, DA_HEADS), lax.rem(t, nq)

    def qk_tile(g):
        return tile(jnp.minimum(g, n_tiles - 1))

    def pv_tile(g):
        return tile(jnp.clip(g - 2, 0, n_tiles - 1))

    in_specs = [pl.BlockSpec(memory_space=pltpu.SMEM),
                pl.BlockSpec((1, tq, hw), lambda g: (qk_tile(g)[0], qk_tile(g)[2], qk_tile(g)[1]))]
    in_specs += [pl.BlockSpec((1, kv.shape[1], hw), lambda g: (qk_tile(g)[0], 0, DA_HEADS + qk_tile(g)[1]))
                 for kv in kv_arrs]
    in_specs += [pl.BlockSpec((1, kv.shape[1], hw), lambda g: (pv_tile(g)[0], 0, 2 * DA_HEADS + pv_tile(g)[1]))
                 for kv in kv_arrs]
    in_specs += [pl.BlockSpec((hw, 1), lambda g: (0, 0))]
    s_buf = pltpu.VMEM((2, n_keys, tq), F32)
    p_buf = pltpu.VMEM((2, n_keys, tq), BF16)
    mx_buf = pltpu.VMEM((2, 8, tq), F32)
    return pl.pallas_call(
        functools.partial(_da_kernel, n_seg=n_seg, post_scale=1.0 - lam_init, n_tiles=n_tiles, nq=nq),
        grid=(n_tiles + 2,),
        in_specs=in_specs,
        out_specs=pl.BlockSpec((1, tq, hw), lambda g: (pv_tile(g)[0], pv_tile(g)[2], pv_tile(g)[1])),
        out_shape=jax.ShapeDtypeStruct((b, lq, D_MODEL), BF16),
        scratch_shapes=[pltpu.VMEM((n_keys, hw), BF16), pltpu.VMEM((hw + DA_SUM_ROWS, n_keys), BF16),
                        s_buf, s_buf, p_buf, p_buf, mx_buf, mx_buf],
        compiler_params=_cparams(("arbitrary",)),
        name=name,
    )(lam.reshape(1, 1), q_arr, *kv_arrs, *kv_arrs, subln_g.reshape(hw, 1))


WA_SUM_ROWS = 16


def _wa_kernel(*refs, windowed, seq_len):
    if windowed:
        sink_ref, q_ref, kc_ref, vc_ref, kp_ref, kq_ref, kn_ref, vp_ref, vq_ref, vn_ref, o_ref = refs
        k_all = jnp.concatenate([kc_ref[0], kp_ref[0], kq_ref[0], kn_ref[0]], axis=0)
        v_all = jnp.concatenate([vc_ref[0], vp_ref[0], vq_ref[0], vn_ref[0]], axis=0)
    else:
        sink_ref, q_ref, kc_ref, vc_ref, o_ref = refs
        k_all, v_all = kc_ref[0], vc_ref[0]
    tq = q_ref.shape[1]
    n_keys = k_all.shape[0]
    n_ctx = kc_ref.shape[1]
    n_cols = 2 * tq
    i = pl.program_id(1)
    col = lax.broadcasted_iota(jnp.int32, (1, n_cols), 1)
    first = col < tq
    if windowed:
        row = lax.broadcasted_iota(jnp.int32, (n_keys, n_cols), 0)
        qpos = i * tq + jnp.where(first, col, col - tq)
        kpos = (i - 1) * tq + row - n_ctx
        ok = (row < n_ctx) | ((jnp.abs(qpos - kpos) <= WINDOW) & (kpos >= 0) & (kpos < seq_len))
    lane = lax.broadcasted_iota(jnp.int32, (1, LANES), 1)
    lo = lane < WA_HEAD
    log2e = math.log2(math.e)
    qscale = WA_HEAD ** -0.5 * log2e
    rep = WA_Q_HEADS // WA_KV_HEADS
    v_t = v_all.astype(F32).T
    sum_rows = jnp.where(lax.broadcasted_iota(jnp.int32, (WA_SUM_ROWS, n_keys), 0) == 0, 1.0, 0.0)
    def scores(g, half):
        kg = k_all[:, (g // 2) * LANES:(g // 2 + 1) * LANES]
        zero = jnp.zeros_like(kg)
        k_same = jnp.where(lo, kg, zero) if g % 2 == 0 else jnp.where(lo, zero, kg)
        if (g % 2 == 0) == (half == 0):
            k_half = k_same
        else:
            k_half = pltpu.roll(k_same.astype(F32), WA_HEAD, 1).astype(BF16)
        chunks = [q_ref[0, :, (g * (rep // 2) + pair) * LANES:(g * (rep // 2) + pair + 1) * LANES]
                  for pair in range(rep // 2)]
        qs = (jnp.concatenate(chunks, axis=0).astype(F32) * qscale).astype(BF16)
        return _dot_nt(k_half, qs)

    def weighted(g, half, s_t):
        if windowed:
            s_t = jnp.where(ok, s_t, -1e30)
        snk = jnp.where(first, sink_ref[g * rep + half], sink_ref[g * rep + 2 + half]) * log2e
        mx = jnp.maximum(jnp.max(s_t, axis=0, keepdims=True), snk)
        vt_g = jnp.concatenate([v_t[g * WA_HEAD:(g + 1) * WA_HEAD, :], sum_rows], axis=0).astype(BF16)
        acc = _dot(vt_g, jnp.exp2((s_t - mx).astype(BF16)))
        l = acc[WA_HEAD:WA_HEAD + 1] + jnp.exp2(snk - mx)
        return acc[:WA_HEAD] * (1.0 / l)

    units = [(g, half) for g in range(WA_KV_HEADS) for half in range(2)]
    outs = []
    nxt = scores(*units[0])
    for u, (g, half) in enumerate(units):
        cur = nxt
        if u + 1 < len(units):
            nxt = scores(*units[u + 1])
        outs.append(weighted(g, half, cur))
        if half == 1:
            for pair in range(rep // 2):
                cols = slice(pair * tq, (pair + 1) * tq)
                o_t = jnp.concatenate([outs[-2][:, cols], outs[-1][:, cols]], axis=0)
                chunk = g * (rep // 2) + pair
                o_ref[0, :, chunk * LANES:(chunk + 1) * LANES] = o_t.T.astype(o_ref.dtype)


def _window_attention(q_arr, ctx_arr, sink, *, windowed, name):
    b, lq, _ = q_arr.shape
    n_ctx = ctx_arr.shape[1]
    qw = WA_Q_HEADS * WA_HEAD
    kw = WA_KV_HEADS * WA_HEAD
    k_col, v_col = qw // kw, qw // kw + 1
    tq = 128 if windowed else lq
    nq = lq // tq
    in_specs = [pl.BlockSpec(memory_space=pltpu.SMEM),
                pl.BlockSpec((1, tq, qw), lambda bi, i: (bi, i, 0)),
                pl.BlockSpec((1, n_ctx, kw), lambda bi, i: (bi, 0, k_col)),
                pl.BlockSpec((1, n_ctx, kw), lambda bi, i: (bi, 0, v_col))]
    args = [sink, q_arr, ctx_arr, ctx_arr]
    if windowed:
        assert tq == WINDOW
        for col in (k_col, v_col):
            in_specs += [pl.BlockSpec((1, tq, kw), lambda bi, i, col=col: (bi, jnp.maximum(i - 1, 0), col)),
                         pl.BlockSpec((1, tq, kw), lambda bi, i, col=col: (bi, i, col)),
                         pl.BlockSpec((1, tq, kw), lambda bi, i, col=col: (bi, jnp.minimum(i + 1, nq - 1), col))]
            args += [q_arr, q_arr, q_arr]
    return pl.pallas_call(
        functools.partial(_wa_kernel, windowed=windowed, seq_len=lq),
        grid=(b, nq),
        in_specs=in_specs,
        out_specs=pl.BlockSpec((1, tq, qw), lambda bi, i: (bi, i, 0)),
        out_shape=jax.ShapeDtypeStruct((b, lq, qw), BF16),
        compiler_params=_cparams(("parallel", "parallel")),
        name=name,
    )(*args)


def _ret_kernel(cd_ref, qcf, kcf, vcf, qlf, klf, vlf, qcb, kcb, vcb, qlb, klb, vlb, dtab_ref, qd_ref, kd_ref,
                ocf, olf, ocb, olb, s_scr, *, n_ctx_chunks):
    j = pl.program_id(1)
    is_ctx = j < n_ctx_chunks

    @pl.when(j == 0)
    def _():
        s_scr[...] = jnp.zeros_like(s_scr)

    kscale = jnp.asarray(RET_KEY ** -0.5, BF16)
    dirs = ((qcf, kcf, vcf, qlf, klf, vlf, ocf, olf), (qcb, kcb, vcb, qlb, klb, vlb, ocb, olb))
    for d, (qc, kc, vc, ql, kl, vl, oc, ol) in enumerate(dirs):
        q = jnp.where(is_ctx, qc[0], ql[0])
        k = jnp.where(is_ctx, kc[0], kl[0]) * kscale
        v = jnp.where(is_ctx, vc[0], vl[0])
        outs = []
        for h in range(RET_HEADS):
            qh = q[:, h * RET_KEY:(h + 1) * RET_KEY]
            kh = k[:, h * RET_KEY:(h + 1) * RET_KEY]
            vh = v[:, h * RET_VAL:(h + 1) * RET_VAL]
            state = s_scr[d, h]
            inner = _dot_nt(qh, kh) * dtab_ref[d, h]
            o = _dot(inner.astype(BF16), vh) + _dot(qh, state.astype(BF16)) * qd_ref[d, h]
            kdec = (kh.astype(F32) * kd_ref[d, h]).astype(BF16)
            s_scr[d, h] = state * cd_ref[d, h] + _dot_tn(kdec, vh)
            outs.append(o)
        o_all = jnp.concatenate(outs, axis=1).astype(oc.dtype)

        @pl.when(is_ctx)
        def _():
            oc[0] = o_all

        @pl.when(jnp.logical_not(is_ctx))
        def _():
            ol[0] = o_all


def _retention(ctx_arr, lat_arr, log_g):
    b, lc, _ = ctx_arr.shape
    l = lat_arr.shape[1]
    c = RET_CHUNK
    nc, nl = lc // c, l // c
    qw = RET_HEADS * RET_KEY
    vw = RET_HEADS * RET_VAL
    idx = jnp.arange(c, dtype=F32)
    rel = idx[:, None] - idx[None, :]
    lg = log_g[:, :, None, None]
    keep_f = (rel >= 0)[None]
    keep_b = (rel < 0)[None]
    d_f = jnp.where(keep_f, jnp.exp(jnp.where(keep_f, rel[None] * lg[0], 0.0)), 0.0)
    d_b = jnp.where(keep_b, jnp.exp(jnp.where(keep_b, -rel[None] * lg[1], 0.0)), 0.0)
    dtab = jnp.stack([d_f, d_b])
    col = idx[None, :, None]
    qd = jnp.stack([jnp.exp((col + 1.0) * lg[0]), jnp.exp((c - col) * lg[1])])
    kd = jnp.stack([jnp.exp((c - 1.0 - col) * lg[0]), jnp.exp(col * lg[1])])
    cd = jnp.exp(c * log_g)

    f_ctx = lambda bi, j: (bi, jnp.minimum(j, nc - 1))
    f_lat = lambda bi, j: (bi, jnp.maximum(j - nc, 0))
    b_ctx = lambda bi, j: (bi, jnp.maximum(nc - 1 - j, 0))
    b_lat = lambda bi, j: (bi, jnp.minimum(nc + nl - 1 - j, nl - 1))

    def qkv_specs(row_map):
        return [pl.BlockSpec((1, c, qw), lambda bi, j: row_map(bi, j) + (0,)),
                pl.BlockSpec((1, c, qw), lambda bi, j: row_map(bi, j) + (1,)),
                pl.BlockSpec((1, c, vw), lambda bi, j: row_map(bi, j) + (1,))]

    full = lambda shape: pl.BlockSpec(shape, lambda bi, j: (0,) * len(shape))
    in_specs = ([pl.BlockSpec(memory_space=pltpu.SMEM)]
                + qkv_specs(f_ctx) + qkv_specs(f_lat) + qkv_specs(b_ctx) + qkv_specs(b_lat)
                + [full(dtab.shape), full(qd.shape), full(kd.shape)])
    out_spec = lambda row_map: pl.BlockSpec((1, c, vw), lambda bi, j: row_map(bi, j) + (0,))
    out_sds = lambda n: jax.ShapeDtypeStruct((b, n, vw), BF16)
    return pl.pallas_call(
        functools.partial(_ret_kernel, n_ctx_chunks=nc),
        grid=(b, nc + nl),
        in_specs=in_specs,
        out_specs=[out_spec(f_ctx), out_spec(f_lat), out_spec(b_ctx), out_spec(b_lat)],
        out_shape=[out_sds(lc), out_sds(l), out_sds(lc), out_sds(l)],
        scratch_shapes=[pltpu.VMEM((2, RET_HEADS, RET_KEY, RET_VAL), F32)],
        compiler_params=_cparams(("parallel", "arbitrary")),
        name="retention_scan",
    )(cd, ctx_arr, ctx_arr, ctx_arr, lat_arr, lat_arr, lat_arr, ctx_arr, ctx_arr, ctx_arr, lat_arr, lat_arr,
      lat_arr, dtab, qd, kd)


def _router_kernel(x_ref, g_ref, sh_ref, sc_ref, wr_ref, a_ref, lg_ref):
    a = _modulated(x_ref, g_ref, sh_ref, sc_ref)
    a_hi = a.astype(BF16)
    a_ref[...] = a_hi
    a_lo = (a - a_hi.astype(F32)).astype(BF16)
    w = wr_ref[...]
    w_hi = w.astype(BF16)
    w_lo = (w - w_hi.astype(F32)).astype(BF16)
    logits = _dot(a_hi, w_hi) + _dot(a_lo, w_hi) + _dot(a_hi, w_lo)
    lane = lax.broadcasted_iota(jnp.int32, logits.shape, 1)
    neg = -jnp.inf
    lg = jnp.where(lane < N_EXPERTS, logits, neg)
    v0 = jnp.max(lg, axis=-1, keepdims=True)
    i0 = jnp.min(jnp.where(lg == v0, lane, ROUTER_PAD), axis=-1, keepdims=True)
    lg = jnp.where(lane == i0, neg, lg)
    v1 = jnp.max(lg, axis=-1, keepdims=True)
    i1 = jnp.min(jnp.where(lg == v1, lane, ROUTER_PAD), axis=-1, keepdims=True)
    e1 = jnp.exp(v1 - v0)
    g0 = 1.0 / (1.0 + e1)
    route = jnp.where(lane == 0, g0, jnp.where(lane == 1, e1 * g0, jnp.where(
        lane == 2, i0.astype(F32), jnp.where(lane == 3, i1.astype(F32), 0.0))))
    lg_ref[...] = route


def _router(x, g, mod, sh_idx, rows_per_mod, w_router_pad, *, tm, name):
    t, d = x.shape
    if rows_per_mod is None:
        mod_row = lambda i: 0
    else:
        tiles_per_mod = rows_per_mod // tm
        mod_row = lambda i: i // tiles_per_mod
    return pl.pallas_call(
        _router_kernel,
        grid=(t // tm,),
        in_specs=[pl.BlockSpec((tm, d), lambda i: (i, 0)),
                  pl.BlockSpec((1, d), lambda i: (0, 0)),
                  pl.BlockSpec((1, 1, d), lambda i: (mod_row(i), 0, sh_idx)),
                  pl.BlockSpec((1, 1, d), lambda i: (mod_row(i), 0, sh_idx + 1)),
                  pl.BlockSpec((d, ROUTER_PAD), lambda i: (0, 0))],
        out_specs=[pl.BlockSpec((tm, d), lambda i: (i, 0)), pl.BlockSpec((tm, ROUTER_PAD), lambda i: (i, 0))],
        out_shape=[jax.ShapeDtypeStruct((t, d), BF16), jax.ShapeDtypeStruct((t, ROUTER_PAD), F32)],
        compiler_params=_cparams(("parallel",)),
        name=name,
    )(x, g.reshape(1, d), mod, mod, w_router_pad)


def _experts_kernel(be_ref, nb_ref, x_ref, wg_ref, wu_ref, wd_ref, o_ref, acc, *, nf):
    blk = pl.program_id(0)
    f = pl.program_id(1)
    used = blk < nb_ref[0]
    rc = min(x_ref.shape[0], MOE_ROW_CHUNK)

    @pl.when((blk == 0) & (f == 0))
    def _():
        acc[...] = jnp.zeros_like(acc)

    @pl.when(used)
    def _():
        wg = wg_ref[0, 0].astype(BF16)
        wu = wu_ref[0, 0].astype(BF16)
        wd = wd_ref[0, 0].astype(BF16)
        n_chunks = x_ref.shape[0] // rc

        def gate_up(c):
            x = x_ref[c * rc:(c + 1) * rc, :]
            return _dot(x, wg), _dot(x, wu)

        nxt = gate_up(0)
        for c in range(n_chunks):
            g, u = nxt
            if c + 1 < n_chunks:
                nxt = gate_up(c + 1)
            part = _dot((g * _sigmoid(g) * u).astype(BF16), wd)
            rows = slice(c * rc, (c + 1) * rc)
            acc[rows, :] = part + jnp.where(f > 0, acc[rows, :], 0.0)

    @pl.when(f == nf - 1)
    def _():
        o_ref[...] = jnp.where(used, acc[...], 0.0).astype(o_ref.dtype)


def _experts(buf, block_expert, n_used, w_gu, w_down, layer):
    rows, d = buf.shape
    nb = rows // MOE_ROWS
    two_f = w_gu.shape[3]
    fdim = two_f // 2
    fc = _pick(fdim, (MOE_FC, 256, 128))
    nf = fdim // fc
    grid_spec = pltpu.PrefetchScalarGridSpec(
        num_scalar_prefetch=2,
        grid=(nb, nf),
        in_specs=[pl.BlockSpec((MOE_ROWS, d), lambda i, f, be, nu: (i, 0)),
                  pl.BlockSpec((1, 1, d, fc), lambda i, f, be, nu: (layer, be[i], 0, f)),
                  pl.BlockSpec((1, 1, d, fc), lambda i, f, be, nu: (layer, be[i], 0, nf + f)),
                  pl.BlockSpec((1, 1, fc, d), lambda i, f, be, nu: (layer, be[i], f, 0))],
        out_specs=pl.BlockSpec((MOE_ROWS, d), lambda i, f, be, nu: (i, 0)),
        scratch_shapes=[pltpu.VMEM((MOE_ROWS, d), F32)],
    )
    return pl.pallas_call(
        functools.partial(_experts_kernel, nf=nf),
        grid_spec=grid_spec,
        out_shape=jax.ShapeDtypeStruct((rows, d), BF16),
        compiler_params=_cparams(("parallel", "arbitrary")),
        name="moe_experts",
    )(block_expert, n_used, buf, w_gu, w_gu, w_down)


def _moe(a_all, top_idx, w_gu, w_down, layer):
    t, d = a_all.shape
    e = w_gu.shape[1]
    onehot = jnp.sum((top_idx[..., None] == jnp.arange(e)[None, None, :]).astype(jnp.int32), axis=1)
    counts = jnp.sum(onehot, axis=0)
    rank = jnp.cumsum(onehot, axis=0) - onehot
    padded = (counts + MOE_ROWS - 1) // MOE_ROWS * MOE_ROWS
    pend = jnp.cumsum(padded)
    pstart = pend - padded
    dest = pstart[top_idx] + jnp.take_along_axis(rank, top_idx, axis=1)
    nb = -(-(t * TOP_K) // MOE_ROWS) + e
    tok = jnp.repeat(jnp.arange(t, dtype=jnp.int32), TOP_K)
    row_src = jnp.zeros((nb * MOE_ROWS,), jnp.int32).at[dest.reshape(-1)].set(tok, unique_indices=True)
    buf = a_all[row_src]
    block_expert = jnp.minimum(jnp.searchsorted(pend, jnp.arange(nb) * MOE_ROWS, side="right"), e - 1).astype(jnp.int32)
    n_used = (pend[-1] // MOE_ROWS).astype(jnp.int32).reshape(1)
    return _experts(buf, block_expert, n_used, w_gu, w_down, layer), dest


def _rope_angles(pos, dim):
    inv = ROPE_BASE ** (-jnp.arange(0, dim, 2, dtype=F32) / dim)
    ang = pos.astype(F32)[:, None] * inv[None, :]
    return jnp.concatenate([ang, ang], axis=-1)


def _signed_sin(sin, shift):
    low = (jnp.arange(sin.shape[1]) % (2 * shift)) < shift
    return jnp.where(low[None, :], -sin, 0.0), jnp.where(low[None, :], 0.0, sin)


def _axial_tables(n_tokens, head_dim):
    rows = n_tokens // GRID_W
    row = jnp.repeat(jnp.arange(rows, dtype=jnp.int32), GRID_W)
    col = jnp.tile(jnp.arange(GRID_W, dtype=jnp.int32), rows)
    half = head_dim // 2
    ang = jnp.concatenate([_rope_angles(row, half), _rope_angles(col, half)], axis=-1)
    ang = jnp.tile(ang, (1, LANES // head_dim))
    sa, sb = _signed_sin(jnp.sin(ang), half // 2)
    return jnp.cos(ang), sa, sb, half // 2


def _ret_tables(n_tokens):
    ang = _rope_angles(jnp.arange(n_tokens), RET_KEY)
    sa, sb = _signed_sin(jnp.sin(ang), RET_KEY // 2)
    return jnp.cos(ang), sa, sb, RET_KEY // 2


def kernel(x, c, ctx, c_ctx, mod_w, mod_b, norm_g, da_w_in, da_w_out, da_lambda, da_subln_g, wa_w_in, wa_w_out,
           wa_sink, ret_w_in, ret_w_out, ret_decay_logit, ffn_w_gu, ffn_w_down, moe_router, moe_w_gu, moe_w_down):
    b, l, d = x.shape
    lc = ctx.shape[1]
    depth = mod_w.shape[0]
    t, tc = b * l, b * lc
    tm = _pick(l, (1024, 512, 256, 128))
    tmc = _pick(tc, (1024, 512, 256, 128))

    n_cond = -(-(b + 1) // 8) * 8
    cs = jnp.zeros((n_cond, d), F32).at[:b].set(c).at[b].set(c_ctx)
    mod_all = _mod_vectors(cs, mod_w, mod_b)

    xs = x.reshape(t, d)
    hs = ctx.reshape(tc, d)
    mixer_count = [0] * N_MIXERS
    i_dense = 0
    i_moe = 0
    for layer in range(depth):
        last = layer == depth - 1
        mod = mod_all[layer, :b].reshape(b, 1, 6 * d)
        mod_c = mod_all[layer, b:b + 1].reshape(1, 1, 6 * d)
        ng = norm_g[layer]
        kind = layer % N_MIXERS
        jm = mixer_count[kind]
        mixer_count[kind] += 1

        if kind == 0:
            lam_init = 0.8 - 0.6 * math.exp(-0.3 * layer)
            w_in = da_w_in[jm].astype(BF16)
            cos, sa, sb, shift = _axial_tables(l, DA_HEAD)
            qkv = _norm_proj(xs, ng[0], mod, 0, l, w_in, tm=tm, tn=3 * d, rot=(cos, sa, sb, 2 * d, shift, l),
                             name="da_in_proj").reshape(b, l, 3 * d)
            qkv_c = _norm_proj(hs, ng[0], mod_c, 0, None, w_in, tm=tmc, tn=3 * d, name="da_in_proj_ctx").reshape(b, lc, 3 * d)
            lp = da_lambda[jm].astype(F32)
            lam = jnp.exp(jnp.sum(lp[0] * lp[1])) - jnp.exp(jnp.sum(lp[2] * lp[3])) + lam_init
            o = _diff_attention(qkv, [qkv_c, qkv], lam, da_subln_g[jm], lam_init, tq=_pick(l, (DA_TQ, 128)), name="diff_attn")
            w_out = da_w_out[jm].astype(BF16)
            xs = _resid(xs, ng[1], mod, 2, l, tm=tm, a=o.reshape(t, d), w=w_out, name="da_out_proj")
            if not last:
                oc = _diff_attention(qkv_c, [qkv_c], lam, da_subln_g[jm], lam_init, tq=_pick(lc, (DA_TQ, 128)), name="diff_attn_ctx")
                hs = _resid(hs, ng[1], mod_c, 2, None, tm=tmc, a=oc.reshape(tc, d), w=w_out, name="da_out_proj_ctx")
        elif kind == 1:
            w_in = wa_w_in[jm].astype(BF16)
            n_qkv = w_in.shape[1]
            cos, sa, sb, shift = _axial_tables(l, WA_HEAD)
            n_rot = (WA_Q_HEADS + WA_KV_HEADS) * WA_HEAD
            qkv = _norm_proj(xs, ng[0], mod, 0, l, w_in, tm=tm, tn=n_qkv, rot=(cos, sa, sb, n_rot, shift, l),
                             name="wa_in_proj").reshape(b, l, n_qkv)
            qkv_c = _norm_proj(hs, ng[0], mod_c, 0, None, w_in, tm=tmc, tn=n_qkv, name="wa_in_proj_ctx").reshape(b, lc, n_qkv)
            o = _window_attention(qkv, qkv_c, wa_sink[jm], windowed=True, name="window_attn")
            w_out = wa_w_out[jm].astype(BF16)
            xs = _resid(xs, ng[1], mod, 2, l, tm=tm, a=o.reshape(t, d), w=w_out, name="wa_out_proj")
            if not last:
                oc = _window_attention(qkv_c, qkv_c, wa_sink[jm], windowed=False, name="window_attn_ctx")
                hs = _resid(hs, ng[1], mod_c, 2, None, tm=tmc, a=oc.reshape(tc, d), w=w_out, name="wa_out_proj_ctx")
        else:
            w_in = ret_w_in[jm].astype(BF16)
            n_qkv = w_in.shape[1]
            cos, sa, sb, shift = _ret_tables(l)
            n_rot = 2 * RET_HEADS * RET_KEY
            tr = _pick(l, (512, 256, 128))
            qkv = _norm_proj(xs, ng[0], mod, 0, l, w_in, tm=tr, tn=n_qkv // 2, rot=(cos, sa, sb, n_rot, shift, l),
                             name="ret_in_proj")
            qkv_c = _norm_proj(hs, ng[0], mod_c, 0, None, w_in, tm=_pick(tc, (512, 256, 128)), tn=n_qkv // 2,
                               name="ret_in_proj_ctx")
            log_g = jax.nn.log_sigmoid(ret_decay_logit[jm].astype(F32))
            ocf, olf, ocb, olb = _retention(qkv_c.reshape(b, lc, n_qkv), qkv.reshape(b, l, n_qkv), log_g)
            w_out = ret_w_out[jm].astype(BF16)
            vw = RET_HEADS * RET_VAL
            xs = _resid(xs, ng[1], mod, 2, l, tm=_pick(l, (512, 256, 128)), w=w_out,
                        ret=(olf.reshape(t, vw), olb.reshape(t, vw), qkv), name="ret_out_proj")
            if not last:
                hs = _resid(hs, ng[1], mod_c, 2, None, tm=_pick(tc, (512, 256, 128)), w=w_out,
                            ret=(ocf.reshape(tc, vw), ocb.reshape(tc, vw), qkv_c), name="ret_out_proj_ctx")

        if layer % 2 == 0:
            w_gu = ffn_w_gu[i_dense].astype(BF16)
            w_dn = ffn_w_down[i_dense].astype(BF16)
            i_dense += 1
            f = w_dn.shape[0]
            w_g, w_u = w_gu[:, :f], w_gu[:, f:]
            tn = f
            tf = _pick(l, (512, 256, 128))
            act = _norm_proj(xs, ng[2], mod, 3, l, (w_g, w_u), tm=tf, tn=tn, name="ffn_up")
            xs = _resid(xs, ng[3], mod, 5, l, tm=tf, a=act, w=w_dn, name="ffn_down")
            if not last:
                tfc = _pick(tc, (512, 256, 128))
                act_c = _norm_proj(hs, ng[2], mod_c, 3, None, (w_g, w_u), tm=tfc, tn=tn, name="ffn_up_ctx")
                hs = _resid(hs, ng[3], mod_c, 5, None, tm=tfc, a=act_c, w=w_dn, name="ffn_down_ctx")
        else:
            w_r = jnp.zeros((d, ROUTER_PAD), F32).at[:, :N_EXPERTS].set(moe_router[i_moe])
            moe_layer = i_moe
            i_moe += 1
            a_x, route_x = _router(xs, ng[2], mod, 3, l, w_r, tm=tm, name="moe_router")
            if not last:
                a_c, route_c = _router(hs, ng[2], mod_c, 3, None, w_r, tm=tmc, name="moe_router_ctx")
                a_all = jnp.concatenate([a_x, a_c], axis=0)
                top_idx = jnp.concatenate([route_x[:, 2:4], route_c[:, 2:4]], axis=0).astype(jnp.int32)
            else:
                a_all, top_idx = a_x, route_x[:, 2:4].astype(jnp.int32)
            yb, dest = _moe(a_all, top_idx, moe_w_gu, moe_w_down, moe_layer)
            xs = _resid(xs, ng[3], mod, 5, l, tm=tm, mix=(yb[dest[:t, 0]], yb[dest[:t, 1]], route_x), name="moe_combine")
            if not last:
                hs = _resid(hs, ng[3], mod_c, 5, None, tm=tmc, mix=(yb[dest[t:, 0]], yb[dest[t:, 1]], route_c),
                            name="moe_combine_ctx")
    return xs.reshape(b, l, d)
```

```python
import functools
import math

import jax
import jax.numpy as jnp
from jax import lax
from jax.experimental import pallas as pl
from jax.experimental.pallas import tpu as pltpu

F32 = jnp.float32
BF16 = jnp.bfloat16

D_MODEL = 1024
N_MIXERS = 3
NORM_EPS = 1e-6
ROPE_BASE = 10000.0
GRID_W = 64
DA_HEAD = 64
DA_HEADS = D_MODEL // (2 * DA_HEAD)
WA_HEAD = 64
WA_Q_HEADS = D_MODEL // WA_HEAD
WA_KV_HEADS = WA_Q_HEADS // 4
WINDOW = 128
RET_KEY = 256
RET_HEADS = D_MODEL // RET_KEY
RET_VAL = 2 * RET_KEY
RET_CHUNK = 128
N_EXPERTS = 8
TOP_K = 2

LANES = 128
V7X_VMEM_LIMIT = 56 * 1024 * 1024
ROUTER_PAD = LANES
MOE_ROWS = 1024
MOE_FC = 512
MOE_ROW_CHUNK = 256
MOE_GROUPS = 4
PROJ_COLS = 512


def _cparams(sem):
    return pltpu.CompilerParams(dimension_semantics=sem, vmem_limit_bytes=V7X_VMEM_LIMIT)


def _pick(n, prefs):
    for p in prefs:
        if n % p == 0:
            return p
    return n


def _sigmoid(x):
    return 1.0 / (1.0 + jnp.exp(-x))


def _rms(y):
    return y * lax.rsqrt(jnp.mean(y * y, axis=-1, keepdims=True) + NORM_EPS)


def _dot(a, b):
    return jnp.dot(a, b, preferred_element_type=F32)


def _dot_nt(a, b):
    return lax.dot_general(a, b, (((1,), (1,)), ((), ())), preferred_element_type=F32)


def _dot_tn(a, b):
    return lax.dot_general(a, b, (((0,), (0,)), ((), ())), preferred_element_type=F32)


def _mod_kernel(c_ref, w_ref, b_ref, o_ref):
    cs = c_ref[...]
    s = cs * _sigmoid(cs)
    w = w_ref[0]
    s_hi = s.astype(BF16)
    s_lo = (s - s_hi.astype(F32)).astype(BF16)
    w_hi = w.astype(BF16)
    w_lo = (w - w_hi.astype(F32)).astype(BF16)
    o_ref[0] = _dot(s_hi, w_hi) + _dot(s_lo, w_hi) + _dot(s_hi, w_lo) + b_ref[0]


def _mod_vectors(cs, mod_w, mod_b):
    depth, d, n = mod_w.shape
    r = cs.shape[0]
    tn = _pick(n, (1024, 512, 256, 128))
    return pl.pallas_call(
        _mod_kernel,
        grid=(depth, n // tn),
        in_specs=[
            pl.BlockSpec((r, d), lambda l, j: (0, 0)),
            pl.BlockSpec((1, d, tn), lambda l, j: (l, 0, j)),
            pl.BlockSpec((1, 1, tn), lambda l, j: (l, 0, j)),
        ],
        out_specs=pl.BlockSpec((1, r, tn), lambda l, j: (l, 0, j)),
        out_shape=jax.ShapeDtypeStruct((depth, r, n), F32),
        compiler_params=_cparams(("parallel", "parallel")),
        name="mod_vectors",
    )(cs, mod_w, mod_b.reshape(depth, 1, n))


def _modulated(x_ref, g_ref, sh_ref, sc_ref):
    x = x_ref[...]
    return _rms(x) * g_ref[...] * (1.0 + sc_ref[0]) + sh_ref[0]


def _rotate(acc, cos, sa, sb, shift, cw):
    outs = []
    for c in range(acc.shape[1] // cw):
        y = acc[:, c * cw:(c + 1) * cw]
        up = pltpu.roll(y, cw - shift, 1)
        dn = pltpu.roll(y, shift, 1)
        outs.append(y * cos + up * sa + dn * sb)
    return outs[0] if len(outs) == 1 else jnp.concatenate(outs, axis=1)


def _col_chunks(width, cuts):
    edges = sorted({0, width} | set(range(PROJ_COLS, width, PROJ_COLS)) | {c for c in cuts if 0 < c < width})
    return list(zip(edges[:-1], edges[1:]))


def _proj_kernel(*refs, mode, n_rot_cols, shift, cw, n_col_tiles):
    if mode == "swiglu":
        x_ref, g_ref, sh_ref, sc_ref, wg_ref, wu_ref, o_ref, a_scr = refs
    elif mode == "rot":
        x_ref, g_ref, sh_ref, sc_ref, w_ref, cos_ref, sa_ref, sb_ref, o_ref, a_scr = refs
    else:
        x_ref, g_ref, sh_ref, sc_ref, w_ref, o_ref, a_scr = refs
    j = pl.program_id(1)
    tn = o_ref.shape[1]

    @pl.when(j == 0)
    def _():
        a_scr[...] = _modulated(x_ref, g_ref, sh_ref, sc_ref).astype(BF16)

    def tile(jt):
        a = a_scr[...]
        for lo, hi in _col_chunks(tn, [n_rot_cols - jt * tn]):
            if mode == "swiglu":
                g = _dot(a, wg_ref[:, lo:hi])
                res = g * _sigmoid(g) * _dot(a, wu_ref[:, lo:hi])
            else:
                res = _dot(a, w_ref[:, lo:hi])
                if mode == "rot" and jt * tn + hi <= n_rot_cols:
                    res = _rotate(res, cos_ref[...], sa_ref[...], sb_ref[...], shift, cw)
            o_ref[:, lo:hi] = res.astype(o_ref.dtype)

    if mode != "rot":
        tile(0)
    else:
        n_rot_tiles = -(-n_rot_cols // tn)
        for jt in range(min(n_rot_tiles + 1, n_col_tiles)):
            @pl.when((j >= jt) if jt == n_rot_tiles else (j == jt))
            def _(jt=jt):
                tile(jt)


def _norm_proj(x, g, mod, sh_idx, rows_per_mod, weights, *, tm, tn, rot=None, name):
    t, d = x.shape
    swiglu = isinstance(weights, tuple)
    n = weights[0].shape[1] if swiglu else weights.shape[1]
    assert t % tm == 0 and n % tn == 0
    if rows_per_mod is None:
        mod_row = lambda i: 0
    else:
        assert rows_per_mod % tm == 0
        tiles_per_mod = rows_per_mod // tm
        mod_row = lambda i: i // tiles_per_mod
    in_specs = [
        pl.BlockSpec((tm, d), lambda i, j: (i, 0)),
        pl.BlockSpec((1, d), lambda i, j: (0, 0)),
        pl.BlockSpec((1, 1, d), lambda i, j: (mod_row(i), 0, sh_idx)),
        pl.BlockSpec((1, 1, d), lambda i, j: (mod_row(i), 0, sh_idx + 1)),
    ]
    args = [x, g.reshape(1, d), mod, mod]
    w_spec = pl.BlockSpec((d, tn), lambda i, j: (0, j))
    n_rot_cols, shift, cw = 0, 0, 0
    if swiglu:
        mode = "swiglu"
        in_specs += [w_spec, w_spec]
        args += list(weights)
    elif rot is not None:
        mode = "rot"
        cos, sa, sb, n_rot_cols, shift, seq_len = rot
        cw = cos.shape[1]
        assert n_rot_cols % cw == 0 and PROJ_COLS % cw == 0 and seq_len % tm == 0
        tiles_per_seq = seq_len // tm
        t_spec = pl.BlockSpec((tm, cw), lambda i, j: (i % tiles_per_seq, 0))
        in_specs += [w_spec, t_spec, t_spec, t_spec]
        args += [weights, cos, sa, sb]
    else:
        mode = "plain"
        in_specs += [w_spec]
        args += [weights]
    return pl.pallas_call(
        functools.partial(_proj_kernel, mode=mode, n_rot_cols=n_rot_cols, shift=shift, cw=cw, n_col_tiles=n // tn),
        grid=(t // tm, n // tn),
        in_specs=in_specs,
        out_specs=pl.BlockSpec((tm, tn), lambda i, j: (i, j)),
        out_shape=jax.ShapeDtypeStruct((t, n), BF16),
        scratch_shapes=[pltpu.VMEM((tm, d), BF16)],
        compiler_params=_cparams(("parallel", "arbitrary")),
        name=name,
    )(*args)


def _resid_kernel(*refs, mode):
    if mode == "ret":
        of_ref, ob_ref, gt_ref, w_ref, x_ref, gate_ref, g_ref, o_ref = refs
        o = of_ref[...].astype(F32) + ob_ref[...].astype(F32)
        parts = []
        for h in range(RET_HEADS):
            parts.append(_rms(o[:, h * RET_VAL:(h + 1) * RET_VAL]))
        o = jnp.concatenate(parts, axis=1)
        gt = gt_ref[...].astype(F32)
        y = _dot((gt * _sigmoid(gt) * o).astype(BF16), w_ref[...])
    elif mode == "matmul":
        a_ref, w_ref, x_ref, gate_ref, g_ref, o_ref = refs
        y = _dot(a_ref[...], w_ref[...])
    else:
        y0_ref, y1_ref, route_ref, x_ref, gate_ref, g_ref, o_ref = refs
        route = route_ref[...]
        y = y0_ref[...].astype(F32) * route[:, 0:1] + y1_ref[...].astype(F32) * route[:, 1:2]
    o_ref[...] = x_ref[...] + gate_ref[0] * (_rms(y) * g_ref[...])


def _resid(x, g, mod, gate_idx, rows_per_mod, *, tm, name, a=None, w=None, mix=None, ret=None):
    t, d = x.shape
    assert t % tm == 0
    if rows_per_mod is None:
        mod_row = lambda i: 0
    else:
        assert rows_per_mod % tm == 0
        tiles_per_mod = rows_per_mod // tm
        mod_row = lambda i: i // tiles_per_mod
    row = lambda width: pl.BlockSpec((tm, width), lambda i: (i, 0))
    tail_specs = [row(d), pl.BlockSpec((1, 1, d), lambda i: (mod_row(i), 0, gate_idx)),
                  pl.BlockSpec((1, d), lambda i: (0, 0))]
    tail_args = [x, mod, g.reshape(1, d)]
    if ret is not None:
        mode = "ret"
        o_f, o_b, qkvg = ret
        vw = RET_HEADS * RET_VAL
        gate_col = qkvg.shape[1] // vw - 1
        in_specs = [row(vw), row(vw), pl.BlockSpec((tm, vw), lambda i: (i, gate_col)),
                    pl.BlockSpec(w.shape, lambda i: (0, 0))]
        args = [o_f, o_b, qkvg, w]
    elif a is not None:
        mode = "matmul"
        in_specs = [row(a.shape[1]), pl.BlockSpec(w.shape, lambda i: (0, 0))]
        args = [a, w]
    else:
        mode = "mix"
        in_specs = [row(d), row(d), row(ROUTER_PAD)]
        args = list(mix)
    return pl.pallas_call(
        functools.partial(_resid_kernel, mode=mode),
        grid=(t // tm,),
        in_specs=in_specs + tail_specs,
        out_specs=row(d),
        out_shape=jax.ShapeDtypeStruct((t, d), F32),
        compiler_params=_cparams(("parallel",)),
        name=name,
    )(*args, *tail_args)


DA_SUM_ROWS = 16
DA_TQ = 256
DA_EXP_ROWS = 64


def _da_kernel(*refs, n_seg, post_scale, n_tiles, nq):
    lam_ref, q_ref = refs[0], refs[1]
    k_refs = refs[2:2 + n_seg]
    v_refs = refs[2 + n_seg:2 + 2 * n_seg]
    sg_ref, o_ref, k_scr, vt_scr, s_a, s_b, p_a, p_b, mx_a, mx_b = refs[2 + 2 * n_seg:]
    hw = 2 * DA_HEAD
    n_keys = vt_scr.shape[1]
    tq = q_ref.shape[1]
    g = pl.program_id(0)
    t_qk = jnp.minimum(g, n_tiles - 1)
    t_pv = jnp.clip(g - 2, 0, n_tiles - 1)

    @pl.when(g == 0)
    def _():
        for buf in (s_a, s_b, p_a, p_b, mx_a, mx_b):
            buf[...] = jnp.zeros_like(buf)

    @pl.when(lax.rem(t_qk, nq) == 0)
    def _():
        off = 0
        for kr in k_refs:
            n = kr.shape[1]
            k_scr[off:off + n, :] = kr[0]
            off += n

    @pl.when(lax.rem(t_pv, nq) == 0)
    def _():
        off = 0
        for vr in v_refs:
            n = vr.shape[1]
            vt_scr[0:hw, off:off + n] = vr[0].astype(F32).T.astype(BF16)
            off += n
        row = lax.broadcasted_iota(jnp.int32, (DA_SUM_ROWS, n_keys), 0)
        vt_scr[hw:hw + DA_SUM_ROWS, :] = jnp.where(row == 0, 1.0, 0.0).astype(BF16)

    lane = lax.broadcasted_iota(jnp.int32, (1, hw), 1)
    qscale = DA_HEAD ** -0.5 * math.log2(math.e)

    def stages(s_new, mx_new, s_old, mx_old, p_old, p_older):
        accs = []
        for m in range(2):
            accs.append(_dot(vt_scr[...], p_older[m]))
        q = q_ref[0].astype(F32)
        for m in range(2):
            sel = (lane < DA_HEAD) if m == 0 else (lane >= DA_HEAD)
            qm = jnp.where(sel, q * qscale, 0.0).astype(BF16)
            s_t = _dot_nt(k_scr[...], qm)
            s_new[m] = s_t
            part = [None] * 4
            for r in range(n_keys // 8):
                tile = s_t[r * 8:(r + 1) * 8, :]
                j = r % len(part)
                part[j] = tile if part[j] is None else jnp.maximum(part[j], tile)
            mx_new[m] = functools.reduce(jnp.maximum, part)
            mx8 = jnp.broadcast_to(jnp.max(mx_old[m], axis=0, keepdims=True), (8, tq))
            for c in range(n_keys // DA_EXP_ROWS):
                rows = slice(c * DA_EXP_ROWS, (c + 1) * DA_EXP_ROWS)
                zero = jnp.minimum(jnp.abs(s_t[c * DA_EXP_ROWS:c * DA_EXP_ROWS + 8, :]), 0.0)
                d = s_old[m, rows, :].reshape(DA_EXP_ROWS // 8, 8, tq) - (mx8 + zero)[None]
                p_old[m, rows, :] = jnp.exp2(d.reshape(DA_EXP_ROWS, tq).astype(BF16))
        a0, a1 = accs
        o_t = a0[:hw] * (1.0 / a0[hw:hw + 1]) - a1[:hw] * (lam_ref[0, 0] / a1[hw:hw + 1])
        o_t = o_t * lax.rsqrt(jnp.mean(o_t * o_t, axis=0, keepdims=True) + NORM_EPS) * (sg_ref[...] * post_scale)
        o_ref[0] = o_t.T.astype(o_ref.dtype)

    @pl.when(lax.rem(g, 2) == 0)
    def _():
        stages(s_a, mx_a, s_b, mx_b, p_b, p_a)

    @pl.when(lax.rem(g, 2) == 1)
    def _():
        stages(s_b, mx_b, s_a, mx_a, p_a, p_b)


def _diff_attention(q_arr, kv_arrs, lam, subln_g, lam_init, *, tq, name):
    b, lq, _ = q_arr.shape
    hw = 2 * DA_HEAD
    n_seg = len(kv_arrs)
    n_keys = sum(kv.shape[1] for kv in kv_arrs)
    assert lq % tq == 0 and n_keys % DA_EXP_ROWS == 0
    nq = lq // tq
    n_tiles = b * DA_HEADS * nq

    def tile(t):
        return t // (nq * DA_HEADS), lax.rem(t // nq, DA_HEADS), lax.rem(t, nq)

    def qk_tile(g):
        return tile(jnp.minimum(g, n_tiles - 1))

    def pv_tile(g):
        return tile(jnp.clip(g - 2, 0, n_tiles - 1))

    in_specs = [pl.BlockSpec(memory_space=pltpu.SMEM),
                pl.BlockSpec((1, tq, hw), lambda g: (qk_tile(g)[0], qk_tile(g)[2], qk_tile(g)[1]))]
    in_specs += [pl.BlockSpec((1, kv.shape[1], hw), lambda g: (qk_tile(g)[0], 0, DA_HEADS + qk_tile(g)[1]))
                 for kv in kv_arrs]
    in_specs += [pl.BlockSpec((1, kv.shape[1], hw), lambda g: (pv_tile(g)[0], 0, 2 * DA_HEADS + pv_tile(g)[1]))
                 for kv in kv_arrs]
    in_specs += [pl.BlockSpec((hw, 1), lambda g: (0, 0))]
    s_buf = pltpu.VMEM((2, n_keys, tq), F32)
    p_buf = pltpu.VMEM((2, n_keys, tq), BF16)
    mx_buf = pltpu.VMEM((2, 8, tq), F32)
    return pl.pallas_call(
        functools.partial(_da_kernel, n_seg=n_seg, post_scale=1.0 - lam_init, n_tiles=n_tiles, nq=nq),
        grid=(n_tiles + 2,),
        in_specs=in_specs,
        out_specs=pl.BlockSpec((1, tq, hw), lambda g: (pv_tile(g)[0], pv_tile(g)[2], pv_tile(g)[1])),
        out_shape=jax.ShapeDtypeStruct((b, lq, D_MODEL), BF16),
        scratch_shapes=[pltpu.VMEM((n_keys, hw), BF16), pltpu.VMEM((hw + DA_SUM_ROWS, n_keys), BF16),
                        s_buf, s_buf, p_buf, p_buf, mx_buf, mx_buf],
        compiler_params=_cparams(("arbitrary",)),
        name=name,
    )(lam.reshape(1, 1), q_arr, *kv_arrs, *kv_arrs, subln_g.reshape(hw, 1))


WA_SUM_ROWS = 16


def _wa_kernel(*refs, windowed, seq_len):
    if windowed:
        sink_ref, q_ref, kc_ref, vc_ref, kp_ref, kq_ref, kn_ref, vp_ref, vq_ref, vn_ref, o_ref = refs
        k_all = jnp.concatenate([kc_ref[0], kp_ref[0], kq_ref[0], kn_ref[0]], axis=0)
        v_all = jnp.concatenate([vc_ref[0], vp_ref[0], vq_ref[0], vn_ref[0]], axis=0)
    else:
        sink_ref, q_ref, kc_ref, vc_ref, o_ref = refs
        k_all, v_all = kc_ref[0], vc_ref[0]
    tq = q_ref.shape[1]
    n_keys = k_all.shape[0]
    n_ctx = kc_ref.shape[1]
    n_cols = 2 * tq
    i = pl.program_id(1)
    col = lax.broadcasted_iota(jnp.int32, (1, n_cols), 1)
    first = col < tq
    if windowed:
        row = lax.broadcasted_iota(jnp.int32, (n_keys, n_cols), 0)
        qpos = i * tq + jnp.where(first, col, col - tq)
        kpos = (i - 1) * tq + row - n_ctx
        ok = (row < n_ctx) | ((jnp.abs(qpos - kpos) <= WINDOW) & (kpos >= 0) & (kpos < seq_len))
    lane = lax.broadcasted_iota(jnp.int32, (1, LANES), 1)
    lo = lane < WA_HEAD
    log2e = math.log2(math.e)
    qscale = WA_HEAD ** -0.5 * log2e
    rep = WA_Q_HEADS // WA_KV_HEADS
    v_t = v_all.astype(F32).T
    sum_rows = jnp.where(lax.broadcasted_iota(jnp.int32, (WA_SUM_ROWS, n_keys), 0) == 0, 1.0, 0.0)
    def scores(g, half):
        kg = k_all[:, (g // 2) * LANES:(g // 2 + 1) * LANES]
        zero = jnp.zeros_like(kg)
        k_same = jnp.where(lo, kg, zero) if g % 2 == 0 else jnp.where(lo, zero, kg)
        if (g % 2 == 0) == (half == 0):
            k_half = k_same
        else:
            k_half = pltpu.roll(k_same.astype(F32), WA_HEAD, 1).astype(BF16)
        chunks = [q_ref[0, :, (g * (rep // 2) + pair) * LANES:(g * (rep // 2) + pair + 1) * LANES]
                  for pair in range(rep // 2)]
        qs = (jnp.concatenate(chunks, axis=0).astype(F32) * qscale).astype(BF16)
        return _dot_nt(k_half, qs)

    def weighted(g, half, s_t):
        if windowed:
            s_t = jnp.where(ok, s_t, -1e30)
        snk = jnp.where(first, sink_ref[g * rep + half], sink_ref[g * rep + 2 + half]) * log2e
        mx = jnp.maximum(jnp.max(s_t, axis=0, keepdims=True), snk)
        vt_g = jnp.concatenate([v_t[g * WA_HEAD:(g + 1) * WA_HEAD, :], sum_rows], axis=0).astype(BF16)
        acc = _dot(vt_g, jnp.exp2((s_t - mx).astype(BF16)))
        l = acc[WA_HEAD:WA_HEAD + 1] + jnp.exp2(snk - mx)
        return acc[:WA_HEAD] * (1.0 / l)

    units = [(g, half) for g in range(WA_KV_HEADS) for half in range(2)]
    outs = []
    nxt = scores(*units[0])
    for u, (g, half) in enumerate(units):
        cur = nxt
        if u + 1 < len(units):
            nxt = scores(*units[u + 1])
        outs.append(weighted(g, half, cur))
        if half == 1:
            for pair in range(rep // 2):
                cols = slice(pair * tq, (pair + 1) * tq)
                o_t = jnp.concatenate([outs[-2][:, cols], outs[-1][:, cols]], axis=0)
                chunk = g * (rep // 2) + pair
                o_ref[0, :, chunk * LANES:(chunk + 1) * LANES] = o_t.T.astype(o_ref.dtype)


def _window_attention(q_arr, ctx_arr, sink, *, windowed, name):
    b, lq, _ = q_arr.shape
    n_ctx = ctx_arr.shape[1]
    qw = WA_Q_HEADS * WA_HEAD
    kw = WA_KV_HEADS * WA_HEAD
    k_col, v_col = qw // kw, qw // kw + 1
    tq = 128 if windowed else lq
    nq = lq // tq
    in_specs = [pl.BlockSpec(memory_space=pltpu.SMEM),
                pl.BlockSpec((1, tq, qw), lambda bi, i: (bi, i, 0)),
                pl.BlockSpec((1, n_ctx, kw), lambda bi, i: (bi, 0, k_col)),
                pl.BlockSpec((1, n_ctx, kw), lambda bi, i: (bi, 0, v_col))]
    args = [sink, q_arr, ctx_arr, ctx_arr]
    if windowed:
        assert tq == WINDOW
        for col in (k_col, v_col):
            in_specs += [pl.BlockSpec((1, tq, kw), lambda bi, i, col=col: (bi, jnp.maximum(i - 1, 0), col)),
                         pl.BlockSpec((1, tq, kw), lambda bi, i, col=col: (bi, i, col)),
                         pl.BlockSpec((1, tq, kw), lambda bi, i, col=col: (bi, jnp.minimum(i + 1, nq - 1), col))]
            args += [q_arr, q_arr, q_arr]
    return pl.pallas_call(
        functools.partial(_wa_kernel, windowed=windowed, seq_len=lq),
        grid=(b, nq),
        in_specs=in_specs,
        out_specs=pl.BlockSpec((1, tq, qw), lambda bi, i: (bi, i, 0)),
        out_shape=jax.ShapeDtypeStruct((b, lq, qw), BF16),
        compiler_params=_cparams(("parallel", "parallel")),
        name=name,
    )(*args)


def _ret_kernel(cd_ref, qcf, kcf, vcf, qlf, klf, vlf, qcb, kcb, vcb, qlb, klb, vlb, dtab_ref, qd_ref, kd_ref,
                ocf, olf, ocb, olb, s_scr, *, n_ctx_chunks):
    j = pl.program_id(1)
    is_ctx = j < n_ctx_chunks

    @pl.when(j == 0)
    def _():
        s_scr[...] = jnp.zeros_like(s_scr)

    kscale = jnp.asarray(RET_KEY ** -0.5, BF16)
    dirs = ((qcf, kcf, vcf, qlf, klf, vlf), (qcb, kcb, vcb, qlb, klb, vlb))
    qkv = []
    for qc, kc, vc, ql, kl, vl in dirs:
        qkv.append((jnp.where(is_ctx, qc[0], ql[0]), jnp.where(is_ctx, kc[0], kl[0]) * kscale,
                    jnp.where(is_ctx, vc[0], vl[0])))

    def products(d, h):
        q, k, v = qkv[d]
        qh = q[:, h * RET_KEY:(h + 1) * RET_KEY]
        kh = k[:, h * RET_KEY:(h + 1) * RET_KEY]
        vh = v[:, h * RET_VAL:(h + 1) * RET_VAL]
        state = s_scr[d, h]
        kdec = (kh.astype(F32) * kd_ref[d, h]).astype(BF16)
        return _dot_nt(qh, kh), _dot(qh, state.astype(BF16)), _dot_tn(kdec, vh), state, vh

    def finish(d, h, prod):
        qk, q_state, k_v, state, vh = prod
        s_scr[d, h] = state * cd_ref[d, h] + k_v
        return _dot((qk * dtab_ref[d, h]).astype(BF16), vh) + q_state * qd_ref[d, h]

    units = [(d, h) for d in range(2) for h in range(RET_HEADS)]
    outs = []
    nxt = products(*units[0])
    for u, (d, h) in enumerate(units):
        cur = nxt
        if u + 1 < len(units):
            nxt = products(*units[u + 1])
        outs.append(finish(d, h, cur))
    o_f = jnp.concatenate(outs[:RET_HEADS], axis=1).astype(ocf.dtype)
    o_b = jnp.concatenate(outs[RET_HEADS:], axis=1).astype(ocb.dtype)

    @pl.when(is_ctx)
    def _():
        ocf[0] = o_f
        ocb[0] = o_b

    @pl.when(jnp.logical_not(is_ctx))
    def _():
        olf[0] = o_f
        olb[0] = o_b


def _retention(ctx_arr, lat_arr, log_g):
    b, lc, _ = ctx_arr.shape
    l = lat_arr.shape[1]
    c = RET_CHUNK
    nc, nl = lc // c, l // c
    qw = RET_HEADS * RET_KEY
    vw = RET_HEADS * RET_VAL
    idx = jnp.arange(c, dtype=F32)
    rel = idx[:, None] - idx[None, :]
    lg = log_g[:, :, None, None]
    keep_f = (rel >= 0)[None]
    keep_b = (rel < 0)[None]
    d_f = jnp.where(keep_f, jnp.exp(jnp.where(keep_f, rel[None] * lg[0], 0.0)), 0.0)
    d_b = jnp.where(keep_b, jnp.exp(jnp.where(keep_b, -rel[None] * lg[1], 0.0)), 0.0)
    dtab = jnp.stack([d_f, d_b])
    col = idx[None, :, None]
    qd = jnp.stack([jnp.exp((col + 1.0) * lg[0]), jnp.exp((c - col) * lg[1])])
    kd = jnp.stack([jnp.exp((c - 1.0 - col) * lg[0]), jnp.exp(col * lg[1])])
    cd = jnp.exp(c * log_g)

    f_ctx = lambda bi, j: (bi, jnp.minimum(j, nc - 1))
    f_lat = lambda bi, j: (bi, jnp.maximum(j - nc, 0))
    b_ctx = lambda bi, j: (bi, jnp.maximum(nc - 1 - j, 0))
    b_lat = lambda bi, j: (bi, jnp.minimum(nc + nl - 1 - j, nl - 1))

    def qkv_specs(row_map):
        return [pl.BlockSpec((1, c, qw), lambda bi, j: row_map(bi, j) + (0,)),
                pl.BlockSpec((1, c, qw), lambda bi, j: row_map(bi, j) + (1,)),
                pl.BlockSpec((1, c, vw), lambda bi, j: row_map(bi, j) + (1,))]

    full = lambda shape: pl.BlockSpec(shape, lambda bi, j: (0,) * len(shape))
    in_specs = ([pl.BlockSpec(memory_space=pltpu.SMEM)]
                + qkv_specs(f_ctx) + qkv_specs(f_lat) + qkv_specs(b_ctx) + qkv_specs(b_lat)
                + [full(dtab.shape), full(qd.shape), full(kd.shape)])
    out_spec = lambda row_map: pl.BlockSpec((1, c, vw), lambda bi, j: row_map(bi, j) + (0,))
    out_sds = lambda n: jax.ShapeDtypeStruct((b, n, vw), BF16)
    return pl.pallas_call(
        functools.partial(_ret_kernel, n_ctx_chunks=nc),
        grid=(b, nc + nl),
        in_specs=in_specs,
        out_specs=[out_spec(f_ctx), out_spec(f_lat), out_spec(b_ctx), out_spec(b_lat)],
        out_shape=[out_sds(lc), out_sds(l), out_sds(lc), out_sds(l)],
        scratch_shapes=[pltpu.VMEM((2, RET_HEADS, RET_KEY, RET_VAL), F32)],
        compiler_params=_cparams(("parallel", "arbitrary")),
        name="retention_scan",
    )(cd, ctx_arr, ctx_arr, ctx_arr, lat_arr, lat_arr, lat_arr, ctx_arr, ctx_arr, ctx_arr, lat_arr, lat_arr,
      lat_arr, dtab, qd, kd)


def _router_kernel(x_ref, g_ref, sh_ref, sc_ref, wr_ref, *rest):
    a_ref, lg_ref = rest[-2:]
    a = _modulated(x_ref, g_ref, sh_ref, sc_ref)
    a_hi = a.astype(BF16)
    a_ref[...] = a_hi
    a_lo = (a - a_hi.astype(F32)).astype(BF16)
    w = wr_ref[...]
    w_hi = w.astype(BF16)
    w_lo = (w - w_hi.astype(F32)).astype(BF16)
    logits = _dot(a_hi, w_hi) + _dot(a_lo, w_hi) + _dot(a_hi, w_lo)
    lane = lax.broadcasted_iota(jnp.int32, logits.shape, 1)
    neg = -jnp.inf
    lg = jnp.where(lane < N_EXPERTS, logits, neg)
    v0 = jnp.max(lg, axis=-1, keepdims=True)
    i0 = jnp.min(jnp.where(lg == v0, lane, ROUTER_PAD), axis=-1, keepdims=True)
    lg = jnp.where(lane == i0, neg, lg)
    v1 = jnp.max(lg, axis=-1, keepdims=True)
    i1 = jnp.min(jnp.where(lg == v1, lane, ROUTER_PAD), axis=-1, keepdims=True)
    e1 = jnp.exp(v1 - v0)
    g0 = 1.0 / (1.0 + e1)
    route = jnp.where(lane == 0, g0, jnp.where(lane == 1, e1 * g0, jnp.where(
        lane == 2, i0.astype(F32), jnp.where(lane == 3, i1.astype(F32), 0.0))))
    lg_ref[...] = route


def _router(x, g, mod, sh_idx, rows_per_mod, w_router_pad, *, tm, name, pool_rows, row_offset=0, prev=None):
    t, d = x.shape
    assert row_offset % tm == 0
    first = row_offset // tm
    if rows_per_mod is None:
        mod_row = lambda i: 0
    else:
        tiles_per_mod = rows_per_mod // tm
        mod_row = lambda i: i // tiles_per_mod
    in_specs = [pl.BlockSpec((tm, d), lambda i: (i, 0)),
                pl.BlockSpec((1, d), lambda i: (0, 0)),
                pl.BlockSpec((1, 1, d), lambda i: (mod_row(i), 0, sh_idx)),
                pl.BlockSpec((1, 1, d), lambda i: (mod_row(i), 0, sh_idx + 1)),
                pl.BlockSpec((d, ROUTER_PAD), lambda i: (0, 0))]
    args = [x, g.reshape(1, d), mod, mod, w_router_pad]
    aliases = {}
    if prev is not None:
        in_specs += [pl.BlockSpec(memory_space=pl.ANY)] * 2
        aliases = {len(args): 0, len(args) + 1: 1}
        args += list(prev)
    return pl.pallas_call(
        _router_kernel,
        grid=(t // tm,),
        in_specs=in_specs,
        out_specs=[pl.BlockSpec((tm, d), lambda i: (first + i, 0)),
                   pl.BlockSpec((tm, ROUTER_PAD), lambda i: (first + i, 0))],
        out_shape=[jax.ShapeDtypeStruct((pool_rows, d), BF16), jax.ShapeDtypeStruct((pool_rows, ROUTER_PAD), F32)],
        input_output_aliases=aliases,
        compiler_params=_cparams(("parallel",)),
        name=name,
    )(*args)


def _experts_kernel(be_ref, nb_ref, x_ref, wg_ref, wu_ref, wd_ref, *rest, nf, first_block):
    o_ref, acc = rest[-2:]
    blk = pl.program_id(0)
    f = pl.program_id(1)
    used = first_block + blk < nb_ref[0]
    rc = min(x_ref.shape[0], MOE_ROW_CHUNK)

    @pl.when((blk == 0) & (f == 0))
    def _():
        acc[...] = jnp.zeros_like(acc)

    @pl.when(used)
    def _():
        wg = wg_ref[0, 0].astype(BF16)
        wu = wu_ref[0, 0].astype(BF16)
        wd = wd_ref[0, 0].astype(BF16)
        n_chunks = x_ref.shape[0] // rc

        def gate_up(c):
            x = x_ref[c * rc:(c + 1) * rc, :]
            return _dot(x, wg), _dot(x, wu)

        nxt = gate_up(0)
        for c in range(n_chunks):
            g, u = nxt
            if c + 1 < n_chunks:
                nxt = gate_up(c + 1)
            part = _dot((g * _sigmoid(g) * u).astype(BF16), wd)
            rows = slice(c * rc, (c + 1) * rc)
            acc[rows, :] = part + jnp.where(f > 0, acc[rows, :], 0.0)

    @pl.when(f == nf - 1)
    def _():
        o_ref[...] = jnp.where(used, acc[...], 0.0).astype(o_ref.dtype)


def _experts(buf, block_expert, n_used, w_gu, w_down, layer, *, first_block, total_blocks, prev=None):
    rows, d = buf.shape
    nb = rows // MOE_ROWS
    two_f = w_gu.shape[3]
    fdim = two_f // 2
    fc = _pick(fdim, (MOE_FC, 256, 128))
    nf = fdim // fc
    in_specs = [pl.BlockSpec((MOE_ROWS, d), lambda i, f, be, nu: (i, 0)),
                pl.BlockSpec((1, 1, d, fc), lambda i, f, be, nu: (layer, be[first_block + i], 0, f)),
                pl.BlockSpec((1, 1, d, fc), lambda i, f, be, nu: (layer, be[first_block + i], 0, nf + f)),
                pl.BlockSpec((1, 1, fc, d), lambda i, f, be, nu: (layer, be[first_block + i], f, 0))]
    args = [block_expert, n_used, buf, w_gu, w_gu, w_down]
    aliases = {}
    if prev is not None:
        in_specs.append(pl.BlockSpec(memory_space=pl.ANY))
        aliases = {len(args): 0}
        args.append(prev)
    grid_spec = pltpu.PrefetchScalarGridSpec(
        num_scalar_prefetch=2,
        grid=(nb, nf),
        in_specs=in_specs,
        out_specs=pl.BlockSpec((MOE_ROWS, d), lambda i, f, be, nu: (first_block + i, 0)),
        scratch_shapes=[pltpu.VMEM((MOE_ROWS, d), F32)],
    )
    return pl.pallas_call(
        functools.partial(_experts_kernel, nf=nf, first_block=first_block),
        grid_spec=grid_spec,
        out_shape=jax.ShapeDtypeStruct((total_blocks * MOE_ROWS, d), BF16),
        input_output_aliases=aliases,
        compiler_params=_cparams(("parallel", "arbitrary")),
        name="moe_experts",
    )(*args)


def _moe(a_all, top_idx, w_gu, w_down, layer):
    t, d = a_all.shape
    e = w_gu.shape[1]
    onehot = jnp.sum((top_idx[..., None] == jnp.arange(e)[None, None, :]).astype(jnp.int32), axis=1)
    counts = jnp.sum(onehot, axis=0)
    rank = jnp.cumsum(onehot, axis=0) - onehot
    padded = (counts + MOE_ROWS - 1) // MOE_ROWS * MOE_ROWS
    pend = jnp.cumsum(padded)
    pstart = pend - padded
    dest = pstart[top_idx] + jnp.take_along_axis(rank, top_idx, axis=1)
    nb = -(-(t * TOP_K) // MOE_ROWS) + e
    tok = jnp.repeat(jnp.arange(t, dtype=jnp.int32), TOP_K)
    row_src = jnp.zeros((nb * MOE_ROWS,), jnp.int32).at[dest.reshape(-1)].set(tok, unique_indices=True)
    block_expert = jnp.minimum(jnp.searchsorted(pend, jnp.arange(nb) * MOE_ROWS, side="right"), e - 1).astype(jnp.int32)
    n_used = (pend[-1] // MOE_ROWS).astype(jnp.int32).reshape(1)
    n_groups = _pick(nb, (MOE_GROUPS, 2, 1))
    per = nb // n_groups
    yb = None
    for k in range(n_groups):
        buf = a_all[row_src[k * per * MOE_ROWS:(k + 1) * per * MOE_ROWS]]
        yb = _experts(buf, block_expert, n_used, w_gu, w_down, layer, first_block=k * per, total_blocks=nb, prev=yb)
    return yb, dest


def _rope_angles(pos, dim):
    inv = ROPE_BASE ** (-jnp.arange(0, dim, 2, dtype=F32) / dim)
    ang = pos.astype(F32)[:, None] * inv[None, :]
    return jnp.concatenate([ang, ang], axis=-1)


def _signed_sin(sin, shift):
    low = (jnp.arange(sin.shape[1]) % (2 * shift)) < shift
    return jnp.where(low[None, :], -sin, 0.0), jnp.where(low[None, :], 0.0, sin)


def _axial_tables(n_tokens, head_dim):
    rows = n_tokens // GRID_W
    row = jnp.repeat(jnp.arange(rows, dtype=jnp.int32), GRID_W)
    col = jnp.tile(jnp.arange(GRID_W, dtype=jnp.int32), rows)
    half = head_dim // 2
    ang = jnp.concatenate([_rope_angles(row, half), _rope_angles(col, half)], axis=-1)
    ang = jnp.tile(ang, (1, LANES // head_dim))
    sa, sb = _signed_sin(jnp.sin(ang), half // 2)
    return jnp.cos(ang), sa, sb, half // 2


def _ret_tables(n_tokens):
    ang = _rope_angles(jnp.arange(n_tokens), RET_KEY)
    sa, sb = _signed_sin(jnp.sin(ang), RET_KEY // 2)
    return jnp.cos(ang), sa, sb, RET_KEY // 2


def kernel(x, c, ctx, c_ctx, mod_w, mod_b, norm_g, da_w_in, da_w_out, da_lambda, da_subln_g, wa_w_in, wa_w_out,
           wa_sink, ret_w_in, ret_w_out, ret_decay_logit, ffn_w_gu, ffn_w_down, moe_router, moe_w_gu, moe_w_down):
    b, l, d = x.shape
    lc = ctx.shape[1]
    depth = mod_w.shape[0]
    t, tc = b * l, b * lc
    tm = _pick(l, (1024, 512, 256, 128))
    tmc = _pick(tc, (1024, 512, 256, 128))

    n_cond = -(-(b + 1) // 8) * 8
    cs = jnp.zeros((n_cond, d), F32).at[:b].set(c).at[b].set(c_ctx)
    mod_all = _mod_vectors(cs, mod_w, mod_b)

    xs = x.reshape(t, d)
    hs = ctx.reshape(tc, d)
    mixer_count = [0] * N_MIXERS
    i_dense = 0
    i_moe = 0
    for layer in range(depth):
        last = layer == depth - 1
        mod = mod_all[layer, :b].reshape(b, 1, 6 * d)
        mod_c = mod_all[layer, b:b + 1].reshape(1, 1, 6 * d)
        ng = norm_g[layer]
        kind = layer % N_MIXERS
        jm = mixer_count[kind]
        mixer_count[kind] += 1

        if kind == 0:
            lam_init = 0.8 - 0.6 * math.exp(-0.3 * layer)
            w_in = da_w_in[jm].astype(BF16)
            cos, sa, sb, shift = _axial_tables(l, DA_HEAD)
            qkv = _norm_proj(xs, ng[0], mod, 0, l, w_in, tm=tm, tn=3 * d, rot=(cos, sa, sb, 2 * d, shift, l),
                             name="da_in_proj").reshape(b, l, 3 * d)
            qkv_c = _norm_proj(hs, ng[0], mod_c, 0, None, w_in, tm=tmc, tn=3 * d, name="da_in_proj_ctx").reshape(b, lc, 3 * d)
            lp = da_lambda[jm].astype(F32)
            lam = jnp.exp(jnp.sum(lp[0] * lp[1])) - jnp.exp(jnp.sum(lp[2] * lp[3])) + lam_init
            o = _diff_attention(qkv, [qkv_c, qkv], lam, da_subln_g[jm], lam_init, tq=_pick(l, (DA_TQ, 128)), name="diff_attn")
            w_out = da_w_out[jm].astype(BF16)
            xs = _resid(xs, ng[1], mod, 2, l, tm=tm, a=o.reshape(t, d), w=w_out, name="da_out_proj")
            if not last:
                oc = _diff_attention(qkv_c, [qkv_c], lam, da_subln_g[jm], lam_init, tq=_pick(lc, (DA_TQ, 128)), name="diff_attn_ctx")
                hs = _resid(hs, ng[1], mod_c, 2, None, tm=tmc, a=oc.reshape(tc, d), w=w_out, name="da_out_proj_ctx")
        elif kind == 1:
            w_in = wa_w_in[jm].astype(BF16)
            n_qkv = w_in.shape[1]
            cos, sa, sb, shift = _axial_tables(l, WA_HEAD)
            n_rot = (WA_Q_HEADS + WA_KV_HEADS) * WA_HEAD
            qkv = _norm_proj(xs, ng[0], mod, 0, l, w_in, tm=tm, tn=n_qkv, rot=(cos, sa, sb, n_rot, shift, l),
                             name="wa_in_proj").reshape(b, l, n_qkv)
            qkv_c = _norm_proj(hs, ng[0], mod_c, 0, None, w_in, tm=tmc, tn=n_qkv, name="wa_in_proj_ctx").reshape(b, lc, n_qkv)
            o = _window_attention(qkv, qkv_c, wa_sink[jm], windowed=True, name="window_attn")
            w_out = wa_w_out[jm].astype(BF16)
            xs = _resid(xs, ng[1], mod, 2, l, tm=tm, a=o.reshape(t, d), w=w_out, name="wa_out_proj")
            if not last:
                oc = _window_attention(qkv_c, qkv_c, wa_sink[jm], windowed=False, name="window_attn_ctx")
                hs = _resid(hs, ng[1], mod_c, 2, None, tm=tmc, a=oc.reshape(tc, d), w=w_out, name="wa_out_proj_ctx")
        else:
            w_in = ret_w_in[jm].astype(BF16)
            n_qkv = w_in.shape[1]
            cos, sa, sb, shift = _ret_tables(l)
            n_rot = 2 * RET_HEADS * RET_KEY
            tr = _pick(l, (512, 256, 128))
            qkv = _norm_proj(xs, ng[0], mod, 0, l, w_in, tm=tr, tn=n_qkv // 2, rot=(cos, sa, sb, n_rot, shift, l),
                             name="ret_in_proj")
            qkv_c = _norm_proj(hs, ng[0], mod_c, 0, None, w_in, tm=_pick(tc, (512, 256, 128)), tn=n_qkv // 2,
                               name="ret_in_proj_ctx")
            log_g = jax.nn.log_sigmoid(ret_decay_logit[jm].astype(F32))
            ocf, olf, ocb, olb = _retention(qkv_c.reshape(b, lc, n_qkv), qkv.reshape(b, l, n_qkv), log_g)
            w_out = ret_w_out[jm].astype(BF16)
            vw = RET_HEADS * RET_VAL
            xs = _resid(xs, ng[1], mod, 2, l, tm=_pick(l, (512, 256, 128)), w=w_out,
                        ret=(olf.reshape(t, vw), olb.reshape(t, vw), qkv), name="ret_out_proj")
            if not last:
                hs = _resid(hs, ng[1], mod_c, 2, None, tm=_pick(tc, (512, 256, 128)), w=w_out,
                            ret=(ocf.reshape(tc, vw), ocb.reshape(tc, vw), qkv_c), name="ret_out_proj_ctx")

        if layer % 2 == 0:
            w_gu = ffn_w_gu[i_dense].astype(BF16)
            w_dn = ffn_w_down[i_dense].astype(BF16)
            i_dense += 1
            f = w_dn.shape[0]
            w_g, w_u = w_gu[:, :f], w_gu[:, f:]
            tn = f
            tf = _pick(l, (512, 256, 128))
            act = _norm_proj(xs, ng[2], mod, 3, l, (w_g, w_u), tm=tf, tn=tn, name="ffn_up")
            xs = _resid(xs, ng[3], mod, 5, l, tm=tf, a=act, w=w_dn, name="ffn_down")
            if not last:
                tfc = _pick(tc, (512, 256, 128))
                act_c = _norm_proj(hs, ng[2], mod_c, 3, None, (w_g, w_u), tm=tfc, tn=tn, name="ffn_up_ctx")
                hs = _resid(hs, ng[3], mod_c, 5, None, tm=tfc, a=act_c, w=w_dn, name="ffn_down_ctx")
        else:
            w_r = jnp.zeros((d, ROUTER_PAD), F32).at[:, :N_EXPERTS].set(moe_router[i_moe])
            moe_layer = i_moe
            i_moe += 1
            pool = t if last else t + tc
            a_all, route = _router(xs, ng[2], mod, 3, l, w_r, tm=tm, name="moe_router", pool_rows=pool)
            if not last:
                a_all, route = _router(hs, ng[2], mod_c, 3, None, w_r, tm=tmc, name="moe_router_ctx", pool_rows=pool,
                                       row_offset=t, prev=(a_all, route))
            route_x, route_c = route[:t], route[t:]
            yb, dest = _moe(a_all, route[:, 2:4].astype(jnp.int32), moe_w_gu, moe_w_down, moe_layer)
            xs = _resid(xs, ng[3], mod, 5, l, tm=tm, mix=(yb[dest[:t, 0]], yb[dest[:t, 1]], route_x), name="moe_combine")
            if not last:
                hs = _resid(hs, ng[3], mod_c, 5, None, tm=tmc, mix=(yb[dest[t:, 0]], yb[dest[t:, 1]], route_c),
                            name="moe_combine_ctx")
    return xs.reshape(b, l, d)
```

```python
import functools
import math

import jax
import jax.numpy as jnp
import numpy as np
from jax import lax
from jax.experimental import pallas as pl
from jax.experimental.pallas import tpu as pltpu

F32 = jnp.float32
BF16 = jnp.bfloat16

D_MODEL = 1024
N_MIXERS = 3
NORM_EPS = 1e-6
ROPE_BASE = 10000.0
GRID_W = 64
DA_HEAD = 64
DA_HEADS = D_MODEL // (2 * DA_HEAD)
WA_HEAD = 64
WA_Q_HEADS = D_MODEL // WA_HEAD
WA_KV_HEADS = WA_Q_HEADS // 4
WINDOW = 128
RET_KEY = 256
RET_HEADS = D_MODEL // RET_KEY
RET_VAL = 2 * RET_KEY
RET_CHUNK = 128
N_EXPERTS = 8
TOP_K = 2

LANES = 128
V7X_VMEM_LIMIT = 56 * 1024 * 1024
ROUTER_PAD = LANES
MOE_ROWS = 1024
MOE_FC = 512
MOE_ROW_CHUNK = 256
MOE_GROUPS = 4
PROJ_COLS = 512


def _cparams(sem):
    return pltpu.CompilerParams(dimension_semantics=sem, vmem_limit_bytes=V7X_VMEM_LIMIT)


def _pick(n, prefs):
    for p in prefs:
        if n % p == 0:
            return p
    return n


def _sigmoid(x):
    return 1.0 / (1.0 + jnp.exp(-x))


def _rms(y):
    return y * lax.rsqrt(jnp.mean(y * y, axis=-1, keepdims=True) + NORM_EPS)


def _dot(a, b):
    return jnp.dot(a, b, preferred_element_type=F32)


def _dot_nt(a, b):
    return lax.dot_general(a, b, (((1,), (1,)), ((), ())), preferred_element_type=F32)


def _dot_tn(a, b):
    return lax.dot_general(a, b, (((0,), (0,)), ((), ())), preferred_element_type=F32)


def _mod_kernel(c_ref, w_ref, b_ref, o_ref):
    cs = c_ref[...]
    s = cs * _sigmoid(cs)
    w = w_ref[0]
    s_hi = s.astype(BF16)
    s_lo = (s - s_hi.astype(F32)).astype(BF16)
    w_hi = w.astype(BF16)
    w_lo = (w - w_hi.astype(F32)).astype(BF16)
    o_ref[0] = _dot(s_hi, w_hi) + _dot(s_lo, w_hi) + _dot(s_hi, w_lo) + b_ref[0]


def _mod_vectors(cs, mod_w, mod_b):
    depth, d, n = mod_w.shape
    r = cs.shape[0]
    tn = _pick(n, (1024, 512, 256, 128))
    return pl.pallas_call(
        _mod_kernel,
        grid=(depth, n // tn),
        in_specs=[
            pl.BlockSpec((r, d), lambda l, j: (0, 0)),
            pl.BlockSpec((1, d, tn), lambda l, j: (l, 0, j)),
            pl.BlockSpec((1, 1, tn), lambda l, j: (l, 0, j)),
        ],
        out_specs=pl.BlockSpec((1, r, tn), lambda l, j: (l, 0, j)),
        out_shape=jax.ShapeDtypeStruct((depth, r, n), F32),
        compiler_params=_cparams(("parallel", "parallel")),
        name="mod_vectors",
    )(cs, mod_w, mod_b.reshape(depth, 1, n))


def _modulated(x_ref, g_ref, sh_ref, sc_ref):
    x = x_ref[...]
    return _rms(x) * g_ref[...] * (1.0 + sc_ref[0]) + sh_ref[0]


def _rotate(acc, cos, sa, sb, shift, cw):
    outs = []
    for c in range(acc.shape[1] // cw):
        y = acc[:, c * cw:(c + 1) * cw]
        up = pltpu.roll(y, cw - shift, 1)
        dn = pltpu.roll(y, shift, 1)
        outs.append(y * cos + up * sa + dn * sb)
    return outs[0] if len(outs) == 1 else jnp.concatenate(outs, axis=1)


def _col_chunks(width, cuts):
    edges = sorted({0, width} | set(range(PROJ_COLS, width, PROJ_COLS)) | {c for c in cuts if 0 < c < width})
    return list(zip(edges[:-1], edges[1:]))


def _proj_kernel(*refs, mode, n_rot_cols, shift, cw, n_col_tiles):
    if mode == "swiglu":
        x_ref, g_ref, sh_ref, sc_ref, wg_ref, wu_ref, o_ref, a_scr = refs
    elif mode == "rot":
        x_ref, g_ref, sh_ref, sc_ref, w_ref, cos_ref, sa_ref, sb_ref, o_ref, a_scr = refs
    else:
        x_ref, g_ref, sh_ref, sc_ref, w_ref, o_ref, a_scr = refs
    j = pl.program_id(1)
    tn = o_ref.shape[1]

    @pl.when(j == 0)
    def _():
        a_scr[...] = _modulated(x_ref, g_ref, sh_ref, sc_ref).astype(BF16)

    def tile(jt):
        a = a_scr[...]
        for lo, hi in _col_chunks(tn, [n_rot_cols - jt * tn]):
            if mode == "swiglu":
                g = _dot(a, wg_ref[:, lo:hi])
                res = g * _sigmoid(g) * _dot(a, wu_ref[:, lo:hi])
            else:
                res = _dot(a, w_ref[:, lo:hi])
                if mode == "rot" and jt * tn + hi <= n_rot_cols:
                    res = _rotate(res, cos_ref[...], sa_ref[...], sb_ref[...], shift, cw)
            o_ref[:, lo:hi] = res.astype(o_ref.dtype)

    if mode != "rot":
        tile(0)
    else:
        n_rot_tiles = -(-n_rot_cols // tn)
        for jt in range(min(n_rot_tiles + 1, n_col_tiles)):
            @pl.when((j >= jt) if jt == n_rot_tiles else (j == jt))
            def _(jt=jt):
                tile(jt)


def _norm_proj(x, g, mod, sh_idx, rows_per_mod, weights, *, tm, tn, rot=None, name):
    t, d = x.shape
    swiglu = isinstance(weights, tuple)
    n = weights[0].shape[1] if swiglu else weights.shape[1]
    assert t % tm == 0 and n % tn == 0
    if rows_per_mod is None:
        mod_row = lambda i: 0
    else:
        assert rows_per_mod % tm == 0
        tiles_per_mod = rows_per_mod // tm
        mod_row = lambda i: i // tiles_per_mod
    in_specs = [
        pl.BlockSpec((tm, d), lambda i, j: (i, 0)),
        pl.BlockSpec((1, d), lambda i, j: (0, 0)),
        pl.BlockSpec((1, 1, d), lambda i, j: (mod_row(i), 0, sh_idx)),
        pl.BlockSpec((1, 1, d), lambda i, j: (mod_row(i), 0, sh_idx + 1)),
    ]
    args = [x, g.reshape(1, d), mod, mod]
    w_spec = pl.BlockSpec((d, tn), lambda i, j: (0, j))
    n_rot_cols, shift, cw = 0, 0, 0
    if swiglu:
        mode = "swiglu"
        in_specs += [w_spec, w_spec]
        args += list(weights)
    elif rot is not None:
        mode = "rot"
        cos, sa, sb, n_rot_cols, shift, seq_len = rot
        cw = cos.shape[1]
        assert n_rot_cols % cw == 0 and PROJ_COLS % cw == 0 and seq_len % tm == 0
        tiles_per_seq = seq_len // tm
        t_spec = pl.BlockSpec((tm, cw), lambda i, j: (i % tiles_per_seq, 0))
        in_specs += [w_spec, t_spec, t_spec, t_spec]
        args += [weights, cos, sa, sb]
    else:
        mode = "plain"
        in_specs += [w_spec]
        args += [weights]
    return pl.pallas_call(
        functools.partial(_proj_kernel, mode=mode, n_rot_cols=n_rot_cols, shift=shift, cw=cw, n_col_tiles=n // tn),
        grid=(t // tm, n // tn),
        in_specs=in_specs,
        out_specs=pl.BlockSpec((tm, tn), lambda i, j: (i, j)),
        out_shape=jax.ShapeDtypeStruct((t, n), BF16),
        scratch_shapes=[pltpu.VMEM((tm, d), BF16)],
        compiler_params=_cparams(("parallel", "arbitrary")),
        name=name,
    )(*args)


def _resid_kernel(*refs, mode):
    if mode == "ret":
        of_ref, ob_ref, gt_ref, w_ref, x_ref, gate_ref, g_ref, o_ref = refs
        o = of_ref[...].astype(F32) + ob_ref[...].astype(F32)
        parts = []
        for h in range(RET_HEADS):
            parts.append(_rms(o[:, h * RET_VAL:(h + 1) * RET_VAL]))
        o = jnp.concatenate(parts, axis=1)
        gt = gt_ref[...].astype(F32)
        y = _dot((gt * _sigmoid(gt) * o).astype(BF16), w_ref[...])
    elif mode == "matmul":
        a_ref, w_ref, x_ref, gate_ref, g_ref, o_ref = refs
        y = _dot(a_ref[...], w_ref[...])
    else:
        y0_ref, y1_ref, route_ref, x_ref, gate_ref, g_ref, o_ref = refs
        route = route_ref[...]
        y = y0_ref[...].astype(F32) * route[:, 0:1] + y1_ref[...].astype(F32) * route[:, 1:2]
    o_ref[...] = x_ref[...] + gate_ref[0] * (_rms(y) * g_ref[...])


def _resid(x, g, mod, gate_idx, rows_per_mod, *, tm, name, a=None, w=None, mix=None, ret=None):
    t, d = x.shape
    assert t % tm == 0
    if rows_per_mod is None:
        mod_row = lambda i: 0
    else:
        assert rows_per_mod % tm == 0
        tiles_per_mod = rows_per_mod // tm
        mod_row = lambda i: i // tiles_per_mod
    row = lambda width: pl.BlockSpec((tm, width), lambda i: (i, 0))
    tail_specs = [row(d), pl.BlockSpec((1, 1, d), lambda i: (mod_row(i), 0, gate_idx)),
                  pl.BlockSpec((1, d), lambda i: (0, 0))]
    tail_args = [x, mod, g.reshape(1, d)]
    if ret is not None:
        mode = "ret"
        o_f, o_b, qkvg = ret
        vw = RET_HEADS * RET_VAL
        gate_col = qkvg.shape[1] // vw - 1
        in_specs = [row(vw), row(vw), pl.BlockSpec((tm, vw), lambda i: (i, gate_col)),
                    pl.BlockSpec(w.shape, lambda i: (0, 0))]
        args = [o_f, o_b, qkvg, w]
    elif a is not None:
        mode = "matmul"
        in_specs = [row(a.shape[1]), pl.BlockSpec(w.shape, lambda i: (0, 0))]
        args = [a, w]
    else:
        mode = "mix"
        in_specs = [row(d), row(d), row(ROUTER_PAD)]
        args = list(mix)
    return pl.pallas_call(
        functools.partial(_resid_kernel, mode=mode),
        grid=(t // tm,),
        in_specs=in_specs + tail_specs,
        out_specs=row(d),
        out_shape=jax.ShapeDtypeStruct((t, d), F32),
        compiler_params=_cparams(("parallel",)),
        name=name,
    )(*args, *tail_args)


DA_SUM_ROWS = 16
DA_TQ = 256
DA_EXP_ROWS = 64


def _da_kernel(*refs, n_seg, post_scale, n_tiles, nq):
    lam_ref, q_ref = refs[0], refs[1]
    k_refs = refs[2:2 + n_seg]
    v_refs = refs[2 + n_seg:2 + 2 * n_seg]
    sg_ref, o_ref, k_scr, vt_scr, s_a, s_b, p_a, p_b, mx_a, mx_b = refs[2 + 2 * n_seg:]
    hw = 2 * DA_HEAD
    n_keys = vt_scr.shape[1]
    tq = q_ref.shape[1]
    g = pl.program_id(0)
    t_qk = jnp.minimum(g, n_tiles - 1)
    t_pv = jnp.clip(g - 2, 0, n_tiles - 1)

    @pl.when(g == 0)
    def _():
        for buf in (s_a, s_b, p_a, p_b, mx_a, mx_b):
            buf[...] = jnp.zeros_like(buf)

    @pl.when(lax.rem(t_qk, nq) == 0)
    def _():
        off = 0
        for kr in k_refs:
            n = kr.shape[1]
            k_scr[off:off + n, :] = kr[0]
            off += n

    @pl.when(lax.rem(t_pv, nq) == 0)
    def _():
        off = 0
        for vr in v_refs:
            n = vr.shape[1]
            vt_scr[0:hw, off:off + n] = vr[0].astype(F32).T.astype(BF16)
            off += n
        row = lax.broadcasted_iota(jnp.int32, (DA_SUM_ROWS, n_keys), 0)
        vt_scr[hw:hw + DA_SUM_ROWS, :] = jnp.where(row == 0, 1.0, 0.0).astype(BF16)

    lane = lax.broadcasted_iota(jnp.int32, (1, hw), 1)
    qscale = DA_HEAD ** -0.5 * math.log2(math.e)

    def stages(s_new, mx_new, s_old, mx_old, p_old, p_older):
        accs = []
        for m in range(2):
            accs.append(_dot(vt_scr[...], p_older[m]))
        q = q_ref[0].astype(F32)
        for m in range(2):
            sel = (lane < DA_HEAD) if m == 0 else (lane >= DA_HEAD)
            qm = jnp.where(sel, q * qscale, 0.0).astype(BF16)
            s_t = _dot_nt(k_scr[...], qm)
            s_new[m] = s_t
            part = [None] * 4
            for r in range(n_keys // 8):
                tile = s_t[r * 8:(r + 1) * 8, :]
                j = r % len(part)
                part[j] = tile if part[j] is None else jnp.maximum(part[j], tile)
            mx_new[m] = functools.reduce(jnp.maximum, part)
            mx8 = jnp.broadcast_to(jnp.max(mx_old[m], axis=0, keepdims=True), (8, tq))
            for c in range(n_keys // DA_EXP_ROWS):
                rows = slice(c * DA_EXP_ROWS, (c + 1) * DA_EXP_ROWS)
                zero = jnp.minimum(jnp.abs(s_t[c * DA_EXP_ROWS:c * DA_EXP_ROWS + 8, :]), 0.0)
                d = s_old[m, rows, :].reshape(DA_EXP_ROWS // 8, 8, tq) - (mx8 + zero)[None]
                p_old[m, rows, :] = jnp.exp2(d.reshape(DA_EXP_ROWS, tq).astype(BF16))
        a0, a1 = accs
        o_t = a0[:hw] * (1.0 / a0[hw:hw + 1]) - a1[:hw] * (lam_ref[0, 0] / a1[hw:hw + 1])
        o_t = o_t * lax.rsqrt(jnp.mean(o_t * o_t, axis=0, keepdims=True) + NORM_EPS) * (sg_ref[...] * post_scale)
        o_ref[0] = o_t.T.astype(o_ref.dtype)

    @pl.when(lax.rem(g, 2) == 0)
    def _():
        stages(s_a, mx_a, s_b, mx_b, p_b, p_a)

    @pl.when(lax.rem(g, 2) == 1)
    def _():
        stages(s_b, mx_b, s_a, mx_a, p_a, p_b)


def _diff_attention(q_arr, kv_arrs, lam, subln_g, lam_init, *, tq, name):
    b, lq, _ = q_arr.shape
    hw = 2 * DA_HEAD
    n_seg = len(kv_arrs)
    n_keys = sum(kv.shape[1] for kv in kv_arrs)
    assert lq % tq == 0 and n_keys % DA_EXP_ROWS == 0
    nq = lq // tq
    n_tiles = b * DA_HEADS * nq

    def tile(t):
        return t // (nq * DA_HEADS), lax.rem(t // nq, DA_HEADS), lax.rem(t, nq)

    def qk_tile(g):
        return tile(jnp.minimum(g, n_tiles - 1))

    def pv_tile(g):
        return tile(jnp.clip(g - 2, 0, n_tiles - 1))

    in_specs = [pl.BlockSpec(memory_space=pltpu.SMEM),
                pl.BlockSpec((1, tq, hw), lambda g: (qk_tile(g)[0], qk_tile(g)[2], qk_tile(g)[1]))]
    in_specs += [pl.BlockSpec((1, kv.shape[1], hw), lambda g: (qk_tile(g)[0], 0, DA_HEADS + qk_tile(g)[1]))
                 for kv in kv_arrs]
    in_specs += [pl.BlockSpec((1, kv.shape[1], hw), lambda g: (pv_tile(g)[0], 0, 2 * DA_HEADS + pv_tile(g)[1]))
                 for kv in kv_arrs]
    in_specs += [pl.BlockSpec((hw, 1), lambda g: (0, 0))]
    s_buf = pltpu.VMEM((2, n_keys, tq), F32)
    p_buf = pltpu.VMEM((2, n_keys, tq), BF16)
    mx_buf = pltpu.VMEM((2, 8, tq), F32)
    return pl.pallas_call(
        functools.partial(_da_kernel, n_seg=n_seg, post_scale=1.0 - lam_init, n_tiles=n_tiles, nq=nq),
        grid=(n_tiles + 2,),
        in_specs=in_specs,
        out_specs=pl.BlockSpec((1, tq, hw), lambda g: (pv_tile(g)[0], pv_tile(g)[2], pv_tile(g)[1])),
        out_shape=jax.ShapeDtypeStruct((b, lq, D_MODEL), BF16),
        scratch_shapes=[pltpu.VMEM((n_keys, hw), BF16), pltpu.VMEM((hw + DA_SUM_ROWS, n_keys), BF16),
                        s_buf, s_buf, p_buf, p_buf, mx_buf, mx_buf],
        compiler_params=_cparams(("arbitrary",)),
        name=name,
    )(lam.reshape(1, 1), q_arr, *kv_arrs, *kv_arrs, subln_g.reshape(hw, 1))


WA_SUM_ROWS = 16


def _wa_kernel(*refs, windowed, seq_len):
    if windowed:
        sink_ref, q_ref, kc_ref, vc_ref, kp_ref, kq_ref, kn_ref, vp_ref, vq_ref, vn_ref, o_ref = refs
        k_all = jnp.concatenate([kc_ref[0], kp_ref[0], kq_ref[0], kn_ref[0]], axis=0)
        v_all = jnp.concatenate([vc_ref[0], vp_ref[0], vq_ref[0], vn_ref[0]], axis=0)
    else:
        sink_ref, q_ref, kc_ref, vc_ref, o_ref = refs
        k_all, v_all = kc_ref[0], vc_ref[0]
    tq = q_ref.shape[1]
    n_keys = k_all.shape[0]
    n_ctx = kc_ref.shape[1]
    n_cols = 2 * tq
    i = pl.program_id(1)
    col = lax.broadcasted_iota(jnp.int32, (1, n_cols), 1)
    first = col < tq
    if windowed:
        row = lax.broadcasted_iota(jnp.int32, (n_keys, n_cols), 0)
        qpos = i * tq + jnp.where(first, col, col - tq)
        kpos = (i - 1) * tq + row - n_ctx
        ok = (row < n_ctx) | ((jnp.abs(qpos - kpos) <= WINDOW) & (kpos >= 0) & (kpos < seq_len))
    lane = lax.broadcasted_iota(jnp.int32, (1, LANES), 1)
    lo = lane < WA_HEAD
    log2e = math.log2(math.e)
    qscale = WA_HEAD ** -0.5 * log2e
    rep = WA_Q_HEADS // WA_KV_HEADS
    v_t = v_all.astype(F32).T
    sum_rows = jnp.where(lax.broadcasted_iota(jnp.int32, (WA_SUM_ROWS, n_keys), 0) == 0, 1.0, 0.0)
    def scores(g, half):
        kg = k_all[:, (g // 2) * LANES:(g // 2 + 1) * LANES]
        zero = jnp.zeros_like(kg)
        k_same = jnp.where(lo, kg, zero) if g % 2 == 0 else jnp.where(lo, zero, kg)
        if (g % 2 == 0) == (half == 0):
            k_half = k_same
        else:
            k_half = pltpu.roll(k_same.astype(F32), WA_HEAD, 1).astype(BF16)
        chunks = [q_ref[0, :, (g * (rep // 2) + pair) * LANES:(g * (rep // 2) + pair + 1) * LANES]
                  for pair in range(rep // 2)]
        qs = (jnp.concatenate(chunks, axis=0).astype(F32) * qscale).astype(BF16)
        return _dot_nt(k_half, qs)

    def weighted(g, half, s_t):
        if windowed:
            s_t = jnp.where(ok, s_t, -1e30)
        snk = jnp.where(first, sink_ref[g * rep + half], sink_ref[g * rep + 2 + half]) * log2e
        mx = jnp.maximum(jnp.max(s_t, axis=0, keepdims=True), snk)
        vt_g = jnp.concatenate([v_t[g * WA_HEAD:(g + 1) * WA_HEAD, :], sum_rows], axis=0).astype(BF16)
        acc = _dot(vt_g, jnp.exp2((s_t - mx).astype(BF16)))
        l = acc[WA_HEAD:WA_HEAD + 1] + jnp.exp2(snk - mx)
        return acc[:WA_HEAD] * (1.0 / l)

    units = [(g, half) for g in range(WA_KV_HEADS) for half in range(2)]
    outs = []
    nxt = scores(*units[0])
    for u, (g, half) in enumerate(units):
        cur = nxt
        if u + 1 < len(units):
            nxt = scores(*units[u + 1])
        outs.append(weighted(g, half, cur))
        if half == 1:
            for pair in range(rep // 2):
                cols = slice(pair * tq, (pair + 1) * tq)
                o_t = jnp.concatenate([outs[-2][:, cols], outs[-1][:, cols]], axis=0)
                chunk = g * (rep // 2) + pair
                o_ref[0, :, chunk * LANES:(chunk + 1) * LANES] = o_t.T.astype(o_ref.dtype)


def _window_attention(q_arr, ctx_arr, sink, *, windowed, name):
    b, lq, _ = q_arr.shape
    n_ctx = ctx_arr.shape[1]
    qw = WA_Q_HEADS * WA_HEAD
    kw = WA_KV_HEADS * WA_HEAD
    k_col, v_col = qw // kw, qw // kw + 1
    tq = 128 if windowed else lq
    nq = lq // tq
    in_specs = [pl.BlockSpec(memory_space=pltpu.SMEM),
                pl.BlockSpec((1, tq, qw), lambda bi, i: (bi, i, 0)),
                pl.BlockSpec((1, n_ctx, kw), lambda bi, i: (bi, 0, k_col)),
                pl.BlockSpec((1, n_ctx, kw), lambda bi, i: (bi, 0, v_col))]
    args = [sink, q_arr, ctx_arr, ctx_arr]
    if windowed:
        assert tq == WINDOW
        for col in (k_col, v_col):
            in_specs += [pl.BlockSpec((1, tq, kw), lambda bi, i, col=col: (bi, jnp.maximum(i - 1, 0), col)),
                         pl.BlockSpec((1, tq, kw), lambda bi, i, col=col: (bi, i, col)),
                         pl.BlockSpec((1, tq, kw), lambda bi, i, col=col: (bi, jnp.minimum(i + 1, nq - 1), col))]
            args += [q_arr, q_arr, q_arr]
    return pl.pallas_call(
        functools.partial(_wa_kernel, windowed=windowed, seq_len=lq),
        grid=(b, nq),
        in_specs=in_specs,
        out_specs=pl.BlockSpec((1, tq, qw), lambda bi, i: (bi, i, 0)),
        out_shape=jax.ShapeDtypeStruct((b, lq, qw), BF16),
        compiler_params=_cparams(("parallel", "parallel")),
        name=name,
    )(*args)


def _ret_kernel(cd_ref, qcf, kcf, vcf, qlf, klf, vlf, qcb, kcb, vcb, qlb, klb, vlb, dtab_ref, qd_ref, kd_ref,
                ocf, olf, ocb, olb, s_scr, *, n_ctx_chunks):
    j = pl.program_id(1)
    is_ctx = j < n_ctx_chunks

    @pl.when(j == 0)
    def _():
        s_scr[...] = jnp.zeros_like(s_scr)

    kscale = jnp.asarray(RET_KEY ** -0.5, BF16)
    dirs = ((qcf, kcf, vcf, qlf, klf, vlf), (qcb, kcb, vcb, qlb, klb, vlb))
    qkv = []
    for qc, kc, vc, ql, kl, vl in dirs:
        qkv.append((jnp.where(is_ctx, qc[0], ql[0]), jnp.where(is_ctx, kc[0], kl[0]) * kscale,
                    jnp.where(is_ctx, vc[0], vl[0])))

    def products(d, h):
        q, k, v = qkv[d]
        qh = q[:, h * RET_KEY:(h + 1) * RET_KEY]
        kh = k[:, h * RET_KEY:(h + 1) * RET_KEY]
        vh = v[:, h * RET_VAL:(h + 1) * RET_VAL]
        state = s_scr[d, h]
        kdec = (kh.astype(F32) * kd_ref[d, h]).astype(BF16)
        return _dot_nt(qh, kh), _dot(qh, state.astype(BF16)), _dot_tn(kdec, vh), state, vh

    def finish(d, h, prod):
        qk, q_state, k_v, state, vh = prod
        s_scr[d, h] = state * cd_ref[d, h] + k_v
        return _dot((qk * dtab_ref[d, h]).astype(BF16), vh) + q_state * qd_ref[d, h]

    units = [(d, h) for d in range(2) for h in range(RET_HEADS)]
    outs = []
    nxt = products(*units[0])
    for u, (d, h) in enumerate(units):
        cur = nxt
        if u + 1 < len(units):
            nxt = products(*units[u + 1])
        outs.append(finish(d, h, cur))
    o_f = jnp.concatenate(outs[:RET_HEADS], axis=1).astype(ocf.dtype)
    o_b = jnp.concatenate(outs[RET_HEADS:], axis=1).astype(ocb.dtype)

    @pl.when(is_ctx)
    def _():
        ocf[0] = o_f
        ocb[0] = o_b

    @pl.when(jnp.logical_not(is_ctx))
    def _():
        olf[0] = o_f
        olb[0] = o_b


def _retention(ctx_arr, lat_arr, log_g):
    b, lc, _ = ctx_arr.shape
    l = lat_arr.shape[1]
    c = RET_CHUNK
    nc, nl = lc // c, l // c
    qw = RET_HEADS * RET_KEY
    vw = RET_HEADS * RET_VAL
    idx = jnp.arange(c, dtype=F32)
    rel = idx[:, None] - idx[None, :]
    lg = log_g[:, :, None, None]
    keep_f = (rel >= 0)[None]
    keep_b = (rel < 0)[None]
    d_f = jnp.where(keep_f, jnp.exp(jnp.where(keep_f, rel[None] * lg[0], 0.0)), 0.0)
    d_b = jnp.where(keep_b, jnp.exp(jnp.where(keep_b, -rel[None] * lg[1], 0.0)), 0.0)
    dtab = jnp.stack([d_f, d_b])
    col = idx[None, :, None]
    qd = jnp.stack([jnp.exp((col + 1.0) * lg[0]), jnp.exp((c - col) * lg[1])])
    kd = jnp.stack([jnp.exp((c - 1.0 - col) * lg[0]), jnp.exp(col * lg[1])])
    cd = jnp.exp(c * log_g)

    f_ctx = lambda bi, j: (bi, jnp.minimum(j, nc - 1))
    f_lat = lambda bi, j: (bi, jnp.maximum(j - nc, 0))
    b_ctx = lambda bi, j: (bi, jnp.maximum(nc - 1 - j, 0))
    b_lat = lambda bi, j: (bi, jnp.minimum(nc + nl - 1 - j, nl - 1))

    def qkv_specs(row_map):
        return [pl.BlockSpec((1, c, qw), lambda bi, j: row_map(bi, j) + (0,)),
                pl.BlockSpec((1, c, qw), lambda bi, j: row_map(bi, j) + (1,)),
                pl.BlockSpec((1, c, vw), lambda bi, j: row_map(bi, j) + (1,))]

    full = lambda shape: pl.BlockSpec(shape, lambda bi, j: (0,) * len(shape))
    in_specs = ([pl.BlockSpec(memory_space=pltpu.SMEM)]
                + qkv_specs(f_ctx) + qkv_specs(f_lat) + qkv_specs(b_ctx) + qkv_specs(b_lat)
                + [full(dtab.shape), full(qd.shape), full(kd.shape)])
    out_spec = lambda row_map: pl.BlockSpec((1, c, vw), lambda bi, j: row_map(bi, j) + (0,))
    out_sds = lambda n: jax.ShapeDtypeStruct((b, n, vw), BF16)
    return pl.pallas_call(
        functools.partial(_ret_kernel, n_ctx_chunks=nc),
        grid=(b, nc + nl),
        in_specs=in_specs,
        out_specs=[out_spec(f_ctx), out_spec(f_lat), out_spec(b_ctx), out_spec(b_lat)],
        out_shape=[out_sds(lc), out_sds(l), out_sds(lc), out_sds(l)],
        scratch_shapes=[pltpu.VMEM((2, RET_HEADS, RET_KEY, RET_VAL), F32)],
        compiler_params=_cparams(("parallel", "arbitrary")),
        name="retention_scan",
    )(cd, ctx_arr, ctx_arr, ctx_arr, lat_arr, lat_arr, lat_arr, ctx_arr, ctx_arr, ctx_arr, lat_arr, lat_arr,
      lat_arr, dtab, qd, kd)


def _router_kernel(x_ref, g_ref, sh_ref, sc_ref, wr_ref, *rest):
    a_ref, lg_ref = rest[-2:]
    a = _modulated(x_ref, g_ref, sh_ref, sc_ref)
    a_hi = a.astype(BF16)
    a_ref[...] = a_hi
    a_lo = (a - a_hi.astype(F32)).astype(BF16)
    w = wr_ref[...]
    w_hi = w.astype(BF16)
    w_lo = (w - w_hi.astype(F32)).astype(BF16)
    logits = _dot(a_hi, w_hi) + _dot(a_lo, w_hi) + _dot(a_hi, w_lo)
    lane = lax.broadcasted_iota(jnp.int32, logits.shape, 1)
    neg = -jnp.inf
    lg = jnp.where(lane < N_EXPERTS, logits, neg)
    v0 = jnp.max(lg, axis=-1, keepdims=True)
    i0 = jnp.min(jnp.where(lg == v0, lane, ROUTER_PAD), axis=-1, keepdims=True)
    lg = jnp.where(lane == i0, neg, lg)
    v1 = jnp.max(lg, axis=-1, keepdims=True)
    i1 = jnp.min(jnp.where(lg == v1, lane, ROUTER_PAD), axis=-1, keepdims=True)
    e1 = jnp.exp(v1 - v0)
    g0 = 1.0 / (1.0 + e1)
    route = jnp.where(lane == 0, g0, jnp.where(lane == 1, e1 * g0, jnp.where(
        lane == 2, i0.astype(F32), jnp.where(lane == 3, i1.astype(F32), 0.0))))
    lg_ref[...] = route


def _router(x, g, mod, sh_idx, rows_per_mod, w_router_pad, *, tm, name, pool_rows, row_offset=0, prev=None):
    t, d = x.shape
    assert row_offset % tm == 0
    first = row_offset // tm
    if rows_per_mod is None:
        mod_row = lambda i: 0
    else:
        tiles_per_mod = rows_per_mod // tm
        mod_row = lambda i: i // tiles_per_mod
    in_specs = [pl.BlockSpec((tm, d), lambda i: (i, 0)),
                pl.BlockSpec((1, d), lambda i: (0, 0)),
                pl.BlockSpec((1, 1, d), lambda i: (mod_row(i), 0, sh_idx)),
                pl.BlockSpec((1, 1, d), lambda i: (mod_row(i), 0, sh_idx + 1)),
                pl.BlockSpec((d, ROUTER_PAD), lambda i: (0, 0))]
    args = [x, g.reshape(1, d), mod, mod, w_router_pad]
    aliases = {}
    if prev is not None:
        in_specs += [pl.BlockSpec(memory_space=pl.ANY)] * 2
        aliases = {len(args): 0, len(args) + 1: 1}
        args += list(prev)
    return pl.pallas_call(
        _router_kernel,
        grid=(t // tm,),
        in_specs=in_specs,
        out_specs=[pl.BlockSpec((tm, d), lambda i: (first + i, 0)),
                   pl.BlockSpec((tm, ROUTER_PAD), lambda i: (first + i, 0))],
        out_shape=[jax.ShapeDtypeStruct((pool_rows, d), BF16), jax.ShapeDtypeStruct((pool_rows, ROUTER_PAD), F32)],
        input_output_aliases=aliases,
        compiler_params=_cparams(("parallel",)),
        name=name,
    )(*args)


def _experts_kernel(be_ref, nb_ref, x_ref, wg_ref, wu_ref, wd_ref, *rest, nf, first_block):
    o_ref, acc = rest[-2:]
    blk = pl.program_id(0)
    f = pl.program_id(1)
    used = first_block + blk < nb_ref[0]
    rc = min(x_ref.shape[0], MOE_ROW_CHUNK)

    @pl.when((blk == 0) & (f == 0))
    def _():
        acc[...] = jnp.zeros_like(acc)

    @pl.when(used)
    def _():
        wg = wg_ref[0, 0].astype(BF16)
        wu = wu_ref[0, 0].astype(BF16)
        wd = wd_ref[0, 0].astype(BF16)
        n_chunks = x_ref.shape[0] // rc

        def gate_up(c):
            x = x_ref[c * rc:(c + 1) * rc, :]
            return _dot(x, wg), _dot(x, wu)

        nxt = gate_up(0)
        for c in range(n_chunks):
            g, u = nxt
            if c + 1 < n_chunks:
                nxt = gate_up(c + 1)
            part = _dot((g * _sigmoid(g) * u).astype(BF16), wd)
            rows = slice(c * rc, (c + 1) * rc)
            acc[rows, :] = part + jnp.where(f > 0, acc[rows, :], 0.0)

    @pl.when(f == nf - 1)
    def _():
        o_ref[...] = jnp.where(used, acc[...], 0.0).astype(o_ref.dtype)


def _experts(buf, block_expert, n_used, w_gu, w_down, layer, *, first_block, total_blocks, prev=None):
    rows, d = buf.shape
    nb = rows // MOE_ROWS
    two_f = w_gu.shape[3]
    fdim = two_f // 2
    fc = _pick(fdim, (MOE_FC, 256, 128))
    nf = fdim // fc
    in_specs = [pl.BlockSpec((MOE_ROWS, d), lambda i, f, be, nu: (i, 0)),
                pl.BlockSpec((1, 1, d, fc), lambda i, f, be, nu: (layer, be[first_block + i], 0, f)),
                pl.BlockSpec((1, 1, d, fc), lambda i, f, be, nu: (layer, be[first_block + i], 0, nf + f)),
                pl.BlockSpec((1, 1, fc, d), lambda i, f, be, nu: (layer, be[first_block + i], f, 0))]
    args = [block_expert, n_used, buf, w_gu, w_gu, w_down]
    aliases = {}
    if prev is not None:
        in_specs.append(pl.BlockSpec(memory_space=pl.ANY))
        aliases = {len(args): 0}
        args.append(prev)
    grid_spec = pltpu.PrefetchScalarGridSpec(
        num_scalar_prefetch=2,
        grid=(nb, nf),
        in_specs=in_specs,
        out_specs=pl.BlockSpec((MOE_ROWS, d), lambda i, f, be, nu: (first_block + i, 0)),
        scratch_shapes=[pltpu.VMEM((MOE_ROWS, d), F32)],
    )
    return pl.pallas_call(
        functools.partial(_experts_kernel, nf=nf, first_block=first_block),
        grid_spec=grid_spec,
        out_shape=jax.ShapeDtypeStruct((total_blocks * MOE_ROWS, d), BF16),
        input_output_aliases=aliases,
        compiler_params=_cparams(("parallel", "arbitrary")),
        name="moe_experts",
    )(*args)


def _moe(a_all, top_idx, w_gu, w_down, layer):
    t, d = a_all.shape
    e = w_gu.shape[1]
    onehot = jnp.sum((top_idx[..., None] == jnp.arange(e)[None, None, :]).astype(jnp.int32), axis=1)
    counts = jnp.sum(onehot, axis=0)
    rank = jnp.cumsum(onehot, axis=0) - onehot
    padded = (counts + MOE_ROWS - 1) // MOE_ROWS * MOE_ROWS
    pend = jnp.cumsum(padded)
    pstart = pend - padded
    dest = pstart[top_idx] + jnp.take_along_axis(rank, top_idx, axis=1)
    nb = -(-(t * TOP_K) // MOE_ROWS) + e
    tok = jnp.repeat(jnp.arange(t, dtype=jnp.int32), TOP_K)
    row_src = jnp.zeros((nb * MOE_ROWS,), jnp.int32).at[dest.reshape(-1)].set(tok, unique_indices=True)
    block_expert = jnp.minimum(jnp.searchsorted(pend, jnp.arange(nb) * MOE_ROWS, side="right"), e - 1).astype(jnp.int32)
    n_used = (pend[-1] // MOE_ROWS).astype(jnp.int32).reshape(1)
    n_groups = _pick(nb, (MOE_GROUPS, 2, 1))
    per = nb // n_groups
    yb = None
    for k in range(n_groups):
        buf = a_all[row_src[k * per * MOE_ROWS:(k + 1) * per * MOE_ROWS]]
        yb = _experts(buf, block_expert, n_used, w_gu, w_down, layer, first_block=k * per, total_blocks=nb, prev=yb)
    return yb, dest


def _rope_angles(pos, dim):
    inv = np.float32(ROPE_BASE) ** (-np.arange(0, dim, 2, dtype=np.float32) / np.float32(dim))
    ang = pos.astype(np.float32)[:, None] * inv[None, :]
    return np.concatenate([ang, ang], axis=-1)


def _signed_sin(sin, shift):
    low = (np.arange(sin.shape[1]) % (2 * shift)) < shift
    zero = np.float32(0.0)
    return jnp.asarray(np.where(low[None, :], -sin, zero)), jnp.asarray(np.where(low[None, :], zero, sin))


def _axial_tables(n_tokens, head_dim):
    rows = n_tokens // GRID_W
    row = np.repeat(np.arange(rows, dtype=np.int32), GRID_W)
    col = np.tile(np.arange(GRID_W, dtype=np.int32), rows)
    half = head_dim // 2
    ang = np.concatenate([_rope_angles(row, half), _rope_angles(col, half)], axis=-1)
    ang = np.tile(ang, (1, LANES // head_dim))
    sa, sb = _signed_sin(np.sin(ang), half // 2)
    return jnp.asarray(np.cos(ang)), sa, sb, half // 2


def _ret_tables(n_tokens):
    ang = _rope_angles(np.arange(n_tokens), RET_KEY)
    sa, sb = _signed_sin(np.sin(ang), RET_KEY // 2)
    return jnp.asarray(np.cos(ang)), sa, sb, RET_KEY // 2


def kernel(x, c, ctx, c_ctx, mod_w, mod_b, norm_g, da_w_in, da_w_out, da_lambda, da_subln_g, wa_w_in, wa_w_out,
           wa_sink, ret_w_in, ret_w_out, ret_decay_logit, ffn_w_gu, ffn_w_down, moe_router, moe_w_gu, moe_w_down):
    b, l, d = x.shape
    lc = ctx.shape[1]
    depth = mod_w.shape[0]
    t, tc = b * l, b * lc
    tm = _pick(l, (1024, 512, 256, 128))
    tmc = _pick(tc, (1024, 512, 256, 128))

    n_cond = -(-(b + 1) // 8) * 8
    cs = jnp.zeros((n_cond, d), F32).at[:b].set(c).at[b].set(c_ctx)
    mod_all = _mod_vectors(cs, mod_w, mod_b)

    xs = x.reshape(t, d)
    hs = ctx.reshape(tc, d)
    mixer_count = [0] * N_MIXERS
    i_dense = 0
    i_moe = 0
    for layer in range(depth):
        last = layer == depth - 1
        mod = mod_all[layer, :b].reshape(b, 1, 6 * d)
        mod_c = mod_all[layer, b:b + 1].reshape(1, 1, 6 * d)
        ng = norm_g[layer]
        kind = layer % N_MIXERS
        jm = mixer_count[kind]
        mixer_count[kind] += 1

        if kind == 0:
            lam_init = 0.8 - 0.6 * math.exp(-0.3 * layer)
            w_in = da_w_in[jm].astype(BF16)
            cos, sa, sb, shift = _axial_tables(l, DA_HEAD)
            qkv = _norm_proj(xs, ng[0], mod, 0, l, w_in, tm=tm, tn=3 * d, rot=(cos, sa, sb, 2 * d, shift, l),
                             name="da_in_proj").reshape(b, l, 3 * d)
            qkv_c = _norm_proj(hs, ng[0], mod_c, 0, None, w_in, tm=tmc, tn=3 * d, name="da_in_proj_ctx").reshape(b, lc, 3 * d)
            lp = da_lambda[jm].astype(F32)
            lam = jnp.exp(jnp.sum(lp[0] * lp[1])) - jnp.exp(jnp.sum(lp[2] * lp[3])) + lam_init
            o = _diff_attention(qkv, [qkv_c, qkv], lam, da_subln_g[jm], lam_init, tq=_pick(l, (DA_TQ, 128)), name="diff_attn")
            w_out = da_w_out[jm].astype(BF16)
            xs = _resid(xs, ng[1], mod, 2, l, tm=tm, a=o.reshape(t, d), w=w_out, name="da_out_proj")
            if not last:
                oc = _diff_attention(qkv_c, [qkv_c], lam, da_subln_g[jm], lam_init, tq=_pick(lc, (DA_TQ, 128)), name="diff_attn_ctx")
                hs = _resid(hs, ng[1], mod_c, 2, None, tm=tmc, a=oc.reshape(tc, d), w=w_out, name="da_out_proj_ctx")
        elif kind == 1:
            w_in = wa_w_in[jm].astype(BF16)
            n_qkv = w_in.shape[1]
            cos, sa, sb, shift = _axial_tables(l, WA_HEAD)
            n_rot = (WA_Q_HEADS + WA_KV_HEADS) * WA_HEAD
            qkv = _norm_proj(xs, ng[0], mod, 0, l, w_in, tm=tm, tn=n_qkv, rot=(cos, sa, sb, n_rot, shift, l),
                             name="wa_in_proj").reshape(b, l, n_qkv)
            qkv_c = _norm_proj(hs, ng[0], mod_c, 0, None, w_in, tm=tmc, tn=n_qkv, name="wa_in_proj_ctx").reshape(b, lc, n_qkv)
            o = _window_attention(qkv, qkv_c, wa_sink[jm], windowed=True, name="window_attn")
            w_out = wa_w_out[jm].astype(BF16)
            xs = _resid(xs, ng[1], mod, 2, l, tm=tm, a=o.reshape(t, d), w=w_out, name="wa_out_proj")
            if not last:
                oc = _window_attention(qkv_c, qkv_c, wa_sink[jm], windowed=False, name="window_attn_ctx")
                hs = _resid(hs, ng[1], mod_c, 2, None, tm=tmc, a=oc.reshape(tc, d), w=w_out, name="wa_out_proj_ctx")
        else:
            w_in = ret_w_in[jm].astype(BF16)
            n_qkv = w_in.shape[1]
            cos, sa, sb, shift = _ret_tables(l)
            n_rot = 2 * RET_HEADS * RET_KEY
            qkv = _norm_proj(xs, ng[0], mod, 0, l, w_in, tm=tm, tn=n_qkv // 2, rot=(cos, sa, sb, n_rot, shift, l),
                             name="ret_in_proj")
            qkv_c = _norm_proj(hs, ng[0], mod_c, 0, None, w_in, tm=_pick(tc, (512, 256, 128)), tn=n_qkv // 2,
                               name="ret_in_proj_ctx")
            log_g = jax.nn.log_sigmoid(ret_decay_logit[jm].astype(F32))
            ocf, olf, ocb, olb = _retention(qkv_c.reshape(b, lc, n_qkv), qkv.reshape(b, l, n_qkv), log_g)
            w_out = ret_w_out[jm].astype(BF16)
            vw = RET_HEADS * RET_VAL
            xs = _resid(xs, ng[1], mod, 2, l, tm=_pick(l, (512, 256, 128)), w=w_out,
                        ret=(olf.reshape(t, vw), olb.reshape(t, vw), qkv), name="ret_out_proj")
            if not last:
                hs = _resid(hs, ng[1], mod_c, 2, None, tm=_pick(tc, (512, 256, 128)), w=w_out,
                            ret=(ocf.reshape(tc, vw), ocb.reshape(tc, vw), qkv_c), name="ret_out_proj_ctx")

        if layer % 2 == 0:
            w_gu = ffn_w_gu[i_dense].astype(BF16)
            w_dn = ffn_w_down[i_dense].astype(BF16)
            i_dense += 1
            f = w_dn.shape[0]
            w_g, w_u = w_gu[:, :f], w_gu[:, f:]
            tn = f
            tf = _pick(l, (512, 256, 128))
            act = _norm_proj(xs, ng[2], mod, 3, l, (w_g, w_u), tm=tf, tn=tn, name="ffn_up")
            xs = _resid(xs, ng[3], mod, 5, l, tm=tf, a=act, w=w_dn, name="ffn_down")
            if not last:
                tfc = _pick(tc, (512, 256, 128))
                act_c = _norm_proj(hs, ng[2], mod_c, 3, None, (w_g, w_u), tm=tfc, tn=tn, name="ffn_up_ctx")
                hs = _resid(hs, ng[3], mod_c, 5, None, tm=tfc, a=act_c, w=w_dn, name="ffn_down_ctx")
        else:
            w_r = jnp.zeros((d, ROUTER_PAD), F32).at[:, :N_EXPERTS].set(moe_router[i_moe])
            moe_layer = i_moe
            i_moe += 1
            pool = t if last else t + tc
            a_all, route = _router(xs, ng[2], mod, 3, l, w_r, tm=tm, name="moe_router", pool_rows=pool)
            if not last:
                a_all, route = _router(hs, ng[2], mod_c, 3, None, w_r, tm=tmc, name="moe_router_ctx", pool_rows=pool,
                                       row_offset=t, prev=(a_all, route))
            route_x, route_c = route[:t], route[t:]
            yb, dest = _moe(a_all, route[:, 2:4].astype(jnp.int32), moe_w_gu, moe_w_down, moe_layer)
            xs = _resid(xs, ng[3], mod, 5, l, tm=tm, mix=(yb[dest[:t, 0]], yb[dest[:t, 1]], route_x), name="moe_combine")
            if not last:
                hs = _resid(hs, ng[3], mod_c, 5, None, tm=tmc, mix=(yb[dest[t:, 0]], yb[dest[t:, 1]], route_c),
                            name="moe_combine_ctx")
    return xs.reshape(b, l, d)
```

```python
import functools
import math

import jax
import jax.numpy as jnp
import numpy as np
from jax import lax
from jax.experimental import pallas as pl
from jax.experimental.pallas import tpu as pltpu

F32 = jnp.float32
BF16 = jnp.bfloat16

D_MODEL = 1024
N_MIXERS = 3
NORM_EPS = 1e-6
ROPE_BASE = 10000.0
GRID_W = 64
DA_HEAD = 64
DA_HEADS = D_MODEL // (2 * DA_HEAD)
WA_HEAD = 64
WA_Q_HEADS = D_MODEL // WA_HEAD
WA_KV_HEADS = WA_Q_HEADS // 4
WINDOW = 128
RET_KEY = 256
RET_HEADS = D_MODEL // RET_KEY
RET_VAL = 2 * RET_KEY
RET_CHUNK = 128
N_EXPERTS = 8
TOP_K = 2

LANES = 128
V7X_VMEM_LIMIT = 56 * 1024 * 1024
ROUTER_PAD = LANES
MOE_ROWS = 1024
MOE_FC = 512
MOE_ROW_CHUNK = 256
MOE_GROUPS = 4
PROJ_COLS = 512


def _cparams(sem):
    return pltpu.CompilerParams(dimension_semantics=sem, vmem_limit_bytes=V7X_VMEM_LIMIT)


def _pick(n, prefs):
    for p in prefs:
        if n % p == 0:
            return p
    return n


def _sigmoid(x):
    return 1.0 / (1.0 + jnp.exp(-x))


def _rms(y):
    return y * lax.rsqrt(jnp.mean(y * y, axis=-1, keepdims=True) + NORM_EPS)


def _dot(a, b):
    return jnp.dot(a, b, preferred_element_type=F32)


def _dot_nt(a, b):
    return lax.dot_general(a, b, (((1,), (1,)), ((), ())), preferred_element_type=F32)


def _dot_tn(a, b):
    return lax.dot_general(a, b, (((0,), (0,)), ((), ())), preferred_element_type=F32)


def _mod_kernel(c_ref, w_ref, b_ref, o_ref):
    cs = c_ref[...]
    s = cs * _sigmoid(cs)
    w = w_ref[0]
    s_hi = s.astype(BF16)
    s_lo = (s - s_hi.astype(F32)).astype(BF16)
    w_hi = w.astype(BF16)
    w_lo = (w - w_hi.astype(F32)).astype(BF16)
    o_ref[0] = _dot(s_hi, w_hi) + _dot(s_lo, w_hi) + _dot(s_hi, w_lo) + b_ref[0]


def _mod_vectors(cs, mod_w, mod_b):
    depth, d, n = mod_w.shape
    r = cs.shape[0]
    tn = _pick(n, (1024, 512, 256, 128))
    return pl.pallas_call(
        _mod_kernel,
        grid=(depth, n // tn),
        in_specs=[
            pl.BlockSpec((r, d), lambda l, j: (0, 0)),
            pl.BlockSpec((1, d, tn), lambda l, j: (l, 0, j)),
            pl.BlockSpec((1, 1, tn), lambda l, j: (l, 0, j)),
        ],
        out_specs=pl.BlockSpec((1, r, tn), lambda l, j: (l, 0, j)),
        out_shape=jax.ShapeDtypeStruct((depth, r, n), F32),
        compiler_params=_cparams(("parallel", "parallel")),
        name="mod_vectors",
    )(cs, mod_w, mod_b.reshape(depth, 1, n))


def _modulated(x_ref, g_ref, sh_ref, sc_ref):
    x = x_ref[...]
    return _rms(x) * g_ref[...] * (1.0 + sc_ref[0]) + sh_ref[0]


def _rotate(acc, cos, sa, sb, shift, cw):
    outs = []
    for c in range(acc.shape[1] // cw):
        y = acc[:, c * cw:(c + 1) * cw]
        up = pltpu.roll(y, cw - shift, 1)
        dn = pltpu.roll(y, shift, 1)
        outs.append(y * cos + up * sa + dn * sb)
    return outs[0] if len(outs) == 1 else jnp.concatenate(outs, axis=1)


def _col_chunks(width, cuts):
    edges = sorted({0, width} | set(range(PROJ_COLS, width, PROJ_COLS)) | {c for c in cuts if 0 < c < width})
    return list(zip(edges[:-1], edges[1:]))


def _proj_kernel(*refs, mode, n_rot_cols, shift, cw, n_col_tiles):
    if mode == "swiglu":
        x_ref, g_ref, sh_ref, sc_ref, wg_ref, wu_ref, o_ref, a_scr = refs
    elif mode == "rot":
        x_ref, g_ref, sh_ref, sc_ref, w_ref, cos_ref, sa_ref, sb_ref, o_ref, a_scr = refs
    else:
        x_ref, g_ref, sh_ref, sc_ref, w_ref, o_ref, a_scr = refs
    j = pl.program_id(1)
    tn = o_ref.shape[1]

    @pl.when(j == 0)
    def _():
        a_scr[...] = _modulated(x_ref, g_ref, sh_ref, sc_ref).astype(BF16)

    def tile(jt):
        a = a_scr[...]
        for lo, hi in _col_chunks(tn, [n_rot_cols - jt * tn]):
            if mode == "swiglu":
                g = _dot(a, wg_ref[:, lo:hi])
                res = g * _sigmoid(g) * _dot(a, wu_ref[:, lo:hi])
            else:
                res = _dot(a, w_ref[:, lo:hi])
                if mode == "rot" and jt * tn + hi <= n_rot_cols:
                    res = _rotate(res, cos_ref[...], sa_ref[...], sb_ref[...], shift, cw)
            o_ref[:, lo:hi] = res.astype(o_ref.dtype)

    if mode != "rot":
        tile(0)
    else:
        n_rot_tiles = -(-n_rot_cols // tn)
        for jt in range(min(n_rot_tiles + 1, n_col_tiles)):
            @pl.when((j >= jt) if jt == n_rot_tiles else (j == jt))
            def _(jt=jt):
                tile(jt)


def _norm_proj(x, g, mod, sh_idx, rows_per_mod, weights, *, tm, tn, rot=None, name):
    t, d = x.shape
    swiglu = isinstance(weights, tuple)
    n = weights[0].shape[1] if swiglu else weights.shape[1]
    assert t % tm == 0 and n % tn == 0
    if rows_per_mod is None:
        mod_row = lambda i: 0
    else:
        assert rows_per_mod % tm == 0
        tiles_per_mod = rows_per_mod // tm
        mod_row = lambda i: i // tiles_per_mod
    in_specs = [
        pl.BlockSpec((tm, d), lambda i, j: (i, 0)),
        pl.BlockSpec((1, d), lambda i, j: (0, 0)),
        pl.BlockSpec((1, 1, d), lambda i, j: (mod_row(i), 0, sh_idx)),
        pl.BlockSpec((1, 1, d), lambda i, j: (mod_row(i), 0, sh_idx + 1)),
    ]
    args = [x, g.reshape(1, d), mod, mod]
    w_spec = pl.BlockSpec((d, tn), lambda i, j: (0, j))
    n_rot_cols, shift, cw = 0, 0, 0
    if swiglu:
        mode = "swiglu"
        in_specs += [w_spec, w_spec]
        args += list(weights)
    elif rot is not None:
        mode = "rot"
        cos, sa, sb, n_rot_cols, shift, seq_len = rot
        cw = cos.shape[1]
        assert n_rot_cols % cw == 0 and PROJ_COLS % cw == 0 and seq_len % tm == 0
        tiles_per_seq = seq_len // tm
        t_spec = pl.BlockSpec((tm, cw), lambda i, j: (i % tiles_per_seq, 0))
        in_specs += [w_spec, t_spec, t_spec, t_spec]
        args += [weights, cos, sa, sb]
    else:
        mode = "plain"
        in_specs += [w_spec]
        args += [weights]
    return pl.pallas_call(
        functools.partial(_proj_kernel, mode=mode, n_rot_cols=n_rot_cols, shift=shift, cw=cw, n_col_tiles=n // tn),
        grid=(t // tm, n // tn),
        in_specs=in_specs,
        out_specs=pl.BlockSpec((tm, tn), lambda i, j: (i, j)),
        out_shape=jax.ShapeDtypeStruct((t, n), BF16),
        scratch_shapes=[pltpu.VMEM((tm, d), BF16)],
        compiler_params=_cparams(("parallel", "arbitrary")),
        name=name,
    )(*args)


def _resid_kernel(*refs, mode):
    if mode == "ret":
        of_ref, ob_ref, gt_ref, w_ref, x_ref, gate_ref, g_ref, o_ref = refs
        o = of_ref[...].astype(F32) + ob_ref[...].astype(F32)
        parts = []
        for h in range(RET_HEADS):
            parts.append(_rms(o[:, h * RET_VAL:(h + 1) * RET_VAL]))
        o = jnp.concatenate(parts, axis=1)
        gt = gt_ref[...].astype(F32)
        y = _dot((gt * _sigmoid(gt) * o).astype(BF16), w_ref[...])
    elif mode == "matmul":
        a_ref, w_ref, x_ref, gate_ref, g_ref, o_ref = refs
        y = _dot(a_ref[...], w_ref[...])
    else:
        y0_ref, y1_ref, route_ref, x_ref, gate_ref, g_ref, o_ref = refs
        route = route_ref[...]
        y = y0_ref[...].astype(F32) * route[:, 0:1] + y1_ref[...].astype(F32) * route[:, 1:2]
    o_ref[...] = x_ref[...] + gate_ref[0] * (_rms(y) * g_ref[...])


def _resid(x, g, mod, gate_idx, rows_per_mod, *, tm, name, a=None, w=None, mix=None, ret=None):
    t, d = x.shape
    assert t % tm == 0
    if rows_per_mod is None:
        mod_row = lambda i: 0
    else:
        assert rows_per_mod % tm == 0
        tiles_per_mod = rows_per_mod // tm
        mod_row = lambda i: i // tiles_per_mod
    row = lambda width: pl.BlockSpec((tm, width), lambda i: (i, 0))
    tail_specs = [row(d), pl.BlockSpec((1, 1, d), lambda i: (mod_row(i), 0, gate_idx)),
                  pl.BlockSpec((1, d), lambda i: (0, 0))]
    tail_args = [x, mod, g.reshape(1, d)]
    if ret is not None:
        mode = "ret"
        o_f, o_b, qkvg = ret
        vw = RET_HEADS * RET_VAL
        gate_col = qkvg.shape[1] // vw - 1
        in_specs = [row(vw), row(vw), pl.BlockSpec((tm, vw), lambda i: (i, gate_col)),
                    pl.BlockSpec(w.shape, lambda i: (0, 0))]
        args = [o_f, o_b, qkvg, w]
    elif a is not None:
        mode = "matmul"
        in_specs = [row(a.shape[1]), pl.BlockSpec(w.shape, lambda i: (0, 0))]
        args = [a, w]
    else:
        mode = "mix"
        in_specs = [row(d), row(d), row(ROUTER_PAD)]
        args = list(mix)
    return pl.pallas_call(
        functools.partial(_resid_kernel, mode=mode),
        grid=(t // tm,),
        in_specs=in_specs + tail_specs,
        out_specs=row(d),
        out_shape=jax.ShapeDtypeStruct((t, d), F32),
        compiler_params=_cparams(("parallel",)),
        name=name,
    )(*args, *tail_args)


DA_SUM_ROWS = 16
DA_TQ = 256
DA_EXP_ROWS = 32


def _da_kernel(*refs, n_seg, post_scale, n_tiles, nq):
    lam_ref, q_ref = refs[0], refs[1]
    k_refs = refs[2:2 + n_seg]
    v_refs = refs[2 + n_seg:2 + 2 * n_seg]
    sg_ref, o_ref, k_scr, vt_scr, s_a, s_b, p_a, p_b, mx_a, mx_b = refs[2 + 2 * n_seg:]
    hw = 2 * DA_HEAD
    n_keys = vt_scr.shape[1]
    tq = q_ref.shape[1]
    g = pl.program_id(0)
    t_qk = jnp.minimum(g, n_tiles - 1)
    t_pv = jnp.clip(g - 2, 0, n_tiles - 1)

    @pl.when(g == 0)
    def _():
        for buf in (s_a, s_b, p_a, p_b, mx_a, mx_b):
            buf[...] = jnp.zeros_like(buf)

    @pl.when(lax.rem(t_qk, nq) == 0)
    def _():
        off = 0
        for kr in k_refs:
            n = kr.shape[1]
            k_scr[off:off + n, :] = kr[0]
            off += n

    @pl.when(lax.rem(t_pv, nq) == 0)
    def _():
        off = 0
        for vr in v_refs:
            n = vr.shape[1]
            vt_scr[0:hw, off:off + n] = vr[0].astype(F32).T.astype(BF16)
            off += n
        row = lax.broadcasted_iota(jnp.int32, (DA_SUM_ROWS, n_keys), 0)
        vt_scr[hw:hw + DA_SUM_ROWS, :] = jnp.where(row == 0, 1.0, 0.0).astype(BF16)

    lane = lax.broadcasted_iota(jnp.int32, (1, hw), 1)
    qscale = DA_HEAD ** -0.5 * math.log2(math.e)

    def stages(s_new, mx_new, s_old, mx_old, p_old, p_older):
        accs = []
        for m in range(2):
            accs.append(_dot(vt_scr[...], p_older[m]))
        q = q_ref[0].astype(F32)
        for m in range(2):
            sel = (lane < DA_HEAD) if m == 0 else (lane >= DA_HEAD)
            qm = jnp.where(sel, q * qscale, 0.0).astype(BF16)
            s_t = _dot_nt(k_scr[...], qm)
            s_new[m] = s_t
            part = [None] * 4
            for r in range(n_keys // 8):
                tile = s_t[r * 8:(r + 1) * 8, :]
                j = r % len(part)
                part[j] = tile if part[j] is None else jnp.maximum(part[j], tile)
            mx_new[m] = functools.reduce(jnp.maximum, part)
            mx8 = jnp.broadcast_to(jnp.max(mx_old[m], axis=0, keepdims=True), (8, tq))
            for c in range(n_keys // DA_EXP_ROWS):
                rows = slice(c * DA_EXP_ROWS, (c + 1) * DA_EXP_ROWS)
                zero = jnp.minimum(jnp.abs(s_t[c * DA_EXP_ROWS:c * DA_EXP_ROWS + 8, :]), 0.0)
                d = s_old[m, rows, :].reshape(DA_EXP_ROWS // 8, 8, tq) - (mx8 + zero)[None]
                p_old[m, rows, :] = jnp.exp2(d.reshape(DA_EXP_ROWS, tq).astype(BF16))
        a0, a1 = accs
        o_t = a0[:hw] * (1.0 / a0[hw:hw + 1]) - a1[:hw] * (lam_ref[0, 0] / a1[hw:hw + 1])
        o_t = o_t * lax.rsqrt(jnp.mean(o_t * o_t, axis=0, keepdims=True) + NORM_EPS) * (sg_ref[...] * post_scale)
        o_ref[0] = o_t.T.astype(o_ref.dtype)

    @pl.when(lax.rem(g, 2) == 0)
    def _():
        stages(s_a, mx_a, s_b, mx_b, p_b, p_a)

    @pl.when(lax.rem(g, 2) == 1)
    def _():
        stages(s_b, mx_b, s_a, mx_a, p_a, p_b)


def _diff_attention(q_arr, kv_arrs, lam, subln_g, lam_init, *, tq, name):
    b, lq, _ = q_arr.shape
    hw = 2 * DA_HEAD
    n_seg = len(kv_arrs)
    n_keys = sum(kv.shape[1] for kv in kv_arrs)
    assert lq % tq == 0 and n_keys % DA_EXP_ROWS == 0
    nq = lq // tq
    n_tiles = b * DA_HEADS * nq

    def tile(t):
        return t // (nq * DA_HEADS), lax.rem(t // nq, DA_HEADS), lax.rem(t, nq)

    def qk_tile(g):
        return tile(jnp.minimum(g, n_tiles - 1))

    def pv_tile(g):
        return tile(jnp.clip(g - 2, 0, n_tiles - 1))

    in_specs = [pl.BlockSpec(memory_space=pltpu.SMEM),
                pl.BlockSpec((1, tq, hw), lambda g: (qk_tile(g)[0], qk_tile(g)[2], qk_tile(g)[1]))]
    in_specs += [pl.BlockSpec((1, kv.shape[1], hw), lambda g: (qk_tile(g)[0], 0, DA_HEADS + qk_tile(g)[1]))
                 for kv in kv_arrs]
    in_specs += [pl.BlockSpec((1, kv.shape[1], hw), lambda g: (pv_tile(g)[0], 0, 2 * DA_HEADS + pv_tile(g)[1]))
                 for kv in kv_arrs]
    in_specs += [pl.BlockSpec((hw, 1), lambda g: (0, 0))]
    s_buf = pltpu.VMEM((2, n_keys, tq), F32)
    p_buf = pltpu.VMEM((2, n_keys, tq), BF16)
    mx_buf = pltpu.VMEM((2, 8, tq), F32)
    return pl.pallas_call(
        functools.partial(_da_kernel, n_seg=n_seg, post_scale=1.0 - lam_init, n_tiles=n_tiles, nq=nq),
        grid=(n_tiles + 2,),
        in_specs=in_specs,
        out_specs=pl.BlockSpec((1, tq, hw), lambda g: (pv_tile(g)[0], pv_tile(g)[2], pv_tile(g)[1])),
        out_shape=jax.ShapeDtypeStruct((b, lq, D_MODEL), BF16),
        scratch_shapes=[pltpu.VMEM((n_keys, hw), BF16), pltpu.VMEM((hw + DA_SUM_ROWS, n_keys), BF16),
                        s_buf, s_buf, p_buf, p_buf, mx_buf, mx_buf],
        compiler_params=_cparams(("arbitrary",)),
        name=name,
    )(lam.reshape(1, 1), q_arr, *kv_arrs, *kv_arrs, subln_g.reshape(hw, 1))


WA_SUM_ROWS = 16


def _wa_kernel(*refs, windowed, seq_len):
    if windowed:
        sink_ref, q_ref, kc_ref, vc_ref, kp_ref, kq_ref, kn_ref, vp_ref, vq_ref, vn_ref, o_ref = refs
        k_all = jnp.concatenate([kc_ref[0], kp_ref[0], kq_ref[0], kn_ref[0]], axis=0)
        v_all = jnp.concatenate([vc_ref[0], vp_ref[0], vq_ref[0], vn_ref[0]], axis=0)
    else:
        sink_ref, q_ref, kc_ref, vc_ref, o_ref = refs
        k_all, v_all = kc_ref[0], vc_ref[0]
    tq = q_ref.shape[1]
    n_keys = k_all.shape[0]
    n_ctx = kc_ref.shape[1]
    n_cols = 2 * tq
    i = pl.program_id(1)
    col = lax.broadcasted_iota(jnp.int32, (1, n_cols), 1)
    first = col < tq
    if windowed:
        row = lax.broadcasted_iota(jnp.int32, (n_keys - n_ctx, n_cols), 0)
        qpos = i * tq + jnp.where(first, col, col - tq)
        kpos = (i - 1) * tq + row
        ok = (jnp.abs(qpos - kpos) <= WINDOW) & (kpos >= 0) & (kpos < seq_len)
    lane = lax.broadcasted_iota(jnp.int32, (1, LANES), 1)
    lo = lane < WA_HEAD
    log2e = math.log2(math.e)
    qscale = WA_HEAD ** -0.5 * log2e
    rep = WA_Q_HEADS // WA_KV_HEADS
    v_t = v_all.astype(F32).T
    sum_rows = jnp.where(lax.broadcasted_iota(jnp.int32, (WA_SUM_ROWS, n_keys), 0) == 0, 1.0, 0.0)
    def scores(g, half):
        kg = k_all[:, (g // 2) * LANES:(g // 2 + 1) * LANES]
        zero = jnp.zeros_like(kg)
        k_same = jnp.where(lo, kg, zero) if g % 2 == 0 else jnp.where(lo, zero, kg)
        if (g % 2 == 0) == (half == 0):
            k_half = k_same
        else:
            k_half = pltpu.roll(k_same.astype(F32), WA_HEAD, 1).astype(BF16)
        chunks = [q_ref[0, :, (g * (rep // 2) + pair) * LANES:(g * (rep // 2) + pair + 1) * LANES]
                  for pair in range(rep // 2)]
        qs = (jnp.concatenate(chunks, axis=0).astype(F32) * qscale).astype(BF16)
        return _dot_nt(k_half, qs)

    def weighted(g, half, s_t):
        if windowed:
            s_t = jnp.concatenate([s_t[:n_ctx], jnp.where(ok, s_t[n_ctx:], -1e30)], axis=0)
        snk = jnp.where(first, sink_ref[g * rep + half], sink_ref[g * rep + 2 + half]) * log2e
        mx = jnp.maximum(jnp.max(s_t, axis=0, keepdims=True), snk)
        vt_g = jnp.concatenate([v_t[g * WA_HEAD:(g + 1) * WA_HEAD, :], sum_rows], axis=0).astype(BF16)
        acc = _dot(vt_g, jnp.exp2((s_t - mx).astype(BF16)))
        l = acc[WA_HEAD:WA_HEAD + 1] + jnp.exp2(snk - mx)
        return acc[:WA_HEAD] * (1.0 / l)

    units = [(g, half) for g in range(WA_KV_HEADS) for half in range(2)]
    outs = []
    nxt = scores(*units[0])
    for u, (g, half) in enumerate(units):
        cur = nxt
        if u + 1 < len(units):
            nxt = scores(*units[u + 1])
        outs.append(weighted(g, half, cur))
        if half == 1:
            for pair in range(rep // 2):
                cols = slice(pair * tq, (pair + 1) * tq)
                o_t = jnp.concatenate([outs[-2][:, cols], outs[-1][:, cols]], axis=0)
                chunk = g * (rep // 2) + pair
                o_ref[0, :, chunk * LANES:(chunk + 1) * LANES] = o_t.T.astype(o_ref.dtype)


def _window_attention(q_arr, ctx_arr, sink, *, windowed, name):
    b, lq, _ = q_arr.shape
    n_ctx = ctx_arr.shape[1]
    qw = WA_Q_HEADS * WA_HEAD
    kw = WA_KV_HEADS * WA_HEAD
    k_col, v_col = qw // kw, qw // kw + 1
    tq = 128 if windowed else lq
    nq = lq // tq
    in_specs = [pl.BlockSpec(memory_space=pltpu.SMEM),
                pl.BlockSpec((1, tq, qw), lambda bi, i: (bi, i, 0)),
                pl.BlockSpec((1, n_ctx, kw), lambda bi, i: (bi, 0, k_col)),
                pl.BlockSpec((1, n_ctx, kw), lambda bi, i: (bi, 0, v_col))]
    args = [sink, q_arr, ctx_arr, ctx_arr]
    if windowed:
        assert tq == WINDOW
        for col in (k_col, v_col):
            in_specs += [pl.BlockSpec((1, tq, kw), lambda bi, i, col=col: (bi, jnp.maximum(i - 1, 0), col)),
                         pl.BlockSpec((1, tq, kw), lambda bi, i, col=col: (bi, i, col)),
                         pl.BlockSpec((1, tq, kw), lambda bi, i, col=col: (bi, jnp.minimum(i + 1, nq - 1), col))]
            args += [q_arr, q_arr, q_arr]
    return pl.pallas_call(
        functools.partial(_wa_kernel, windowed=windowed, seq_len=lq),
        grid=(b, nq),
        in_specs=in_specs,
        out_specs=pl.BlockSpec((1, tq, qw), lambda bi, i: (bi, i, 0)),
        out_shape=jax.ShapeDtypeStruct((b, lq, qw), BF16),
        compiler_params=_cparams(("parallel", "parallel")),
        name=name,
    )(*args)


def _ret_kernel(cd_ref, qcf, kcf, vcf, qlf, klf, vlf, qcb, kcb, vcb, qlb, klb, vlb, dtab_ref, qd_ref, kd_ref,
                ocf, olf, ocb, olb, s_scr, *, n_ctx_chunks):
    j = pl.program_id(1)
    is_ctx = j < n_ctx_chunks

    @pl.when(j == 0)
    def _():
        s_scr[...] = jnp.zeros_like(s_scr)

    kscale = jnp.asarray(RET_KEY ** -0.5, BF16)
    dirs = ((qcf, kcf, vcf, qlf, klf, vlf), (qcb, kcb, vcb, qlb, klb, vlb))
    qkv = []
    for qc, kc, vc, ql, kl, vl in dirs:
        qkv.append((jnp.where(is_ctx, qc[0], ql[0]), jnp.where(is_ctx, kc[0], kl[0]) * kscale,
                    jnp.where(is_ctx, vc[0], vl[0])))

    def products(d, h):
        q, k, v = qkv[d]
        qh = q[:, h * RET_KEY:(h + 1) * RET_KEY]
        kh = k[:, h * RET_KEY:(h + 1) * RET_KEY]
        vh = v[:, h * RET_VAL:(h + 1) * RET_VAL]
        state = s_scr[d, h]
        kdec = (kh.astype(F32) * kd_ref[d, h]).astype(BF16)
        return _dot_nt(qh, kh), _dot(qh, state.astype(BF16)), _dot_tn(kdec, vh), state, vh

    def finish(d, h, prod):
        qk, q_state, k_v, state, vh = prod
        s_scr[d, h] = state * cd_ref[d, h] + k_v
        return _dot((qk * dtab_ref[d, h]).astype(BF16), vh) + q_state * qd_ref[d, h]

    units = [(d, h) for d in range(2) for h in range(RET_HEADS)]
    outs = []
    nxt = products(*units[0])
    for u, (d, h) in enumerate(units):
        cur = nxt
        if u + 1 < len(units):
            nxt = products(*units[u + 1])
        outs.append(finish(d, h, cur))
    o_f = jnp.concatenate(outs[:RET_HEADS], axis=1).astype(ocf.dtype)
    o_b = jnp.concatenate(outs[RET_HEADS:], axis=1).astype(ocb.dtype)

    @pl.when(is_ctx)
    def _():
        ocf[0] = o_f
        ocb[0] = o_b

    @pl.when(jnp.logical_not(is_ctx))
    def _():
        olf[0] = o_f
        olb[0] = o_b


def _retention(ctx_arr, lat_arr, log_g):
    b, lc, _ = ctx_arr.shape
    l = lat_arr.shape[1]
    c = RET_CHUNK
    nc, nl = lc // c, l // c
    qw = RET_HEADS * RET_KEY
    vw = RET_HEADS * RET_VAL
    idx = jnp.arange(c, dtype=F32)
    rel = idx[:, None] - idx[None, :]
    lg = log_g[:, :, None, None]
    keep_f = (rel >= 0)[None]
    keep_b = (rel < 0)[None]
    d_f = jnp.where(keep_f, jnp.exp(jnp.where(keep_f, rel[None] * lg[0], 0.0)), 0.0)
    d_b = jnp.where(keep_b, jnp.exp(jnp.where(keep_b, -rel[None] * lg[1], 0.0)), 0.0)
    dtab = jnp.stack([d_f, d_b])
    col = idx[None, :, None]
    qd = jnp.stack([jnp.exp((col + 1.0) * lg[0]), jnp.exp((c - col) * lg[1])])
    kd = jnp.stack([jnp.exp((c - 1.0 - col) * lg[0]), jnp.exp(col * lg[1])])
    cd = jnp.exp(c * log_g)

    f_ctx = lambda bi, j: (bi, jnp.minimum(j, nc - 1))
    f_lat = lambda bi, j: (bi, jnp.maximum(j - nc, 0))
    b_ctx = lambda bi, j: (bi, jnp.maximum(nc - 1 - j, 0))
    b_lat = lambda bi, j: (bi, jnp.minimum(nc + nl - 1 - j, nl - 1))

    def qkv_specs(row_map):
        return [pl.BlockSpec((1, c, qw), lambda bi, j: row_map(bi, j) + (0,)),
                pl.BlockSpec((1, c, qw), lambda bi, j: row_map(bi, j) + (1,)),
                pl.BlockSpec((1, c, vw), lambda bi, j: row_map(bi, j) + (1,))]

    full = lambda shape: pl.BlockSpec(shape, lambda bi, j: (0,) * len(shape))
    in_specs = ([pl.BlockSpec(memory_space=pltpu.SMEM)]
                + qkv_specs(f_ctx) + qkv_specs(f_lat) + qkv_specs(b_ctx) + qkv_specs(b_lat)
                + [full(dtab.shape), full(qd.shape), full(kd.shape)])
    out_spec = lambda row_map: pl.BlockSpec((1, c, vw), lambda bi, j: row_map(bi, j) + (0,))
    out_sds = lambda n: jax.ShapeDtypeStruct((b, n, vw), BF16)
    return pl.pallas_call(
        functools.partial(_ret_kernel, n_ctx_chunks=nc),
        grid=(b, nc + nl),
        in_specs=in_specs,
        out_specs=[out_spec(f_ctx), out_spec(f_lat), out_spec(b_ctx), out_spec(b_lat)],
        out_shape=[out_sds(lc), out_sds(l), out_sds(lc), out_sds(l)],
        scratch_shapes=[pltpu.VMEM((2, RET_HEADS, RET_KEY, RET_VAL), F32)],
        compiler_params=_cparams(("parallel", "arbitrary")),
        name="retention_scan",
    )(cd, ctx_arr, ctx_arr, ctx_arr, lat_arr, lat_arr, lat_arr, ctx_arr, ctx_arr, ctx_arr, lat_arr, lat_arr,
      lat_arr, dtab, qd, kd)


def _router_kernel(x_ref, g_ref, sh_ref, sc_ref, wr_ref, *rest):
    a_ref, lg_ref = rest[-2:]
    a = _modulated(x_ref, g_ref, sh_ref, sc_ref)
    a_hi = a.astype(BF16)
    a_ref[...] = a_hi
    a_lo = (a - a_hi.astype(F32)).astype(BF16)
    w = wr_ref[...]
    w_hi = w.astype(BF16)
    w_lo = (w - w_hi.astype(F32)).astype(BF16)
    logits = _dot(a_hi, w_hi) + _dot(a_lo, w_hi) + _dot(a_hi, w_lo)
    lane = lax.broadcasted_iota(jnp.int32, logits.shape, 1)
    neg = -jnp.inf
    lg = jnp.where(lane < N_EXPERTS, logits, neg)
    v0 = jnp.max(lg, axis=-1, keepdims=True)
    i0 = jnp.min(jnp.where(lg == v0, lane, ROUTER_PAD), axis=-1, keepdims=True)
    lg = jnp.where(lane == i0, neg, lg)
    v1 = jnp.max(lg, axis=-1, keepdims=True)
    i1 = jnp.min(jnp.where(lg == v1, lane, ROUTER_PAD), axis=-1, keepdims=True)
    e1 = jnp.exp(v1 - v0)
    g0 = 1.0 / (1.0 + e1)
    route = jnp.where(lane == 0, g0, jnp.where(lane == 1, e1 * g0, jnp.where(
        lane == 2, i0.astype(F32), jnp.where(lane == 3, i1.astype(F32), 0.0))))
    lg_ref[...] = route


def _router(x, g, mod, sh_idx, rows_per_mod, w_router_pad, *, tm, name, pool_rows, row_offset=0, prev=None):
    t, d = x.shape
    assert row_offset % tm == 0
    first = row_offset // tm
    if rows_per_mod is None:
        mod_row = lambda i: 0
    else:
        tiles_per_mod = rows_per_mod // tm
        mod_row = lambda i: i // tiles_per_mod
    in_specs = [pl.BlockSpec((tm, d), lambda i: (i, 0)),
                pl.BlockSpec((1, d), lambda i: (0, 0)),
                pl.BlockSpec((1, 1, d), lambda i: (mod_row(i), 0, sh_idx)),
                pl.BlockSpec((1, 1, d), lambda i: (mod_row(i), 0, sh_idx + 1)),
                pl.BlockSpec((d, ROUTER_PAD), lambda i: (0, 0))]
    args = [x, g.reshape(1, d), mod, mod, w_router_pad]
    aliases = {}
    if prev is not None:
        in_specs += [pl.BlockSpec(memory_space=pl.ANY)] * 2
        aliases = {len(args): 0, len(args) + 1: 1}
        args += list(prev)
    return pl.pallas_call(
        _router_kernel,
        grid=(t // tm,),
        in_specs=in_specs,
        out_specs=[pl.BlockSpec((tm, d), lambda i: (first + i, 0)),
                   pl.BlockSpec((tm, ROUTER_PAD), lambda i: (first + i, 0))],
        out_shape=[jax.ShapeDtypeStruct((pool_rows, d), BF16), jax.ShapeDtypeStruct((pool_rows, ROUTER_PAD), F32)],
        input_output_aliases=aliases,
        compiler_params=_cparams(("parallel",)),
        name=name,
    )(*args)


def _experts_kernel(be_ref, nb_ref, x_ref, wg_ref, wu_ref, wd_ref, *rest, nf, first_block):
    o_ref, acc = rest[-2:]
    blk = pl.program_id(0)
    f = pl.program_id(1)
    used = first_block + blk < nb_ref[0]
    rc = min(x_ref.shape[0], MOE_ROW_CHUNK)

    @pl.when((blk == 0) & (f == 0))
    def _():
        acc[...] = jnp.zeros_like(acc)

    @pl.when(used)
    def _():
        wg = wg_ref[0, 0].astype(BF16)
        wu = wu_ref[0, 0].astype(BF16)
        wd = wd_ref[0, 0].astype(BF16)
        n_chunks = x_ref.shape[0] // rc

        def gate_up(c):
            x = x_ref[c * rc:(c + 1) * rc, :]
            return _dot(x, wg), _dot(x, wu)

        nxt = gate_up(0)
        for c in range(n_chunks):
            g, u = nxt
            if c + 1 < n_chunks:
                nxt = gate_up(c + 1)
            part = _dot((g * _sigmoid(g) * u).astype(BF16), wd)
            rows = slice(c * rc, (c + 1) * rc)
            acc[rows, :] = part + jnp.where(f > 0, acc[rows, :], 0.0)

    @pl.when(f == nf - 1)
    def _():
        o_ref[...] = jnp.where(used, acc[...], 0.0).astype(o_ref.dtype)


def _experts(buf, block_expert, n_used, w_gu, w_down, layer, *, first_block, total_blocks, prev=None):
    rows, d = buf.shape
    nb = rows // MOE_ROWS
    two_f = w_gu.shape[3]
    fdim = two_f // 2
    fc = _pick(fdim, (MOE_FC, 256, 128))
    nf = fdim // fc
    in_specs = [pl.BlockSpec((MOE_ROWS, d), lambda i, f, be, nu: (i, 0)),
                pl.BlockSpec((1, 1, d, fc), lambda i, f, be, nu: (layer, be[first_block + i], 0, f)),
                pl.BlockSpec((1, 1, d, fc), lambda i, f, be, nu: (layer, be[first_block + i], 0, nf + f)),
                pl.BlockSpec((1, 1, fc, d), lambda i, f, be, nu: (layer, be[first_block + i], f, 0))]
    args = [block_expert, n_used, buf, w_gu, w_gu, w_down]
    aliases = {}
    if prev is not None:
        in_specs.append(pl.BlockSpec(memory_space=pl.ANY))
        aliases = {len(args): 0}
        args.append(prev)
    grid_spec = pltpu.PrefetchScalarGridSpec(
        num_scalar_prefetch=2,
        grid=(nb, nf),
        in_specs=in_specs,
        out_specs=pl.BlockSpec((MOE_ROWS, d), lambda i, f, be, nu: (first_block + i, 0)),
        scratch_shapes=[pltpu.VMEM((MOE_ROWS, d), F32)],
    )
    return pl.pallas_call(
        functools.partial(_experts_kernel, nf=nf, first_block=first_block),
        grid_spec=grid_spec,
        out_shape=jax.ShapeDtypeStruct((total_blocks * MOE_ROWS, d), BF16),
        input_output_aliases=aliases,
        compiler_params=_cparams(("parallel", "arbitrary")),
        name="moe_experts",
    )(*args)


def _moe(a_all, top_idx, w_gu, w_down, layer):
    t, d = a_all.shape
    e = w_gu.shape[1]
    onehot = jnp.sum((top_idx[..., None] == jnp.arange(e)[None, None, :]).astype(jnp.int32), axis=1)
    counts = jnp.sum(onehot, axis=0)
    rank = jnp.cumsum(onehot, axis=0) - onehot
    padded = (counts + MOE_ROWS - 1) // MOE_ROWS * MOE_ROWS
    pend = jnp.cumsum(padded)
    pstart = pend - padded
    dest = pstart[top_idx] + jnp.take_along_axis(rank, top_idx, axis=1)
    nb = -(-(t * TOP_K) // MOE_ROWS) + e
    block_expert = jnp.minimum(jnp.searchsorted(pend, jnp.arange(nb) * MOE_ROWS, side="right"), e - 1).astype(jnp.int32)
    n_used = (pend[-1] // MOE_ROWS).astype(jnp.int32).reshape(1)
    by_expert = (jnp.argsort(top_idx.reshape(-1), stable=True) // TOP_K).astype(jnp.int32)
    by_expert = jnp.concatenate([by_expert, jnp.zeros((nb * MOE_ROWS + MOE_ROWS - t * TOP_K,), jnp.int32)])
    start = jnp.cumsum(counts) - counts
    first = jnp.arange(nb) * MOE_ROWS - (pstart - start)[block_expert]
    row_src = jax.vmap(lambda o: lax.dynamic_slice(by_expert, (o,), (MOE_ROWS,)))(first).reshape(-1)
    n_groups = _pick(nb, (MOE_GROUPS, 2, 1))
    per = nb // n_groups
    yb = None
    for k in range(n_groups):
        buf = a_all[row_src[k * per * MOE_ROWS:(k + 1) * per * MOE_ROWS]]
        yb = _experts(buf, block_expert, n_used, w_gu, w_down, layer, first_block=k * per, total_blocks=nb, prev=yb)
    return yb, dest


def _rope_angles(pos, dim):
    inv = np.float32(ROPE_BASE) ** (-np.arange(0, dim, 2, dtype=np.float32) / np.float32(dim))
    ang = pos.astype(np.float32)[:, None] * inv[None, :]
    return np.concatenate([ang, ang], axis=-1)


def _signed_sin(sin, shift):
    low = (np.arange(sin.shape[1]) % (2 * shift)) < shift
    zero = np.float32(0.0)
    return jnp.asarray(np.where(low[None, :], -sin, zero)), jnp.asarray(np.where(low[None, :], zero, sin))


def _axial_tables(n_tokens, head_dim):
    rows = n_tokens // GRID_W
    row = np.repeat(np.arange(rows, dtype=np.int32), GRID_W)
    col = np.tile(np.arange(GRID_W, dtype=np.int32), rows)
    half = head_dim // 2
    ang = np.concatenate([_rope_angles(row, half), _rope_angles(col, half)], axis=-1)
    ang = np.tile(ang, (1, LANES // head_dim))
    sa, sb = _signed_sin(np.sin(ang), half // 2)
    return jnp.asarray(np.cos(ang)), sa, sb, half // 2


def _ret_tables(n_tokens):
    ang = _rope_angles(np.arange(n_tokens), RET_KEY)
    sa, sb = _signed_sin(np.sin(ang), RET_KEY // 2)
    return jnp.asarray(np.cos(ang)), sa, sb, RET_KEY // 2


def kernel(x, c, ctx, c_ctx, mod_w, mod_b, norm_g, da_w_in, da_w_out, da_lambda, da_subln_g, wa_w_in, wa_w_out,
           wa_sink, ret_w_in, ret_w_out, ret_decay_logit, ffn_w_gu, ffn_w_down, moe_router, moe_w_gu, moe_w_down):
    b, l, d = x.shape
    lc = ctx.shape[1]
    depth = mod_w.shape[0]
    t, tc = b * l, b * lc
    tm = _pick(l, (1024, 512, 256, 128))
    tmc = _pick(tc, (1024, 512, 256, 128))

    n_cond = -(-(b + 1) // 8) * 8
    cs = jnp.zeros((n_cond, d), F32).at[:b].set(c).at[b].set(c_ctx)
    mod_all = _mod_vectors(cs, mod_w, mod_b)

    xs = x.reshape(t, d)
    hs = ctx.reshape(tc, d)
    mixer_count = [0] * N_MIXERS
    i_dense = 0
    i_moe = 0
    for layer in range(depth):
        last = layer == depth - 1
        mod = mod_all[layer, :b].reshape(b, 1, 6 * d)
        mod_c = mod_all[layer, b:b + 1].reshape(1, 1, 6 * d)
        ng = norm_g[layer]
        kind = layer % N_MIXERS
        jm = mixer_count[kind]
        mixer_count[kind] += 1

        if kind == 0:
            lam_init = 0.8 - 0.6 * math.exp(-0.3 * layer)
            w_in = da_w_in[jm].astype(BF16)
            cos, sa, sb, shift = _axial_tables(l, DA_HEAD)
            qkv = _norm_proj(xs, ng[0], mod, 0, l, w_in, tm=tm, tn=3 * d, rot=(cos, sa, sb, 2 * d, shift, l),
                             name="da_in_proj").reshape(b, l, 3 * d)
            qkv_c = _norm_proj(hs, ng[0], mod_c, 0, None, w_in, tm=tmc, tn=3 * d, name="da_in_proj_ctx").reshape(b, lc, 3 * d)
            lp = da_lambda[jm].astype(F32)
            lam = jnp.exp(jnp.sum(lp[0] * lp[1])) - jnp.exp(jnp.sum(lp[2] * lp[3])) + lam_init
            o = _diff_attention(qkv, [qkv_c, qkv], lam, da_subln_g[jm], lam_init, tq=_pick(l, (DA_TQ, 128)), name="diff_attn")
            w_out = da_w_out[jm].astype(BF16)
            xs = _resid(xs, ng[1], mod, 2, l, tm=tm, a=o.reshape(t, d), w=w_out, name="da_out_proj")
            if not last:
                oc = _diff_attention(qkv_c, [qkv_c], lam, da_subln_g[jm], lam_init, tq=_pick(lc, (DA_TQ, 128)), name="diff_attn_ctx")
                hs = _resid(hs, ng[1], mod_c, 2, None, tm=tmc, a=oc.reshape(tc, d), w=w_out, name="da_out_proj_ctx")
        elif kind == 1:
            w_in = wa_w_in[jm].astype(BF16)
            n_qkv = w_in.shape[1]
            cos, sa, sb, shift = _axial_tables(l, WA_HEAD)
            n_rot = (WA_Q_HEADS + WA_KV_HEADS) * WA_HEAD
            qkv = _norm_proj(xs, ng[0], mod, 0, l, w_in, tm=tm, tn=n_qkv, rot=(cos, sa, sb, n_rot, shift, l),
                             name="wa_in_proj").reshape(b, l, n_qkv)
            qkv_c = _norm_proj(hs, ng[0], mod_c, 0, None, w_in, tm=tmc, tn=n_qkv, name="wa_in_proj_ctx").reshape(b, lc, n_qkv)
            o = _window_attention(qkv, qkv_c, wa_sink[jm], windowed=True, name="window_attn")
            w_out = wa_w_out[jm].astype(BF16)
            xs = _resid(xs, ng[1], mod, 2, l, tm=tm, a=o.reshape(t, d), w=w_out, name="wa_out_proj")
            if not last:
                oc = _window_attention(qkv_c, qkv_c, wa_sink[jm], windowed=False, name="window_attn_ctx")
                hs = _resid(hs, ng[1], mod_c, 2, None, tm=tmc, a=oc.reshape(tc, d), w=w_out, name="wa_out_proj_ctx")
        else:
            w_in = ret_w_in[jm].astype(BF16)
            n_qkv = w_in.shape[1]
            cos, sa, sb, shift = _ret_tables(l)
            n_rot = 2 * RET_HEADS * RET_KEY
            qkv = _norm_proj(xs, ng[0], mod, 0, l, w_in, tm=tm, tn=n_qkv // 2, rot=(cos, sa, sb, n_rot, shift, l),
                             name="ret_in_proj")
            qkv_c = _norm_proj(hs, ng[0], mod_c, 0, None, w_in, tm=_pick(tc, (512, 256, 128)), tn=n_qkv // 2,
                               name="ret_in_proj_ctx")
            log_g = jax.nn.log_sigmoid(ret_decay_logit[jm].astype(F32))
            ocf, olf, ocb, olb = _retention(qkv_c.reshape(b, lc, n_qkv), qkv.reshape(b, l, n_qkv), log_g)
            w_out = ret_w_out[jm].astype(BF16)
            vw = RET_HEADS * RET_VAL
            xs = _resid(xs, ng[1], mod, 2, l, tm=_pick(l, (512, 256, 128)), w=w_out,
                        ret=(olf.reshape(t, vw), olb.reshape(t, vw), qkv), name="ret_out_proj")
            if not last:
                hs = _resid(hs, ng[1], mod_c, 2, None, tm=_pick(tc, (512, 256, 128)), w=w_out,
                            ret=(ocf.reshape(tc, vw), ocb.reshape(tc, vw), qkv_c), name="ret_out_proj_ctx")

        if layer % 2 == 0:
            w_gu = ffn_w_gu[i_dense].astype(BF16)
            w_dn = ffn_w_down[i_dense].astype(BF16)
            i_dense += 1
            f = w_dn.shape[0]
            w_g, w_u = w_gu[:, :f], w_gu[:, f:]
            tn = f
            tf = _pick(l, (512, 256, 128))
            act = _norm_proj(xs, ng[2], mod, 3, l, (w_g, w_u), tm=tf, tn=tn, name="ffn_up")
            xs = _resid(xs, ng[3], mod, 5, l, tm=tf, a=act, w=w_dn, name="ffn_down")
            if not last:
                tfc = _pick(tc, (512, 256, 128))
                act_c = _norm_proj(hs, ng[2], mod_c, 3, None, (w_g, w_u), tm=tfc, tn=tn, name="ffn_up_ctx")
                hs = _resid(hs, ng[3], mod_c, 5, None, tm=tfc, a=act_c, w=w_dn, name="ffn_down_ctx")
        else:
            w_r = jnp.zeros((d, ROUTER_PAD), F32).at[:, :N_EXPERTS].set(moe_router[i_moe])
            moe_layer = i_moe
            i_moe += 1
            pool = t if last else t + tc
            a_all, route = _router(xs, ng[2], mod, 3, l, w_r, tm=tm, name="moe_router", pool_rows=pool)
            if not last:
                a_all, route = _router(hs, ng[2], mod_c, 3, None, w_r, tm=tmc, name="moe_router_ctx", pool_rows=pool,
                                       row_offset=t, prev=(a_all, route))
            route_x, route_c = route[:t], route[t:]
            yb, dest = _moe(a_all, route[:, 2:4].astype(jnp.int32), moe_w_gu, moe_w_down, moe_layer)
            xs = _resid(xs, ng[3], mod, 5, l, tm=tm, mix=(yb[dest[:t, 0]], yb[dest[:t, 1]], route_x), name="moe_combine")
            if not last:
                hs = _resid(hs, ng[3], mod_c, 5, None, tm=tmc, mix=(yb[dest[t:, 0]], yb[dest[t:, 1]], route_c),
                            name="moe_combine_ctx")
    return xs.reshape(b, l, d)
```

```python
import functools
import math

import jax
import jax.numpy as jnp
import numpy as np
from jax import lax
from jax.experimental import pallas as pl
from jax.experimental.pallas import tpu as pltpu

F32 = jnp.float32
BF16 = jnp.bfloat16

D_MODEL = 1024
N_MIXERS = 3
NORM_EPS = 1e-6
ROPE_BASE = 10000.0
GRID_W = 64
DA_HEAD = 64
DA_HEADS = D_MODEL // (2 * DA_HEAD)
WA_HEAD = 64
WA_Q_HEADS = D_MODEL // WA_HEAD
WA_KV_HEADS = WA_Q_HEADS // 4
WINDOW = 128
RET_KEY = 256
RET_HEADS = D_MODEL // RET_KEY
RET_VAL = 2 * RET_KEY
RET_CHUNK = 128
N_EXPERTS = 8
TOP_K = 2

LANES = 128
V7X_VMEM_LIMIT = 56 * 1024 * 1024
ROUTER_PAD = LANES
MOE_ROWS = 1024
MOE_FC = 512
MOE_ROW_CHUNK = 256
MOE_GROUPS = 4
PROJ_COLS = 512


def _cparams(sem):
    return pltpu.CompilerParams(dimension_semantics=sem, vmem_limit_bytes=V7X_VMEM_LIMIT)


def _pick(n, prefs):
    for p in prefs:
        if n % p == 0:
            return p
    return n


def _sigmoid(x):
    return 1.0 / (1.0 + jnp.exp(-x))


def _rms(y):
    return y * lax.rsqrt(jnp.mean(y * y, axis=-1, keepdims=True) + NORM_EPS)


def _dot(a, b):
    return jnp.dot(a, b, preferred_element_type=F32)


def _dot_nt(a, b):
    return lax.dot_general(a, b, (((1,), (1,)), ((), ())), preferred_element_type=F32)


def _dot_tn(a, b):
    return lax.dot_general(a, b, (((0,), (0,)), ((), ())), preferred_element_type=F32)


def _mod_kernel(c_ref, w_ref, b_ref, o_ref):
    cs = c_ref[...]
    s = cs * _sigmoid(cs)
    w = w_ref[0]
    s_hi = s.astype(BF16)
    s_lo = (s - s_hi.astype(F32)).astype(BF16)
    w_hi = w.astype(BF16)
    w_lo = (w - w_hi.astype(F32)).astype(BF16)
    o_ref[0] = _dot(s_hi, w_hi) + _dot(s_lo, w_hi) + _dot(s_hi, w_lo) + b_ref[0]


def _mod_vectors(cs, mod_w, mod_b):
    depth, d, n = mod_w.shape
    r = cs.shape[0]
    tn = _pick(n, (1024, 512, 256, 128))
    return pl.pallas_call(
        _mod_kernel,
        grid=(depth, n // tn),
        in_specs=[
            pl.BlockSpec((r, d), lambda l, j: (0, 0)),
            pl.BlockSpec((1, d, tn), lambda l, j: (l, 0, j)),
            pl.BlockSpec((1, 1, tn), lambda l, j: (l, 0, j)),
        ],
        out_specs=pl.BlockSpec((1, r, tn), lambda l, j: (l, 0, j)),
        out_shape=jax.ShapeDtypeStruct((depth, r, n), F32),
        compiler_params=_cparams(("parallel", "parallel")),
        name="mod_vectors",
    )(cs, mod_w, mod_b.reshape(depth, 1, n))


def _modulated(x_ref, g_ref, sh_ref, sc_ref):
    x = x_ref[...]
    return _rms(x) * g_ref[...] * (1.0 + sc_ref[0]) + sh_ref[0]


def _rotate(acc, cos, sa, sb, shift, cw):
    outs = []
    for c in range(acc.shape[1] // cw):
        y = acc[:, c * cw:(c + 1) * cw]
        up = pltpu.roll(y, cw - shift, 1)
        dn = pltpu.roll(y, shift, 1)
        outs.append(y * cos + up * sa + dn * sb)
    return outs[0] if len(outs) == 1 else jnp.concatenate(outs, axis=1)


def _col_chunks(width, cuts):
    edges = sorted({0, width} | set(range(PROJ_COLS, width, PROJ_COLS)) | {c for c in cuts if 0 < c < width})
    return list(zip(edges[:-1], edges[1:]))


def _proj_kernel(*refs, mode, n_rot_cols, shift, cw, n_col_tiles):
    if mode == "swiglu":
        x_ref, g_ref, sh_ref, sc_ref, wg_ref, wu_ref, o_ref, a_scr = refs
    elif mode == "rot":
        x_ref, g_ref, sh_ref, sc_ref, w_ref, cos_ref, sa_ref, sb_ref, o_ref, a_scr = refs
    else:
        x_ref, g_ref, sh_ref, sc_ref, w_ref, o_ref, a_scr = refs
    j = pl.program_id(1)
    tn = o_ref.shape[1]

    @pl.when(j == 0)
    def _():
        a_scr[...] = _modulated(x_ref, g_ref, sh_ref, sc_ref).astype(BF16)

    def tile(jt):
        a = a_scr[...]
        for lo, hi in _col_chunks(tn, [n_rot_cols - jt * tn]):
            if mode == "swiglu":
                g = _dot(a, wg_ref[:, lo:hi])
                res = g * _sigmoid(g) * _dot(a, wu_ref[:, lo:hi])
            else:
                res = _dot(a, w_ref[:, lo:hi])
                if mode == "rot" and jt * tn + hi <= n_rot_cols:
                    res = _rotate(res, cos_ref[...], sa_ref[...], sb_ref[...], shift, cw)
            o_ref[:, lo:hi] = res.astype(o_ref.dtype)

    if mode != "rot":
        tile(0)
    else:
        n_rot_tiles = -(-n_rot_cols // tn)
        for jt in range(min(n_rot_tiles + 1, n_col_tiles)):
            @pl.when((j >= jt) if jt == n_rot_tiles else (j == jt))
            def _(jt=jt):
                tile(jt)


def _norm_proj(x, g, mod, sh_idx, rows_per_mod, weights, *, tm, tn, rot=None, name):
    t, d = x.shape
    swiglu = isinstance(weights, tuple)
    n = weights[0].shape[1] if swiglu else weights.shape[1]
    assert t % tm == 0 and n % tn == 0
    if rows_per_mod is None:
        mod_row = lambda i: 0
    else:
        assert rows_per_mod % tm == 0
        tiles_per_mod = rows_per_mod // tm
        mod_row = lambda i: i // tiles_per_mod
    in_specs = [
        pl.BlockSpec((tm, d), lambda i, j: (i, 0)),
        pl.BlockSpec((1, d), lambda i, j: (0, 0)),
        pl.BlockSpec((1, 1, d), lambda i, j: (mod_row(i), 0, sh_idx)),
        pl.BlockSpec((1, 1, d), lambda i, j: (mod_row(i), 0, sh_idx + 1)),
    ]
    args = [x, g.reshape(1, d), mod, mod]
    w_spec = pl.BlockSpec((d, tn), lambda i, j: (0, j))
    n_rot_cols, shift, cw = 0, 0, 0
    if swiglu:
        mode = "swiglu"
        in_specs += [w_spec, w_spec]
        args += list(weights)
    elif rot is not None:
        mode = "rot"
        cos, sa, sb, n_rot_cols, shift, seq_len = rot
        cw = cos.shape[1]
        assert n_rot_cols % cw == 0 and PROJ_COLS % cw == 0 and seq_len % tm == 0
        tiles_per_seq = seq_len // tm
        t_spec = pl.BlockSpec((tm, cw), lambda i, j: (i % tiles_per_seq, 0))
        in_specs += [w_spec, t_spec, t_spec, t_spec]
        args += [weights, cos, sa, sb]
    else:
        mode = "plain"
        in_specs += [w_spec]
        args += [weights]
    return pl.pallas_call(
        functools.partial(_proj_kernel, mode=mode, n_rot_cols=n_rot_cols, shift=shift, cw=cw, n_col_tiles=n // tn),
        grid=(t // tm, n // tn),
        in_specs=in_specs,
        out_specs=pl.BlockSpec((tm, tn), lambda i, j: (i, j)),
        out_shape=jax.ShapeDtypeStruct((t, n), BF16),
        scratch_shapes=[pltpu.VMEM((tm, d), BF16)],
        compiler_params=_cparams(("parallel", "arbitrary")),
        name=name,
    )(*args)


def _resid_kernel(*refs, mode):
    if mode == "ret":
        of_ref, ob_ref, gt_ref, w_ref, x_ref, gate_ref, g_ref, o_ref = refs
        o = of_ref[...].astype(F32) + ob_ref[...].astype(F32)
        parts = []
        for h in range(RET_HEADS):
            parts.append(_rms(o[:, h * RET_VAL:(h + 1) * RET_VAL]))
        o = jnp.concatenate(parts, axis=1)
        gt = gt_ref[...].astype(F32)
        y = _dot((gt * _sigmoid(gt) * o).astype(BF16), w_ref[...])
    elif mode == "matmul":
        a_ref, w_ref, x_ref, gate_ref, g_ref, o_ref = refs
        y = _dot(a_ref[...], w_ref[...])
    else:
        y0_ref, y1_ref, route_ref, x_ref, gate_ref, g_ref, o_ref = refs
        route = route_ref[...]
        y = y0_ref[...].astype(F32) * route[:, 0:1] + y1_ref[...].astype(F32) * route[:, 1:2]
    o_ref[...] = x_ref[...] + gate_ref[0] * (_rms(y) * g_ref[...])


def _resid(x, g, mod, gate_idx, rows_per_mod, *, tm, name, a=None, w=None, mix=None, ret=None):
    t, d = x.shape
    assert t % tm == 0
    if rows_per_mod is None:
        mod_row = lambda i: 0
    else:
        assert rows_per_mod % tm == 0
        tiles_per_mod = rows_per_mod // tm
        mod_row = lambda i: i // tiles_per_mod
    row = lambda width: pl.BlockSpec((tm, width), lambda i: (i, 0))
    tail_specs = [row(d), pl.BlockSpec((1, 1, d), lambda i: (mod_row(i), 0, gate_idx)),
                  pl.BlockSpec((1, d), lambda i: (0, 0))]
    tail_args = [x, mod, g.reshape(1, d)]
    if ret is not None:
        mode = "ret"
        o_f, o_b, qkvg = ret
        vw = RET_HEADS * RET_VAL
        gate_col = qkvg.shape[1] // vw - 1
        in_specs = [row(vw), row(vw), pl.BlockSpec((tm, vw), lambda i: (i, gate_col)),
                    pl.BlockSpec(w.shape, lambda i: (0, 0))]
        args = [o_f, o_b, qkvg, w]
    elif a is not None:
        mode = "matmul"
        in_specs = [row(a.shape[1]), pl.BlockSpec(w.shape, lambda i: (0, 0))]
        args = [a, w]
    else:
        mode = "mix"
        in_specs = [row(d), row(d), row(ROUTER_PAD)]
        args = list(mix)
    return pl.pallas_call(
        functools.partial(_resid_kernel, mode=mode),
        grid=(t // tm,),
        in_specs=in_specs + tail_specs,
        out_specs=row(d),
        out_shape=jax.ShapeDtypeStruct((t, d), F32),
        compiler_params=_cparams(("parallel",)),
        name=name,
    )(*args, *tail_args)


DA_SUM_ROWS = 16
DA_TQ = 256
DA_EXP_ROWS = 32


def _da_kernel(*refs, n_seg, post_scale, n_tiles, nq):
    lam_ref, q_ref = refs[0], refs[1]
    k_refs = refs[2:2 + n_seg]
    v_refs = refs[2 + n_seg:2 + 2 * n_seg]
    sg_ref, o_ref, k_scr, vt_scr, s_a, s_b, p_a, p_b, mx_a, mx_b = refs[2 + 2 * n_seg:]
    hw = 2 * DA_HEAD
    n_keys = vt_scr.shape[1]
    tq = q_ref.shape[1]
    g = pl.program_id(0)
    t_qk = jnp.minimum(g, n_tiles - 1)
    t_pv = jnp.clip(g - 2, 0, n_tiles - 1)

    @pl.when(g == 0)
    def _():
        for buf in (s_a, s_b, p_a, p_b, mx_a, mx_b):
            buf[...] = jnp.zeros_like(buf)

    @pl.when(lax.rem(t_qk, nq) == 0)
    def _():
        off = 0
        for kr in k_refs:
            n = kr.shape[1]
            k_scr[off:off + n, :] = kr[0]
            off += n

    @pl.when(lax.rem(t_pv, nq) == 0)
    def _():
        off = 0
        for vr in v_refs:
            n = vr.shape[1]
            vt_scr[0:hw, off:off + n] = vr[0].astype(F32).T.astype(BF16)
            off += n
        row = lax.broadcasted_iota(jnp.int32, (DA_SUM_ROWS, n_keys), 0)
        vt_scr[hw:hw + DA_SUM_ROWS, :] = jnp.where(row == 0, 1.0, 0.0).astype(BF16)

    lane = lax.broadcasted_iota(jnp.int32, (1, hw), 1)
    qscale = DA_HEAD ** -0.5 * math.log2(math.e)

    def stages(s_new, mx_new, s_old, mx_old, p_old, p_older):
        accs = []
        for m in range(2):
            accs.append(_dot(vt_scr[...], p_older[m]))
        q = q_ref[0].astype(F32)
        for m in range(2):
            sel = (lane < DA_HEAD) if m == 0 else (lane >= DA_HEAD)
            qm = jnp.where(sel, q * qscale, 0.0).astype(BF16)
            s_t = _dot_nt(k_scr[...], qm)
            s_new[m] = s_t
            part = [None] * 4
            for r in range(n_keys // 8):
                tile = s_t[r * 8:(r + 1) * 8, :]
                j = r % len(part)
                part[j] = tile if part[j] is None else jnp.maximum(part[j], tile)
            mx_new[m] = functools.reduce(jnp.maximum, part)
            mx8 = jnp.broadcast_to(jnp.max(mx_old[m], axis=0, keepdims=True), (8, tq))
            for c in range(n_keys // DA_EXP_ROWS):
                rows = slice(c * DA_EXP_ROWS, (c + 1) * DA_EXP_ROWS)
                zero = jnp.minimum(jnp.abs(s_t[c * DA_EXP_ROWS:c * DA_EXP_ROWS + 8, :]), 0.0)
                d = s_old[m, rows, :].reshape(DA_EXP_ROWS // 8, 8, tq) - (mx8 + zero)[None]
                p_old[m, rows, :] = jnp.exp2(d.reshape(DA_EXP_ROWS, tq).astype(BF16))
        a0, a1 = accs
        o_t = a0[:hw] * (1.0 / a0[hw:hw + 1]) - a1[:hw] * (lam_ref[0, 0] / a1[hw:hw + 1])
        o_t = o_t * lax.rsqrt(jnp.mean(o_t * o_t, axis=0, keepdims=True) + NORM_EPS) * (sg_ref[...] * post_scale)
        o_ref[0] = o_t.T.astype(o_ref.dtype)

    @pl.when(lax.rem(g, 2) == 0)
    def _():
        stages(s_a, mx_a, s_b, mx_b, p_b, p_a)

    @pl.when(lax.rem(g, 2) == 1)
    def _():
        stages(s_b, mx_b, s_a, mx_a, p_a, p_b)


def _diff_attention(q_arr, kv_arrs, lam, subln_g, lam_init, *, tq, name):
    b, lq, _ = q_arr.shape
    hw = 2 * DA_HEAD
    n_seg = len(kv_arrs)
    n_keys = sum(kv.shape[1] for kv in kv_arrs)
    assert lq % tq == 0 and n_keys % DA_EXP_ROWS == 0
    nq = lq // tq
    n_tiles = b * DA_HEADS * nq

    def tile(t):
        return t // (nq * DA_HEADS), lax.rem(t // nq, DA_HEADS), lax.rem(t, nq)

    def qk_tile(g):
        return tile(jnp.minimum(g, n_tiles - 1))

    def pv_tile(g):
        return tile(jnp.clip(g - 2, 0, n_tiles - 1))

    in_specs = [pl.BlockSpec(memory_space=pltpu.SMEM),
                pl.BlockSpec((1, tq, hw), lambda g: (qk_tile(g)[0], qk_tile(g)[2], qk_tile(g)[1]))]
    in_specs += [pl.BlockSpec((1, kv.shape[1], hw), lambda g: (qk_tile(g)[0], 0, DA_HEADS + qk_tile(g)[1]))
                 for kv in kv_arrs]
    in_specs += [pl.BlockSpec((1, kv.shape[1], hw), lambda g: (pv_tile(g)[0], 0, 2 * DA_HEADS + pv_tile(g)[1]))
                 for kv in kv_arrs]
    in_specs += [pl.BlockSpec((hw, 1), lambda g: (0, 0))]
    s_buf = pltpu.VMEM((2, n_keys, tq), F32)
    p_buf = pltpu.VMEM((2, n_keys, tq), BF16)
    mx_buf = pltpu.VMEM((2, 8, tq), F32)
    return pl.pallas_call(
        functools.partial(_da_kernel, n_seg=n_seg, post_scale=1.0 - lam_init, n_tiles=n_tiles, nq=nq),
        grid=(n_tiles + 2,),
        in_specs=in_specs,
        out_specs=pl.BlockSpec((1, tq, hw), lambda g: (pv_tile(g)[0], pv_tile(g)[2], pv_tile(g)[1])),
        out_shape=jax.ShapeDtypeStruct((b, lq, D_MODEL), BF16),
        scratch_shapes=[pltpu.VMEM((n_keys, hw), BF16), pltpu.VMEM((hw + DA_SUM_ROWS, n_keys), BF16),
                        s_buf, s_buf, p_buf, p_buf, mx_buf, mx_buf],
        compiler_params=_cparams(("arbitrary",)),
        name=name,
    )(lam.reshape(1, 1), q_arr, *kv_arrs, *kv_arrs, subln_g.reshape(hw, 1))


WA_SUM_ROWS = 16


def _wa_kernel(*refs, windowed, seq_len):
    if windowed:
        sink_ref, q_ref, kc_ref, vc_ref, kp_ref, kq_ref, kn_ref, vp_ref, vq_ref, vn_ref, o_ref = refs
        k_all = jnp.concatenate([kc_ref[0], kp_ref[0], kq_ref[0], kn_ref[0]], axis=0)
        v_all = jnp.concatenate([vc_ref[0], vp_ref[0], vq_ref[0], vn_ref[0]], axis=0)
    else:
        sink_ref, q_ref, kc_ref, vc_ref, o_ref = refs
        k_all, v_all = kc_ref[0], vc_ref[0]
    tq = q_ref.shape[1]
    n_keys = k_all.shape[0]
    n_ctx = kc_ref.shape[1]
    n_cols = 2 * tq
    i = pl.program_id(1)
    col = lax.broadcasted_iota(jnp.int32, (1, n_cols), 1)
    first = col < tq
    if windowed:
        row = lax.broadcasted_iota(jnp.int32, (n_keys - n_ctx, n_cols), 0)
        qpos = i * tq + jnp.where(first, col, col - tq)
        kpos = (i - 1) * tq + row
        ok = (jnp.abs(qpos - kpos) <= WINDOW) & (kpos >= 0) & (kpos < seq_len)
    lane = lax.broadcasted_iota(jnp.int32, (1, LANES), 1)
    lo = lane < WA_HEAD
    log2e = math.log2(math.e)
    qscale = WA_HEAD ** -0.5 * log2e
    rep = WA_Q_HEADS // WA_KV_HEADS
    v_t = v_all.astype(F32).T
    sum_rows = jnp.where(lax.broadcasted_iota(jnp.int32, (WA_SUM_ROWS, n_keys), 0) == 0, 1.0, 0.0)
    def scores(g, half):
        kg = k_all[:, (g // 2) * LANES:(g // 2 + 1) * LANES]
        zero = jnp.zeros_like(kg)
        k_same = jnp.where(lo, kg, zero) if g % 2 == 0 else jnp.where(lo, zero, kg)
        if (g % 2 == 0) == (half == 0):
            k_half = k_same
        else:
            k_half = pltpu.roll(k_same.astype(F32), WA_HEAD, 1).astype(BF16)
        chunks = [q_ref[0, :, (g * (rep // 2) + pair) * LANES:(g * (rep // 2) + pair + 1) * LANES]
                  for pair in range(rep // 2)]
        qs = (jnp.concatenate(chunks, axis=0).astype(F32) * qscale).astype(BF16)
        return _dot_nt(k_half, qs)

    def weighted(g, half, s_t):
        if windowed:
            s_t = jnp.concatenate([s_t[:n_ctx], jnp.where(ok, s_t[n_ctx:], -1e30)], axis=0)
        snk = jnp.where(first, sink_ref[g * rep + half], sink_ref[g * rep + 2 + half]) * log2e
        mx = jnp.maximum(jnp.max(s_t, axis=0, keepdims=True), snk)
        vt_g = jnp.concatenate([v_t[g * WA_HEAD:(g + 1) * WA_HEAD, :], sum_rows], axis=0).astype(BF16)
        acc = _dot(vt_g, jnp.exp2((s_t - mx).astype(BF16)))
        l = acc[WA_HEAD:WA_HEAD + 1] + jnp.exp2(snk - mx)
        return acc[:WA_HEAD] * (1.0 / l)

    units = [(g, half) for g in range(WA_KV_HEADS) for half in range(2)]
    outs = []
    nxt = scores(*units[0])
    for u, (g, half) in enumerate(units):
        cur = nxt
        if u + 1 < len(units):
            nxt = scores(*units[u + 1])
        outs.append(weighted(g, half, cur))
        if half == 1:
            for pair in range(rep // 2):
                cols = slice(pair * tq, (pair + 1) * tq)
                o_t = jnp.concatenate([outs[-2][:, cols], outs[-1][:, cols]], axis=0)
                chunk = g * (rep // 2) + pair
                o_ref[0, :, chunk * LANES:(chunk + 1) * LANES] = o_t.T.astype(o_ref.dtype)


def _window_attention(q_arr, ctx_arr, sink, *, windowed, name):
    b, lq, _ = q_arr.shape
    n_ctx = ctx_arr.shape[1]
    qw = WA_Q_HEADS * WA_HEAD
    kw = WA_KV_HEADS * WA_HEAD
    k_col, v_col = qw // kw, qw // kw + 1
    tq = 128 if windowed else lq
    nq = lq // tq
    in_specs = [pl.BlockSpec(memory_space=pltpu.SMEM),
                pl.BlockSpec((1, tq, qw), lambda bi, i: (bi, i, 0)),
                pl.BlockSpec((1, n_ctx, kw), lambda bi, i: (bi, 0, k_col)),
                pl.BlockSpec((1, n_ctx, kw), lambda bi, i: (bi, 0, v_col))]
    args = [sink, q_arr, ctx_arr, ctx_arr]
    if windowed:
        assert tq == WINDOW
        for col in (k_col, v_col):
            in_specs += [pl.BlockSpec((1, tq, kw), lambda bi, i, col=col: (bi, jnp.maximum(i - 1, 0), col)),
                         pl.BlockSpec((1, tq, kw), lambda bi, i, col=col: (bi, i, col)),
                         pl.BlockSpec((1, tq, kw), lambda bi, i, col=col: (bi, jnp.minimum(i + 1, nq - 1), col))]
            args += [q_arr, q_arr, q_arr]
    return pl.pallas_call(
        functools.partial(_wa_kernel, windowed=windowed, seq_len=lq),
        grid=(b, nq),
        in_specs=in_specs,
        out_specs=pl.BlockSpec((1, tq, qw), lambda bi, i: (bi, i, 0)),
        out_shape=jax.ShapeDtypeStruct((b, lq, qw), BF16),
        compiler_params=_cparams(("parallel", "parallel")),
        name=name,
    )(*args)


def _ret_kernel(cd_ref, qcf, kcf, vcf, qlf, klf, vlf, qcb, kcb, vcb, qlb, klb, vlb, dtab_ref, qd_ref, kd_ref,
                ocf, olf, ocb, olb, s_scr, *, n_ctx_chunks):
    j = pl.program_id(1)
    is_ctx = j < n_ctx_chunks

    @pl.when(j == 0)
    def _():
        s_scr[...] = jnp.zeros_like(s_scr)

    kscale = jnp.asarray(RET_KEY ** -0.5, BF16)
    dirs = ((qcf, kcf, vcf, qlf, klf, vlf), (qcb, kcb, vcb, qlb, klb, vlb))
    qkv = []
    for qc, kc, vc, ql, kl, vl in dirs:
        qkv.append((jnp.where(is_ctx, qc[0], ql[0]), jnp.where(is_ctx, kc[0], kl[0]) * kscale,
                    jnp.where(is_ctx, vc[0], vl[0])))

    def products(d, h):
        q, k, v = qkv[d]
        qh = q[:, h * RET_KEY:(h + 1) * RET_KEY]
        kh = k[:, h * RET_KEY:(h + 1) * RET_KEY]
        vh = v[:, h * RET_VAL:(h + 1) * RET_VAL]
        state = s_scr[d, h]
        kdec = (kh.astype(F32) * kd_ref[d, h]).astype(BF16)
        return _dot_nt(qh, kh), _dot(qh, state.astype(BF16)), _dot_tn(kdec, vh), state, vh

    def finish(d, h, prod):
        qk, q_state, k_v, state, vh = prod
        s_scr[d, h] = state * cd_ref[d, h] + k_v
        return _dot((qk * dtab_ref[d, h]).astype(BF16), vh) + q_state * qd_ref[d, h]

    units = [(d, h) for d in range(2) for h in range(RET_HEADS)]
    outs = []
    nxt = products(*units[0])
    for u, (d, h) in enumerate(units):
        cur = nxt
        if u + 1 < len(units):
            nxt = products(*units[u + 1])
        outs.append(finish(d, h, cur))
    o_f = jnp.concatenate(outs[:RET_HEADS], axis=1).astype(ocf.dtype)
    o_b = jnp.concatenate(outs[RET_HEADS:], axis=1).astype(ocb.dtype)

    @pl.when(is_ctx)
    def _():
        ocf[0] = o_f
        ocb[0] = o_b

    @pl.when(jnp.logical_not(is_ctx))
    def _():
        olf[0] = o_f
        olb[0] = o_b


def _retention(ctx_arr, lat_arr, log_g):
    b, lc, _ = ctx_arr.shape
    l = lat_arr.shape[1]
    c = RET_CHUNK
    nc, nl = lc // c, l // c
    qw = RET_HEADS * RET_KEY
    vw = RET_HEADS * RET_VAL
    idx = jnp.arange(c, dtype=F32)
    rel = idx[:, None] - idx[None, :]
    lg = log_g[:, :, None, None]
    keep_f = (rel >= 0)[None]
    keep_b = (rel < 0)[None]
    d_f = jnp.where(keep_f, jnp.exp(jnp.where(keep_f, rel[None] * lg[0], 0.0)), 0.0)
    d_b = jnp.where(keep_b, jnp.exp(jnp.where(keep_b, -rel[None] * lg[1], 0.0)), 0.0)
    dtab = jnp.stack([d_f, d_b])
    col = idx[None, :, None]
    qd = jnp.stack([jnp.exp((col + 1.0) * lg[0]), jnp.exp((c - col) * lg[1])])
    kd = jnp.stack([jnp.exp((c - 1.0 - col) * lg[0]), jnp.exp(col * lg[1])])
    cd = jnp.exp(c * log_g)

    f_ctx = lambda bi, j: (bi, jnp.minimum(j, nc - 1))
    f_lat = lambda bi, j: (bi, jnp.maximum(j - nc, 0))
    b_ctx = lambda bi, j: (bi, jnp.maximum(nc - 1 - j, 0))
    b_lat = lambda bi, j: (bi, jnp.minimum(nc + nl - 1 - j, nl - 1))

    def qkv_specs(row_map):
        return [pl.BlockSpec((1, c, qw), lambda bi, j: row_map(bi, j) + (0,)),
                pl.BlockSpec((1, c, qw), lambda bi, j: row_map(bi, j) + (1,)),
                pl.BlockSpec((1, c, vw), lambda bi, j: row_map(bi, j) + (1,))]

    full = lambda shape: pl.BlockSpec(shape, lambda bi, j: (0,) * len(shape))
    in_specs = ([pl.BlockSpec(memory_space=pltpu.SMEM)]
                + qkv_specs(f_ctx) + qkv_specs(f_lat) + qkv_specs(b_ctx) + qkv_specs(b_lat)
                + [full(dtab.shape), full(qd.shape), full(kd.shape)])
    out_spec = lambda row_map: pl.BlockSpec((1, c, vw), lambda bi, j: row_map(bi, j) + (0,))
    out_sds = lambda n: jax.ShapeDtypeStruct((b, n, vw), BF16)
    return pl.pallas_call(
        functools.partial(_ret_kernel, n_ctx_chunks=nc),
        grid=(b, nc + nl),
        in_specs=in_specs,
        out_specs=[out_spec(f_ctx), out_spec(f_lat), out_spec(b_ctx), out_spec(b_lat)],
        out_shape=[out_sds(lc), out_sds(l), out_sds(lc), out_sds(l)],
        scratch_shapes=[pltpu.VMEM((2, RET_HEADS, RET_KEY, RET_VAL), F32)],
        compiler_params=_cparams(("parallel", "arbitrary")),
        name="retention_scan",
    )(cd, ctx_arr, ctx_arr, ctx_arr, lat_arr, lat_arr, lat_arr, ctx_arr, ctx_arr, ctx_arr, lat_arr, lat_arr,
      lat_arr, dtab, qd, kd)


def _router_kernel(x_ref, g_ref, sh_ref, sc_ref, wr_ref, *rest):
    a_ref, lg_ref = rest[-2:]
    a = _modulated(x_ref, g_ref, sh_ref, sc_ref)
    a_hi = a.astype(BF16)
    a_ref[...] = a_hi
    a_lo = (a - a_hi.astype(F32)).astype(BF16)
    w = wr_ref[...]
    w_hi = w.astype(BF16)
    w_lo = (w - w_hi.astype(F32)).astype(BF16)
    logits = _dot(a_hi, w_hi) + _dot(a_lo, w_hi) + _dot(a_hi, w_lo)
    lane = lax.broadcasted_iota(jnp.int32, logits.shape, 1)
    neg = -jnp.inf
    lg = jnp.where(lane < N_EXPERTS, logits, neg)
    v0 = jnp.max(lg, axis=-1, keepdims=True)
    i0 = jnp.min(jnp.where(lg == v0, lane, ROUTER_PAD), axis=-1, keepdims=True)
    lg = jnp.where(lane == i0, neg, lg)
    v1 = jnp.max(lg, axis=-1, keepdims=True)
    i1 = jnp.min(jnp.where(lg == v1, lane, ROUTER_PAD), axis=-1, keepdims=True)
    e1 = jnp.exp(v1 - v0)
    g0 = 1.0 / (1.0 + e1)
    route = jnp.where(lane == 0, g0, jnp.where(lane == 1, e1 * g0, jnp.where(
        lane == 2, i0.astype(F32), jnp.where(lane == 3, i1.astype(F32), 0.0))))
    lg_ref[...] = route


def _router(x, g, mod, sh_idx, rows_per_mod, w_router_pad, *, tm, name, pool_rows, row_offset=0, prev=None):
    t, d = x.shape
    assert row_offset % tm == 0
    first = row_offset // tm
    if rows_per_mod is None:
        mod_row = lambda i: 0
    else:
        tiles_per_mod = rows_per_mod // tm
        mod_row = lambda i: i // tiles_per_mod
    in_specs = [pl.BlockSpec((tm, d), lambda i: (i, 0)),
                pl.BlockSpec((1, d), lambda i: (0, 0)),
                pl.BlockSpec((1, 1, d), lambda i: (mod_row(i), 0, sh_idx)),
                pl.BlockSpec((1, 1, d), lambda i: (mod_row(i), 0, sh_idx + 1)),
                pl.BlockSpec((d, ROUTER_PAD), lambda i: (0, 0))]
    args = [x, g.reshape(1, d), mod, mod, w_router_pad]
    aliases = {}
    if prev is not None:
        in_specs += [pl.BlockSpec(memory_space=pl.ANY)] * 2
        aliases = {len(args): 0, len(args) + 1: 1}
        args += list(prev)
    return pl.pallas_call(
        _router_kernel,
        grid=(t // tm,),
        in_specs=in_specs,
        out_specs=[pl.BlockSpec((tm, d), lambda i: (first + i, 0)),
                   pl.BlockSpec((tm, ROUTER_PAD), lambda i: (first + i, 0))],
        out_shape=[jax.ShapeDtypeStruct((pool_rows, d), BF16), jax.ShapeDtypeStruct((pool_rows, ROUTER_PAD), F32)],
        input_output_aliases=aliases,
        compiler_params=_cparams(("parallel",)),
        name=name,
    )(*args)


def _experts_kernel(be_ref, nb_ref, x_ref, wg_ref, wu_ref, wd_ref, *rest, nf, first_block):
    o_ref, acc = rest[-2:]
    blk = pl.program_id(0)
    f = pl.program_id(1)
    used = first_block + blk < nb_ref[0]
    rc = min(x_ref.shape[0], MOE_ROW_CHUNK)

    @pl.when((blk == 0) & (f == 0))
    def _():
        acc[...] = jnp.zeros_like(acc)

    @pl.when(used)
    def _():
        wg = wg_ref[0, 0].astype(BF16)
        wu = wu_ref[0, 0].astype(BF16)
        wd = wd_ref[0, 0].astype(BF16)
        n_chunks = x_ref.shape[0] // rc

        def gate_up(c):
            x = x_ref[c * rc:(c + 1) * rc, :]
            return _dot(x, wg), _dot(x, wu)

        nxt = gate_up(0)
        for c in range(n_chunks):
            g, u = nxt
            if c + 1 < n_chunks:
                nxt = gate_up(c + 1)
            part = _dot((g * _sigmoid(g) * u).astype(BF16), wd)
            rows = slice(c * rc, (c + 1) * rc)
            acc[rows, :] = part + jnp.where(f > 0, acc[rows, :], 0.0)

    @pl.when(f == nf - 1)
    def _():
        o_ref[...] = jnp.where(used, acc[...], 0.0).astype(o_ref.dtype)


def _experts(buf, block_expert, n_used, w_gu, w_down, layer, *, first_block, total_blocks, prev=None):
    rows, d = buf.shape
    nb = rows // MOE_ROWS
    two_f = w_gu.shape[3]
    fdim = two_f // 2
    fc = _pick(fdim, (MOE_FC, 256, 128))
    nf = fdim // fc
    in_specs = [pl.BlockSpec((MOE_ROWS, d), lambda i, f, be, nu: (i, 0)),
                pl.BlockSpec((1, 1, d, fc), lambda i, f, be, nu: (layer, be[first_block + i], 0, f)),
                pl.BlockSpec((1, 1, d, fc), lambda i, f, be, nu: (layer, be[first_block + i], 0, nf + f)),
                pl.BlockSpec((1, 1, fc, d), lambda i, f, be, nu: (layer, be[first_block + i], f, 0))]
    args = [block_expert, n_used, buf, w_gu, w_gu, w_down]
    aliases = {}
    if prev is not None:
        in_specs.append(pl.BlockSpec(memory_space=pl.ANY))
        aliases = {len(args): 0}
        args.append(prev)
    grid_spec = pltpu.PrefetchScalarGridSpec(
        num_scalar_prefetch=2,
        grid=(nb, nf),
        in_specs=in_specs,
        out_specs=pl.BlockSpec((MOE_ROWS, d), lambda i, f, be, nu: (first_block + i, 0)),
        scratch_shapes=[pltpu.VMEM((MOE_ROWS, d), F32)],
    )
    return pl.pallas_call(
        functools.partial(_experts_kernel, nf=nf, first_block=first_block),
        grid_spec=grid_spec,
        out_shape=jax.ShapeDtypeStruct((total_blocks * MOE_ROWS, d), BF16),
        input_output_aliases=aliases,
        compiler_params=_cparams(("parallel", "arbitrary")),
        name="moe_experts",
    )(*args)


def _moe(a_all, top_idx, w_gu, w_down, layer):
    t, d = a_all.shape
    e = w_gu.shape[1]
    onehot = jnp.sum((top_idx[..., None] == jnp.arange(e)[None, None, :]).astype(jnp.int32), axis=1)
    counts = jnp.sum(onehot, axis=0)
    rank = jnp.cumsum(onehot, axis=0) - onehot
    padded = (counts + MOE_ROWS - 1) // MOE_ROWS * MOE_ROWS
    pend = jnp.cumsum(padded)
    pstart = pend - padded
    dest = pstart[top_idx] + jnp.take_along_axis(rank, top_idx, axis=1)
    nb = -(-(t * TOP_K) // MOE_ROWS) + e
    block_expert = jnp.minimum(jnp.searchsorted(pend, jnp.arange(nb) * MOE_ROWS, side="right"), e - 1).astype(jnp.int32)
    n_used = (pend[-1] // MOE_ROWS).astype(jnp.int32).reshape(1)
    by_expert = (jnp.argsort(top_idx.reshape(-1), stable=True) // TOP_K).astype(jnp.int32)
    by_expert = jnp.concatenate([by_expert, jnp.zeros((nb * MOE_ROWS + MOE_ROWS - t * TOP_K,), jnp.int32)])
    start = jnp.cumsum(counts) - counts
    first = jnp.arange(nb) * MOE_ROWS - (pstart - start)[block_expert]
    row_src = by_expert[(first[:, None] + jnp.arange(MOE_ROWS)[None, :]).reshape(-1)]
    n_groups = _pick(nb, (MOE_GROUPS, 2, 1))
    per = nb // n_groups
    yb = None
    for k in range(n_groups):
        buf = a_all[row_src[k * per * MOE_ROWS:(k + 1) * per * MOE_ROWS]]
        yb = _experts(buf, block_expert, n_used, w_gu, w_down, layer, first_block=k * per, total_blocks=nb, prev=yb)
    return yb, dest


def _rope_angles(pos, dim):
    inv = np.float32(ROPE_BASE) ** (-np.arange(0, dim, 2, dtype=np.float32) / np.float32(dim))
    ang = pos.astype(np.float32)[:, None] * inv[None, :]
    return np.concatenate([ang, ang], axis=-1)


def _signed_sin(sin, shift):
    low = (np.arange(sin.shape[1]) % (2 * shift)) < shift
    zero = np.float32(0.0)
    return jnp.asarray(np.where(low[None, :], -sin, zero)), jnp.asarray(np.where(low[None, :], zero, sin))


def _axial_tables(n_tokens, head_dim):
    rows = n_tokens // GRID_W
    row = np.repeat(np.arange(rows, dtype=np.int32), GRID_W)
    col = np.tile(np.arange(GRID_W, dtype=np.int32), rows)
    half = head_dim // 2
    ang = np.concatenate([_rope_angles(row, half), _rope_angles(col, half)], axis=-1)
    ang = np.tile(ang, (1, LANES // head_dim))
    sa, sb = _signed_sin(np.sin(ang), half // 2)
    return jnp.asarray(np.cos(ang)), sa, sb, half // 2


def _ret_tables(n_tokens):
    ang = _rope_angles(np.arange(n_tokens), RET_KEY)
    sa, sb = _signed_sin(np.sin(ang), RET_KEY // 2)
    return jnp.asarray(np.cos(ang)), sa, sb, RET_KEY // 2


def kernel(x, c, ctx, c_ctx, mod_w, mod_b, norm_g, da_w_in, da_w_out, da_lambda, da_subln_g, wa_w_in, wa_w_out,
           wa_sink, ret_w_in, ret_w_out, ret_decay_logit, ffn_w_gu, ffn_w_down, moe_router, moe_w_gu, moe_w_down):
    b, l, d = x.shape
    lc = ctx.shape[1]
    depth = mod_w.shape[0]
    t, tc = b * l, b * lc
    tm = _pick(l, (1024, 512, 256, 128))
    tmc = _pick(tc, (1024, 512, 256, 128))

    n_cond = -(-(b + 1) // 8) * 8
    cs = jnp.zeros((n_cond, d), F32).at[:b].set(c).at[b].set(c_ctx)
    mod_all = _mod_vectors(cs, mod_w, mod_b)

    xs = x.reshape(t, d)
    hs = ctx.reshape(tc, d)
    mixer_count = [0] * N_MIXERS
    i_dense = 0
    i_moe = 0
    for layer in range(depth):
        last = layer == depth - 1
        mod = mod_all[layer, :b].reshape(b, 1, 6 * d)
        mod_c = mod_all[layer, b:b + 1].reshape(1, 1, 6 * d)
        ng = norm_g[layer]
        kind = layer % N_MIXERS
        jm = mixer_count[kind]
        mixer_count[kind] += 1

        if kind == 0:
            lam_init = 0.8 - 0.6 * math.exp(-0.3 * layer)
            w_in = da_w_in[jm].astype(BF16)
            cos, sa, sb, shift = _axial_tables(l, DA_HEAD)
            qkv = _norm_proj(xs, ng[0], mod, 0, l, w_in, tm=tm, tn=3 * d, rot=(cos, sa, sb, 2 * d, shift, l),
                             name="da_in_proj").reshape(b, l, 3 * d)
            qkv_c = _norm_proj(hs, ng[0], mod_c, 0, None, w_in, tm=tmc, tn=3 * d, name="da_in_proj_ctx").reshape(b, lc, 3 * d)
            lp = da_lambda[jm].astype(F32)
            lam = jnp.exp(jnp.sum(lp[0] * lp[1])) - jnp.exp(jnp.sum(lp[2] * lp[3])) + lam_init
            o = _diff_attention(qkv, [qkv_c, qkv], lam, da_subln_g[jm], lam_init, tq=_pick(l, (DA_TQ, 128)), name="diff_attn")
            w_out = da_w_out[jm].astype(BF16)
            xs = _resid(xs, ng[1], mod, 2, l, tm=tm, a=o.reshape(t, d), w=w_out, name="da_out_proj")
            if not last:
                oc = _diff_attention(qkv_c, [qkv_c], lam, da_subln_g[jm], lam_init, tq=_pick(lc, (DA_TQ, 128)), name="diff_attn_ctx")
                hs = _resid(hs, ng[1], mod_c, 2, None, tm=tmc, a=oc.reshape(tc, d), w=w_out, name="da_out_proj_ctx")
        elif kind == 1:
            w_in = wa_w_in[jm].astype(BF16)
            n_qkv = w_in.shape[1]
            cos, sa, sb, shift = _axial_tables(l, WA_HEAD)
            n_rot = (WA_Q_HEADS + WA_KV_HEADS) * WA_HEAD
            qkv = _norm_proj(xs, ng[0], mod, 0, l, w_in, tm=tm, tn=n_qkv, rot=(cos, sa, sb, n_rot, shift, l),
                             name="wa_in_proj").reshape(b, l, n_qkv)
            qkv_c = _norm_proj(hs, ng[0], mod_c, 0, None, w_in, tm=tmc, tn=n_qkv, name="wa_in_proj_ctx").reshape(b, lc, n_qkv)
            o = _window_attention(qkv, qkv_c, wa_sink[jm], windowed=True, name="window_attn")
            w_out = wa_w_out[jm].astype(BF16)
            xs = _resid(xs, ng[1], mod, 2, l, tm=tm, a=o.reshape(t, d), w=w_out, name="wa_out_proj")
            if not last:
                oc = _window_attention(qkv_c, qkv_c, wa_sink[jm], windowed=False, name="window_attn_ctx")
                hs = _resid(hs, ng[1], mod_c, 2, None, tm=tmc, a=oc.reshape(tc, d), w=w_out, name="wa_out_proj_ctx")
        else:
            w_in = ret_w_in[jm].astype(BF16)
            n_qkv = w_in.shape[1]
            cos, sa, sb, shift = _ret_tables(l)
            n_rot = 2 * RET_HEADS * RET_KEY
            qkv = _norm_proj(xs, ng[0], mod, 0, l, w_in, tm=tm, tn=n_qkv // 2, rot=(cos, sa, sb, n_rot, shift, l),
                             name="ret_in_proj")
            qkv_c = _norm_proj(hs, ng[0], mod_c, 0, None, w_in, tm=_pick(tc, (512, 256, 128)), tn=n_qkv // 2,
                               name="ret_in_proj_ctx")
            log_g = jax.nn.log_sigmoid(ret_decay_logit[jm].astype(F32))
            ocf, olf, ocb, olb = _retention(qkv_c.reshape(b, lc, n_qkv), qkv.reshape(b, l, n_qkv), log_g)
            w_out = ret_w_out[jm].astype(BF16)
            vw = RET_HEADS * RET_VAL
            xs = _resid(xs, ng[1], mod, 2, l, tm=_pick(l, (512, 256, 128)), w=w_out,
                        ret=(olf.reshape(t, vw), olb.reshape(t, vw), qkv), name="ret_out_proj")
            if not last:
                hs = _resid(hs, ng[1], mod_c, 2, None, tm=_pick(tc, (512, 256, 128)), w=w_out,
                            ret=(ocf.reshape(tc, vw), ocb.reshape(tc, vw), qkv_c), name="ret_out_proj_ctx")

        if layer % 2 == 0:
            w_gu = ffn_w_gu[i_dense].astype(BF16)
            w_dn = ffn_w_down[i_dense].astype(BF16)
            i_dense += 1
            f = w_dn.shape[0]
            w_g, w_u = w_gu[:, :f], w_gu[:, f:]
            tn = f
            tf = _pick(l, (512, 256, 128))
            act = _norm_proj(xs, ng[2], mod, 3, l, (w_g, w_u), tm=tf, tn=tn, name="ffn_up")
            xs = _resid(xs, ng[3], mod, 5, l, tm=tf, a=act, w=w_dn, name="ffn_down")
            if not last:
                tfc = _pick(tc, (512, 256, 128))
                act_c = _norm_proj(hs, ng[2], mod_c, 3, None, (w_g, w_u), tm=tfc, tn=tn, name="ffn_up_ctx")
                hs = _resid(hs, ng[3], mod_c, 5, None, tm=tfc, a=act_c, w=w_dn, name="ffn_down_ctx")
        else:
            w_r = jnp.zeros((d, ROUTER_PAD), F32).at[:, :N_EXPERTS].set(moe_router[i_moe])
            moe_layer = i_moe
            i_moe += 1
            pool = t if last else t + tc
            a_all, route = _router(xs, ng[2], mod, 3, l, w_r, tm=tm, name="moe_router", pool_rows=pool)
            if not last:
                a_all, route = _router(hs, ng[2], mod_c, 3, None, w_r, tm=tmc, name="moe_router_ctx", pool_rows=pool,
                                       row_offset=t, prev=(a_all, route))
            route_x, route_c = route[:t], route[t:]
            yb, dest = _moe(a_all, route[:, 2:4].astype(jnp.int32), moe_w_gu, moe_w_down, moe_layer)
            xs = _resid(xs, ng[3], mod, 5, l, tm=tm, mix=(yb[dest[:t, 0]], yb[dest[:t, 1]], route_x), name="moe_combine")
            if not last:
                hs = _resid(hs, ng[3], mod_c, 5, None, tm=tmc, mix=(yb[dest[t:, 0]], yb[dest[t:, 1]], route_c),
                            name="moe_combine_ctx")
    return xs.reshape(b, l, d)
```

```python
import functools
import math

import jax
import jax.numpy as jnp
import numpy as np
from jax import lax
from jax.experimental import pallas as pl
from jax.experimental.pallas import tpu as pltpu

F32 = jnp.float32
BF16 = jnp.bfloat16

D_MODEL = 1024
N_MIXERS = 3
NORM_EPS = 1e-6
ROPE_BASE = 10000.0
GRID_W = 64
DA_HEAD = 64
DA_HEADS = D_MODEL // (2 * DA_HEAD)
WA_HEAD = 64
WA_Q_HEADS = D_MODEL // WA_HEAD
WA_KV_HEADS = WA_Q_HEADS // 4
WINDOW = 128
RET_KEY = 256
RET_HEADS = D_MODEL // RET_KEY
RET_VAL = 2 * RET_KEY
RET_CHUNK = 128
N_EXPERTS = 8
TOP_K = 2

LANES = 128
V7X_VMEM_LIMIT = 56 * 1024 * 1024
ROUTER_PAD = LANES
MOE_ROWS = 1024
MOE_FC = 512
MOE_ROW_CHUNK = 256
MOE_GROUPS = 4
PROJ_COLS = 512


def _cparams(sem):
    return pltpu.CompilerParams(dimension_semantics=sem, vmem_limit_bytes=V7X_VMEM_LIMIT)


def _pick(n, prefs):
    for p in prefs:
        if n % p == 0:
            return p
    return n


def _sigmoid(x):
    return 1.0 / (1.0 + jnp.exp(-x))


def _rms(y):
    return y * lax.rsqrt(jnp.mean(y * y, axis=-1, keepdims=True) + NORM_EPS)


def _dot(a, b):
    return jnp.dot(a, b, preferred_element_type=F32)


def _dot_nt(a, b):
    return lax.dot_general(a, b, (((1,), (1,)), ((), ())), preferred_element_type=F32)


def _dot_tn(a, b):
    return lax.dot_general(a, b, (((0,), (0,)), ((), ())), preferred_element_type=F32)


def _mod_kernel(c_ref, w_ref, b_ref, o_ref):
    cs = c_ref[...]
    s = cs * _sigmoid(cs)
    w = w_ref[0]
    s_hi = s.astype(BF16)
    s_lo = (s - s_hi.astype(F32)).astype(BF16)
    w_hi = w.astype(BF16)
    w_lo = (w - w_hi.astype(F32)).astype(BF16)
    o_ref[0] = _dot(s_hi, w_hi) + _dot(s_lo, w_hi) + _dot(s_hi, w_lo) + b_ref[0]


def _mod_vectors(cs, mod_w, mod_b):
    depth, d, n = mod_w.shape
    r = cs.shape[0]
    tn = _pick(n, (1024, 512, 256, 128))
    return pl.pallas_call(
        _mod_kernel,
        grid=(depth, n // tn),
        in_specs=[
            pl.BlockSpec((r, d), lambda l, j: (0, 0)),
            pl.BlockSpec((1, d, tn), lambda l, j: (l, 0, j)),
            pl.BlockSpec((1, 1, tn), lambda l, j: (l, 0, j)),
        ],
        out_specs=pl.BlockSpec((1, r, tn), lambda l, j: (l, 0, j)),
        out_shape=jax.ShapeDtypeStruct((depth, r, n), F32),
        compiler_params=_cparams(("parallel", "parallel")),
        name="mod_vectors",
    )(cs, mod_w, mod_b.reshape(depth, 1, n))


def _modulated(x_ref, g_ref, sh_ref, sc_ref):
    x = x_ref[...]
    return _rms(x) * g_ref[...] * (1.0 + sc_ref[0]) + sh_ref[0]


def _rotate(acc, cos, sa, sb, shift, cw):
    outs = []
    for c in range(acc.shape[1] // cw):
        y = acc[:, c * cw:(c + 1) * cw]
        up = pltpu.roll(y, cw - shift, 1)
        dn = pltpu.roll(y, shift, 1)
        outs.append(y * cos + up * sa + dn * sb)
    return outs[0] if len(outs) == 1 else jnp.concatenate(outs, axis=1)


def _col_chunks(width, cuts):
    edges = sorted({0, width} | set(range(PROJ_COLS, width, PROJ_COLS)) | {c for c in cuts if 0 < c < width})
    return list(zip(edges[:-1], edges[1:]))


def _proj_kernel(*refs, mode, n_rot_cols, shift, cw, n_col_tiles):
    if mode == "swiglu":
        x_ref, g_ref, sh_ref, sc_ref, wg_ref, wu_ref, o_ref, a_scr = refs
    elif mode == "rot":
        x_ref, g_ref, sh_ref, sc_ref, w_ref, cos_ref, sa_ref, sb_ref, o_ref, a_scr = refs
    else:
        x_ref, g_ref, sh_ref, sc_ref, w_ref, o_ref, a_scr = refs
    j = pl.program_id(1)
    tn = o_ref.shape[1]

    @pl.when(j == 0)
    def _():
        a_scr[...] = _modulated(x_ref, g_ref, sh_ref, sc_ref).astype(BF16)

    def tile(jt):
        a = a_scr[...]
        for lo, hi in _col_chunks(tn, [n_rot_cols - jt * tn]):
            if mode == "swiglu":
                g = _dot(a, wg_ref[:, lo:hi])
                res = g * _sigmoid(g) * _dot(a, wu_ref[:, lo:hi])
            else:
                res = _dot(a, w_ref[:, lo:hi])
                if mode == "rot" and jt * tn + hi <= n_rot_cols:
                    res = _rotate(res, cos_ref[...], sa_ref[...], sb_ref[...], shift, cw)
            o_ref[:, lo:hi] = res.astype(o_ref.dtype)

    if mode != "rot":
        tile(0)
    else:
        n_rot_tiles = -(-n_rot_cols // tn)
        for jt in range(min(n_rot_tiles + 1, n_col_tiles)):
            @pl.when((j >= jt) if jt == n_rot_tiles else (j == jt))
            def _(jt=jt):
                tile(jt)


def _norm_proj(x, g, mod, sh_idx, rows_per_mod, weights, *, tm, tn, rot=None, name):
    t, d = x.shape
    swiglu = isinstance(weights, tuple)
    n = weights[0].shape[1] if swiglu else weights.shape[1]
    assert t % tm == 0 and n % tn == 0
    if rows_per_mod is None:
        mod_row = lambda i: 0
    else:
        assert rows_per_mod % tm == 0
        tiles_per_mod = rows_per_mod // tm
        mod_row = lambda i: i // tiles_per_mod
    in_specs = [
        pl.BlockSpec((tm, d), lambda i, j: (i, 0)),
        pl.BlockSpec((1, d), lambda i, j: (0, 0)),
        pl.BlockSpec((1, 1, d), lambda i, j: (mod_row(i), 0, sh_idx)),
        pl.BlockSpec((1, 1, d), lambda i, j: (mod_row(i), 0, sh_idx + 1)),
    ]
    args = [x, g.reshape(1, d), mod, mod]
    w_spec = pl.BlockSpec((d, tn), lambda i, j: (0, j))
    n_rot_cols, shift, cw = 0, 0, 0
    if swiglu:
        mode = "swiglu"
        in_specs += [w_spec, w_spec]
        args += list(weights)
    elif rot is not None:
        mode = "rot"
        cos, sa, sb, n_rot_cols, shift, seq_len = rot
        cw = cos.shape[1]
        assert n_rot_cols % cw == 0 and PROJ_COLS % cw == 0 and seq_len % tm == 0
        tiles_per_seq = seq_len // tm
        t_spec = pl.BlockSpec((tm, cw), lambda i, j: (i % tiles_per_seq, 0))
        in_specs += [w_spec, t_spec, t_spec, t_spec]
        args += [weights, cos, sa, sb]
    else:
        mode = "plain"
        in_specs += [w_spec]
        args += [weights]
    return pl.pallas_call(
        functools.partial(_proj_kernel, mode=mode, n_rot_cols=n_rot_cols, shift=shift, cw=cw, n_col_tiles=n // tn),
        grid=(t // tm, n // tn),
        in_specs=in_specs,
        out_specs=pl.BlockSpec((tm, tn), lambda i, j: (i, j)),
        out_shape=jax.ShapeDtypeStruct((t, n), BF16),
        scratch_shapes=[pltpu.VMEM((tm, d), BF16)],
        compiler_params=_cparams(("parallel", "arbitrary")),
        name=name,
    )(*args)


def _resid_kernel(*refs, mode):
    if mode == "ret":
        of_ref, ob_ref, gt_ref, w_ref, x_ref, gate_ref, g_ref, o_ref = refs
        o = of_ref[...].astype(F32) + ob_ref[...].astype(F32)
        parts = []
        for h in range(RET_HEADS):
            parts.append(_rms(o[:, h * RET_VAL:(h + 1) * RET_VAL]))
        o = jnp.concatenate(parts, axis=1)
        gt = gt_ref[...].astype(F32)
        y = _dot((gt * _sigmoid(gt) * o).astype(BF16), w_ref[...])
    elif mode == "matmul":
        a_ref, w_ref, x_ref, gate_ref, g_ref, o_ref = refs
        y = _dot(a_ref[...], w_ref[...])
    else:
        y0_ref, y1_ref, route_ref, x_ref, gate_ref, g_ref, o_ref = refs
        route = route_ref[...]
        y = y0_ref[...].astype(F32) * route[:, 0:1] + y1_ref[...].astype(F32) * route[:, 1:2]
    o_ref[...] = x_ref[...] + gate_ref[0] * (_rms(y) * g_ref[...])


def _resid(x, g, mod, gate_idx, rows_per_mod, *, tm, name, a=None, w=None, mix=None, ret=None):
    t, d = x.shape
    assert t % tm == 0
    if rows_per_mod is None:
        mod_row = lambda i: 0
    else:
        assert rows_per_mod % tm == 0
        tiles_per_mod = rows_per_mod // tm
        mod_row = lambda i: i // tiles_per_mod
    row = lambda width: pl.BlockSpec((tm, width), lambda i: (i, 0))
    tail_specs = [row(d), pl.BlockSpec((1, 1, d), lambda i: (mod_row(i), 0, gate_idx)),
                  pl.BlockSpec((1, d), lambda i: (0, 0))]
    tail_args = [x, mod, g.reshape(1, d)]
    if ret is not None:
        mode = "ret"
        o_f, o_b, qkvg = ret
        vw = RET_HEADS * RET_VAL
        gate_col = qkvg.shape[1] // vw - 1
        in_specs = [row(vw), row(vw), pl.BlockSpec((tm, vw), lambda i: (i, gate_col)),
                    pl.BlockSpec(w.shape, lambda i: (0, 0))]
        args = [o_f, o_b, qkvg, w]
    elif a is not None:
        mode = "matmul"
        in_specs = [row(a.shape[1]), pl.BlockSpec(w.shape, lambda i: (0, 0))]
        args = [a, w]
    else:
        mode = "mix"
        in_specs = [row(d), row(d), row(ROUTER_PAD)]
        args = list(mix)
    return pl.pallas_call(
        functools.partial(_resid_kernel, mode=mode),
        grid=(t // tm,),
        in_specs=in_specs + tail_specs,
        out_specs=row(d),
        out_shape=jax.ShapeDtypeStruct((t, d), F32),
        compiler_params=_cparams(("parallel",)),
        name=name,
    )(*args, *tail_args)


DA_SUM_ROWS = 16
DA_TQ = 256
DA_EXP_ROWS = 16


def _da_kernel(*refs, n_seg, post_scale, n_tiles, nq):
    lam_ref, q_ref = refs[0], refs[1]
    k_refs = refs[2:2 + n_seg]
    v_refs = refs[2 + n_seg:2 + 2 * n_seg]
    sg_ref, o_ref, k_scr, vt_scr, s_a, s_b, p_a, p_b, mx_a, mx_b = refs[2 + 2 * n_seg:]
    hw = 2 * DA_HEAD
    n_keys = vt_scr.shape[1]
    tq = q_ref.shape[1]
    g = pl.program_id(0)
    t_qk = jnp.minimum(g, n_tiles - 1)
    t_pv = jnp.clip(g - 2, 0, n_tiles - 1)

    @pl.when(g == 0)
    def _():
        for buf in (s_a, s_b, p_a, p_b, mx_a, mx_b):
            buf[...] = jnp.zeros_like(buf)

    @pl.when(lax.rem(t_qk, nq) == 0)
    def _():
        off = 0
        for kr in k_refs:
            n = kr.shape[1]
            k_scr[off:off + n, :] = kr[0]
            off += n

    @pl.when(lax.rem(t_pv, nq) == 0)
    def _():
        off = 0
        for vr in v_refs:
            n = vr.shape[1]
            vt_scr[0:hw, off:off + n] = vr[0].astype(F32).T.astype(BF16)
            off += n
        row = lax.broadcasted_iota(jnp.int32, (DA_SUM_ROWS, n_keys), 0)
        vt_scr[hw:hw + DA_SUM_ROWS, :] = jnp.where(row == 0, 1.0, 0.0).astype(BF16)

    lane = lax.broadcasted_iota(jnp.int32, (1, hw), 1)
    qscale = DA_HEAD ** -0.5 * math.log2(math.e)

    def stages(s_new, mx_new, s_old, mx_old, p_old, p_older):
        accs = []
        for m in range(2):
            accs.append(_dot(vt_scr[...], p_older[m]))
        q = q_ref[0].astype(F32)
        for m in range(2):
            sel = (lane < DA_HEAD) if m == 0 else (lane >= DA_HEAD)
            qm = jnp.where(sel, q * qscale, 0.0).astype(BF16)
            s_t = _dot_nt(k_scr[...], qm)
            s_new[m] = s_t
            part = [None] * 4
            for r in range(n_keys // 8):
                tile = s_t[r * 8:(r + 1) * 8, :]
                j = r % len(part)
                part[j] = tile if part[j] is None else jnp.maximum(part[j], tile)
            mx_new[m] = functools.reduce(jnp.maximum, part)
            mx8 = jnp.broadcast_to(jnp.max(mx_old[m], axis=0, keepdims=True), (8, tq))
            for c in range(n_keys // DA_EXP_ROWS):
                rows = slice(c * DA_EXP_ROWS, (c + 1) * DA_EXP_ROWS)
                zero = jnp.minimum(jnp.abs(s_t[c * DA_EXP_ROWS:c * DA_EXP_ROWS + 8, :]), 0.0)
                d = s_old[m, rows, :].reshape(DA_EXP_ROWS // 8, 8, tq) - (mx8 + zero)[None]
                p_old[m, rows, :] = jnp.exp2(d.reshape(DA_EXP_ROWS, tq).astype(BF16))
        a0, a1 = accs
        o_t = a0[:hw] * (1.0 / a0[hw:hw + 1]) - a1[:hw] * (lam_ref[0, 0] / a1[hw:hw + 1])
        o_t = o_t * lax.rsqrt(jnp.mean(o_t * o_t, axis=0, keepdims=True) + NORM_EPS) * (sg_ref[...] * post_scale)
        o_ref[0] = o_t.T.astype(o_ref.dtype)

    @pl.when(lax.rem(g, 2) == 0)
    def _():
        stages(s_a, mx_a, s_b, mx_b, p_b, p_a)

    @pl.when(lax.rem(g, 2) == 1)
    def _():
        stages(s_b, mx_b, s_a, mx_a, p_a, p_b)


def _diff_attention(q_arr, kv_arrs, lam, subln_g, lam_init, *, tq, name):
    b, lq, _ = q_arr.shape
    hw = 2 * DA_HEAD
    n_seg = len(kv_arrs)
    n_keys = sum(kv.shape[1] for kv in kv_arrs)
    assert lq % tq == 0 and n_keys % DA_EXP_ROWS == 0
    nq = lq // tq
    n_tiles = b * DA_HEADS * nq

    def tile(t):
        return t // (nq * DA_HEADS), lax.rem(t // nq, DA_HEADS), lax.rem(t, nq)

    def qk_tile(g):
        return tile(jnp.minimum(g, n_tiles - 1))

    def pv_tile(g):
        return tile(jnp.clip(g - 2, 0, n_tiles - 1))

    in_specs = [pl.BlockSpec(memory_space=pltpu.SMEM),
                pl.BlockSpec((1, tq, hw), lambda g: (qk_tile(g)[0], qk_tile(g)[2], qk_tile(g)[1]))]
    in_specs += [pl.BlockSpec((1, kv.shape[1], hw), lambda g: (qk_tile(g)[0], 0, DA_HEADS + qk_tile(g)[1]))
                 for kv in kv_arrs]
    in_specs += [pl.BlockSpec((1, kv.shape[1], hw), lambda g: (pv_tile(g)[0], 0, 2 * DA_HEADS + pv_tile(g)[1]))
                 for kv in kv_arrs]
    in_specs += [pl.BlockSpec((hw, 1), lambda g: (0, 0))]
    s_buf = pltpu.VMEM((2, n_keys, tq), F32)
    p_buf = pltpu.VMEM((2, n_keys, tq), BF16)
    mx_buf = pltpu.VMEM((2, 8, tq), F32)
    return pl.pallas_call(
        functools.partial(_da_kernel, n_seg=n_seg, post_scale=1.0 - lam_init, n_tiles=n_tiles, nq=nq),
        grid=(n_tiles + 2,),
        in_specs=in_specs,
        out_specs=pl.BlockSpec((1, tq, hw), lambda g: (pv_tile(g)[0], pv_tile(g)[2], pv_tile(g)[1])),
        out_shape=jax.ShapeDtypeStruct((b, lq, D_MODEL), BF16),
        scratch_shapes=[pltpu.VMEM((n_keys, hw), BF16), pltpu.VMEM((hw + DA_SUM_ROWS, n_keys), BF16),
                        s_buf, s_buf, p_buf, p_buf, mx_buf, mx_buf],
        compiler_params=_cparams(("arbitrary",)),
        name=name,
    )(lam.reshape(1, 1), q_arr, *kv_arrs, *kv_arrs, subln_g.reshape(hw, 1))


WA_SUM_ROWS = 16


def _wa_kernel(*refs, windowed, seq_len):
    if windowed:
        sink_ref, q_ref, kc_ref, vc_ref, kp_ref, kq_ref, kn_ref, vp_ref, vq_ref, vn_ref, o_ref = refs
        k_all = jnp.concatenate([kc_ref[0], kp_ref[0], kq_ref[0], kn_ref[0]], axis=0)
        v_all = jnp.concatenate([vc_ref[0], vp_ref[0], vq_ref[0], vn_ref[0]], axis=0)
    else:
        sink_ref, q_ref, kc_ref, vc_ref, o_ref = refs
        k_all, v_all = kc_ref[0], vc_ref[0]
    tq = q_ref.shape[1]
    n_keys = k_all.shape[0]
    n_ctx = kc_ref.shape[1]
    n_cols = 2 * tq
    i = pl.program_id(1)
    col = lax.broadcasted_iota(jnp.int32, (1, n_cols), 1)
    first = col < tq
    if windowed:
        row = lax.broadcasted_iota(jnp.int32, (n_keys - n_ctx, n_cols), 0)
        qpos = i * tq + jnp.where(first, col, col - tq)
        kpos = (i - 1) * tq + row
        ok = (jnp.abs(qpos - kpos) <= WINDOW) & (kpos >= 0) & (kpos < seq_len)
    lane = lax.broadcasted_iota(jnp.int32, (1, LANES), 1)
    lo = lane < WA_HEAD
    log2e = math.log2(math.e)
    qscale = WA_HEAD ** -0.5 * log2e
    rep = WA_Q_HEADS // WA_KV_HEADS
    v_t = v_all.astype(F32).T
    sum_rows = jnp.where(lax.broadcasted_iota(jnp.int32, (WA_SUM_ROWS, n_keys), 0) == 0, 1.0, 0.0)
    def scores(g, half):
        kg = k_all[:, (g // 2) * LANES:(g // 2 + 1) * LANES]
        zero = jnp.zeros_like(kg)
        k_same = jnp.where(lo, kg, zero) if g % 2 == 0 else jnp.where(lo, zero, kg)
        if (g % 2 == 0) == (half == 0):
            k_half = k_same
        else:
            k_half = pltpu.roll(k_same.astype(F32), WA_HEAD, 1).astype(BF16)
        chunks = [q_ref[0, :, (g * (rep // 2) + pair) * LANES:(g * (rep // 2) + pair + 1) * LANES]
                  for pair in range(rep // 2)]
        qs = (jnp.concatenate(chunks, axis=0).astype(F32) * qscale).astype(BF16)
        return _dot_nt(k_half, qs)

    def weighted(g, half, s_t):
        if windowed:
            s_t = jnp.concatenate([s_t[:n_ctx], jnp.where(ok, s_t[n_ctx:], -1e30)], axis=0)
        snk = jnp.where(first, sink_ref[g * rep + half], sink_ref[g * rep + 2 + half]) * log2e
        mx = jnp.maximum(jnp.max(s_t, axis=0, keepdims=True), snk)
        vt_g = jnp.concatenate([v_t[g * WA_HEAD:(g + 1) * WA_HEAD, :], sum_rows], axis=0).astype(BF16)
        acc = _dot(vt_g, jnp.exp2((s_t - mx).astype(BF16)))
        l = acc[WA_HEAD:WA_HEAD + 1] + jnp.exp2(snk - mx)
        return acc[:WA_HEAD] * (1.0 / l)

    units = [(g, half) for g in range(WA_KV_HEADS) for half in range(2)]
    outs = []
    nxt = scores(*units[0])
    for u, (g, half) in enumerate(units):
        cur = nxt
        if u + 1 < len(units):
            nxt = scores(*units[u + 1])
        outs.append(weighted(g, half, cur))
        if half == 1:
            for pair in range(rep // 2):
                cols = slice(pair * tq, (pair + 1) * tq)
                o_t = jnp.concatenate([outs[-2][:, cols], outs[-1][:, cols]], axis=0)
                chunk = g * (rep // 2) + pair
                o_ref[0, :, chunk * LANES:(chunk + 1) * LANES] = o_t.T.astype(o_ref.dtype)


def _window_attention(q_arr, ctx_arr, sink, *, windowed, name):
    b, lq, _ = q_arr.shape
    n_ctx = ctx_arr.shape[1]
    qw = WA_Q_HEADS * WA_HEAD
    kw = WA_KV_HEADS * WA_HEAD
    k_col, v_col = qw // kw, qw // kw + 1
    tq = 128 if windowed else lq
    nq = lq // tq
    in_specs = [pl.BlockSpec(memory_space=pltpu.SMEM),
                pl.BlockSpec((1, tq, qw), lambda bi, i: (bi, i, 0)),
                pl.BlockSpec((1, n_ctx, kw), lambda bi, i: (bi, 0, k_col)),
                pl.BlockSpec((1, n_ctx, kw), lambda bi, i: (bi, 0, v_col))]
    args = [sink, q_arr, ctx_arr, ctx_arr]
    if windowed:
        assert tq == WINDOW
        for col in (k_col, v_col):
            in_specs += [pl.BlockSpec((1, tq, kw), lambda bi, i, col=col: (bi, jnp.maximum(i - 1, 0), col)),
                         pl.BlockSpec((1, tq, kw), lambda bi, i, col=col: (bi, i, col)),
                         pl.BlockSpec((1, tq, kw), lambda bi, i, col=col: (bi, jnp.minimum(i + 1, nq - 1), col))]
            args += [q_arr, q_arr, q_arr]
    return pl.pallas_call(
        functools.partial(_wa_kernel, windowed=windowed, seq_len=lq),
        grid=(b, nq),
        in_specs=in_specs,
        out_specs=pl.BlockSpec((1, tq, qw), lambda bi, i: (bi, i, 0)),
        out_shape=jax.ShapeDtypeStruct((b, lq, qw), BF16),
        compiler_params=_cparams(("parallel", "parallel")),
        name=name,
    )(*args)


def _ret_kernel(cd_ref, qcf, kcf, vcf, qlf, klf, vlf, qcb, kcb, vcb, qlb, klb, vlb, dtab_ref, qd_ref, kd_ref,
                ocf, olf, ocb, olb, s_scr, *, n_ctx_chunks):
    j = pl.program_id(1)
    is_ctx = j < n_ctx_chunks

    @pl.when(j == 0)
    def _():
        s_scr[...] = jnp.zeros_like(s_scr)

    kscale = jnp.asarray(RET_KEY ** -0.5, BF16)
    dirs = ((qcf, kcf, vcf, qlf, klf, vlf), (qcb, kcb, vcb, qlb, klb, vlb))
    qkv = []
    for qc, kc, vc, ql, kl, vl in dirs:
        qkv.append((jnp.where(is_ctx, qc[0], ql[0]), jnp.where(is_ctx, kc[0], kl[0]) * kscale,
                    jnp.where(is_ctx, vc[0], vl[0])))

    def products(d, h):
        q, k, v = qkv[d]
        qh = q[:, h * RET_KEY:(h + 1) * RET_KEY]
        kh = k[:, h * RET_KEY:(h + 1) * RET_KEY]
        vh = v[:, h * RET_VAL:(h + 1) * RET_VAL]
        state = s_scr[d, h]
        kdec = (kh.astype(F32) * kd_ref[d, h]).astype(BF16)
        return _dot_nt(qh, kh), _dot(qh, state.astype(BF16)), _dot_tn(kdec, vh), state, vh

    def finish(d, h, prod):
        qk, q_state, k_v, state, vh = prod
        s_scr[d, h] = state * cd_ref[d, h] + k_v
        return _dot((qk * dtab_ref[d, h]).astype(BF16), vh) + q_state * qd_ref[d, h]

    units = [(d, h) for d in range(2) for h in range(RET_HEADS)]
    outs = []
    nxt = products(*units[0])
    for u, (d, h) in enumerate(units):
        cur = nxt
        if u + 1 < len(units):
            nxt = products(*units[u + 1])
        outs.append(finish(d, h, cur))
    o_f = jnp.concatenate(outs[:RET_HEADS], axis=1).astype(ocf.dtype)
    o_b = jnp.concatenate(outs[RET_HEADS:], axis=1).astype(ocb.dtype)

    @pl.when(is_ctx)
    def _():
        ocf[0] = o_f
        ocb[0] = o_b

    @pl.when(jnp.logical_not(is_ctx))
    def _():
        olf[0] = o_f
        olb[0] = o_b


def _retention(ctx_arr, lat_arr, log_g):
    b, lc, _ = ctx_arr.shape
    l = lat_arr.shape[1]
    c = RET_CHUNK
    nc, nl = lc // c, l // c
    qw = RET_HEADS * RET_KEY
    vw = RET_HEADS * RET_VAL
    idx = jnp.arange(c, dtype=F32)
    rel = idx[:, None] - idx[None, :]
    lg = log_g[:, :, None, None]
    keep_f = (rel >= 0)[None]
    keep_b = (rel < 0)[None]
    d_f = jnp.where(keep_f, jnp.exp(jnp.where(keep_f, rel[None] * lg[0], 0.0)), 0.0)
    d_b = jnp.where(keep_b, jnp.exp(jnp.where(keep_b, -rel[None] * lg[1], 0.0)), 0.0)
    dtab = jnp.stack([d_f, d_b])
    col = idx[None, :, None]
    qd = jnp.stack([jnp.exp((col + 1.0) * lg[0]), jnp.exp((c - col) * lg[1])])
    kd = jnp.stack([jnp.exp((c - 1.0 - col) * lg[0]), jnp.exp(col * lg[1])])
    cd = jnp.exp(c * log_g)

    f_ctx = lambda bi, j: (bi, jnp.minimum(j, nc - 1))
    f_lat = lambda bi, j: (bi, jnp.maximum(j - nc, 0))
    b_ctx = lambda bi, j: (bi, jnp.maximum(nc - 1 - j, 0))
    b_lat = lambda bi, j: (bi, jnp.minimum(nc + nl - 1 - j, nl - 1))

    def qkv_specs(row_map):
        return [pl.BlockSpec((1, c, qw), lambda bi, j: row_map(bi, j) + (0,)),
                pl.BlockSpec((1, c, qw), lambda bi, j: row_map(bi, j) + (1,)),
                pl.BlockSpec((1, c, vw), lambda bi, j: row_map(bi, j) + (1,))]

    full = lambda shape: pl.BlockSpec(shape, lambda bi, j: (0,) * len(shape))
    in_specs = ([pl.BlockSpec(memory_space=pltpu.SMEM)]
                + qkv_specs(f_ctx) + qkv_specs(f_lat) + qkv_specs(b_ctx) + qkv_specs(b_lat)
                + [full(dtab.shape), full(qd.shape), full(kd.shape)])
    out_spec = lambda row_map: pl.BlockSpec((1, c, vw), lambda bi, j: row_map(bi, j) + (0,))
    out_sds = lambda n: jax.ShapeDtypeStruct((b, n, vw), BF16)
    return pl.pallas_call(
        functools.partial(_ret_kernel, n_ctx_chunks=nc),
        grid=(b, nc + nl),
        in_specs=in_specs,
        out_specs=[out_spec(f_ctx), out_spec(f_lat), out_spec(b_ctx), out_spec(b_lat)],
        out_shape=[out_sds(lc), out_sds(l), out_sds(lc), out_sds(l)],
        scratch_shapes=[pltpu.VMEM((2, RET_HEADS, RET_KEY, RET_VAL), F32)],
        compiler_params=_cparams(("parallel", "arbitrary")),
        name="retention_scan",
    )(cd, ctx_arr, ctx_arr, ctx_arr, lat_arr, lat_arr, lat_arr, ctx_arr, ctx_arr, ctx_arr, lat_arr, lat_arr,
      lat_arr, dtab, qd, kd)


def _router_kernel(x_ref, g_ref, sh_ref, sc_ref, wr_ref, *rest):
    a_ref, lg_ref = rest[-2:]
    a = _modulated(x_ref, g_ref, sh_ref, sc_ref)
    a_hi = a.astype(BF16)
    a_ref[...] = a_hi
    a_lo = (a - a_hi.astype(F32)).astype(BF16)
    w = wr_ref[...]
    w_hi = w.astype(BF16)
    w_lo = (w - w_hi.astype(F32)).astype(BF16)
    logits = _dot(a_hi, w_hi) + _dot(a_lo, w_hi) + _dot(a_hi, w_lo)
    lane = lax.broadcasted_iota(jnp.int32, logits.shape, 1)
    neg = -jnp.inf
    lg = jnp.where(lane < N_EXPERTS, logits, neg)
    v0 = jnp.max(lg, axis=-1, keepdims=True)
    i0 = jnp.min(jnp.where(lg == v0, lane, ROUTER_PAD), axis=-1, keepdims=True)
    lg = jnp.where(lane == i0, neg, lg)
    v1 = jnp.max(lg, axis=-1, keepdims=True)
    i1 = jnp.min(jnp.where(lg == v1, lane, ROUTER_PAD), axis=-1, keepdims=True)
    e1 = jnp.exp(v1 - v0)
    g0 = 1.0 / (1.0 + e1)
    route = jnp.where(lane == 0, g0, jnp.where(lane == 1, e1 * g0, jnp.where(
        lane == 2, i0.astype(F32), jnp.where(lane == 3, i1.astype(F32), 0.0))))
    lg_ref[...] = route


def _router(x, g, mod, sh_idx, rows_per_mod, w_router_pad, *, tm, name, pool_rows, row_offset=0, prev=None):
    t, d = x.shape
    assert row_offset % tm == 0
    first = row_offset // tm
    if rows_per_mod is None:
        mod_row = lambda i: 0
    else:
        tiles_per_mod = rows_per_mod // tm
        mod_row = lambda i: i // tiles_per_mod
    in_specs = [pl.BlockSpec((tm, d), lambda i: (i, 0)),
                pl.BlockSpec((1, d), lambda i: (0, 0)),
                pl.BlockSpec((1, 1, d), lambda i: (mod_row(i), 0, sh_idx)),
                pl.BlockSpec((1, 1, d), lambda i: (mod_row(i), 0, sh_idx + 1)),
                pl.BlockSpec((d, ROUTER_PAD), lambda i: (0, 0))]
    args = [x, g.reshape(1, d), mod, mod, w_router_pad]
    aliases = {}
    if prev is not None:
        in_specs += [pl.BlockSpec(memory_space=pl.ANY)] * 2
        aliases = {len(args): 0, len(args) + 1: 1}
        args += list(prev)
    return pl.pallas_call(
        _router_kernel,
        grid=(t // tm,),
        in_specs=in_specs,
        out_specs=[pl.BlockSpec((tm, d), lambda i: (first + i, 0)),
                   pl.BlockSpec((tm, ROUTER_PAD), lambda i: (first + i, 0))],
        out_shape=[jax.ShapeDtypeStruct((pool_rows, d), BF16), jax.ShapeDtypeStruct((pool_rows, ROUTER_PAD), F32)],
        input_output_aliases=aliases,
        compiler_params=_cparams(("parallel",)),
        name=name,
    )(*args)


def _experts_kernel(be_ref, nb_ref, x_ref, wg_ref, wu_ref, wd_ref, *rest, nf, first_block):
    o_ref, acc = rest[-2:]
    blk = pl.program_id(0)
    f = pl.program_id(1)
    used = first_block + blk < nb_ref[0]
    rc = min(x_ref.shape[0], MOE_ROW_CHUNK)

    @pl.when((blk == 0) & (f == 0))
    def _():
        acc[...] = jnp.zeros_like(acc)

    @pl.when(used)
    def _():
        wg = wg_ref[0, 0].astype(BF16)
        wu = wu_ref[0, 0].astype(BF16)
        wd = wd_ref[0, 0].astype(BF16)
        n_chunks = x_ref.shape[0] // rc

        def gate_up(c):
            x = x_ref[c * rc:(c + 1) * rc, :]
            return _dot(x, wg), _dot(x, wu)

        nxt = gate_up(0)
        for c in range(n_chunks):
            g, u = nxt
            if c + 1 < n_chunks:
                nxt = gate_up(c + 1)
            part = _dot((g * _sigmoid(g) * u).astype(BF16), wd)
            rows = slice(c * rc, (c + 1) * rc)
            acc[rows, :] = part + jnp.where(f > 0, acc[rows, :], 0.0)

    @pl.when(f == nf - 1)
    def _():
        o_ref[...] = jnp.where(used, acc[...], 0.0).astype(o_ref.dtype)


def _experts(buf, block_expert, n_used, w_gu, w_down, layer, *, first_block, total_blocks, prev=None):
    rows, d = buf.shape
    nb = rows // MOE_ROWS
    two_f = w_gu.shape[3]
    fdim = two_f // 2
    fc = _pick(fdim, (MOE_FC, 256, 128))
    nf = fdim // fc
    in_specs = [pl.BlockSpec((MOE_ROWS, d), lambda i, f, be, nu: (i, 0)),
                pl.BlockSpec((1, 1, d, fc), lambda i, f, be, nu: (layer, be[first_block + i], 0, f)),
                pl.BlockSpec((1, 1, d, fc), lambda i, f, be, nu: (layer, be[first_block + i], 0, nf + f)),
                pl.BlockSpec((1, 1, fc, d), lambda i, f, be, nu: (layer, be[first_block + i], f, 0))]
    args = [block_expert, n_used, buf, w_gu, w_gu, w_down]
    aliases = {}
    if prev is not None:
        in_specs.append(pl.BlockSpec(memory_space=pl.ANY))
        aliases = {len(args): 0}
        args.append(prev)
    grid_spec = pltpu.PrefetchScalarGridSpec(
        num_scalar_prefetch=2,
        grid=(nb, nf),
        in_specs=in_specs,
        out_specs=pl.BlockSpec((MOE_ROWS, d), lambda i, f, be, nu: (first_block + i, 0)),
        scratch_shapes=[pltpu.VMEM((MOE_ROWS, d), F32)],
    )
    return pl.pallas_call(
        functools.partial(_experts_kernel, nf=nf, first_block=first_block),
        grid_spec=grid_spec,
        out_shape=jax.ShapeDtypeStruct((total_blocks * MOE_ROWS, d), BF16),
        input_output_aliases=aliases,
        compiler_params=_cparams(("parallel", "arbitrary")),
        name="moe_experts",
    )(*args)


def _moe(a_all, top_idx, w_gu, w_down, layer):
    t, d = a_all.shape
    e = w_gu.shape[1]
    onehot = jnp.sum((top_idx[..., None] == jnp.arange(e)[None, None, :]).astype(jnp.int32), axis=1)
    counts = jnp.sum(onehot, axis=0)
    rank = jnp.cumsum(onehot, axis=0) - onehot
    padded = (counts + MOE_ROWS - 1) // MOE_ROWS * MOE_ROWS
    pend = jnp.cumsum(padded)
    pstart = pend - padded
    dest = pstart[top_idx] + jnp.take_along_axis(rank, top_idx, axis=1)
    nb = -(-(t * TOP_K) // MOE_ROWS) + e
    block_expert = jnp.minimum(jnp.searchsorted(pend, jnp.arange(nb) * MOE_ROWS, side="right"), e - 1).astype(jnp.int32)
    n_used = (pend[-1] // MOE_ROWS).astype(jnp.int32).reshape(1)
    by_expert = (jnp.argsort(top_idx.reshape(-1), stable=True) // TOP_K).astype(jnp.int32)
    by_expert = jnp.concatenate([by_expert, jnp.zeros((nb * MOE_ROWS + MOE_ROWS - t * TOP_K,), jnp.int32)])
    start = jnp.cumsum(counts) - counts
    first = jnp.arange(nb) * MOE_ROWS - (pstart - start)[block_expert]
    row_src = by_expert[(first[:, None] + jnp.arange(MOE_ROWS)[None, :]).reshape(-1)]
    n_groups = _pick(nb, (MOE_GROUPS, 2, 1))
    per = nb // n_groups
    yb = None
    for k in range(n_groups):
        buf = a_all[row_src[k * per * MOE_ROWS:(k + 1) * per * MOE_ROWS]]
        yb = _experts(buf, block_expert, n_used, w_gu, w_down, layer, first_block=k * per, total_blocks=nb, prev=yb)
    return yb, dest


def _rope_angles(pos, dim):
    inv = np.float32(ROPE_BASE) ** (-np.arange(0, dim, 2, dtype=np.float32) / np.float32(dim))
    ang = pos.astype(np.float32)[:, None] * inv[None, :]
    return np.concatenate([ang, ang], axis=-1)


def _signed_sin(sin, shift):
    low = (np.arange(sin.shape[1]) % (2 * shift)) < shift
    zero = np.float32(0.0)
    return jnp.asarray(np.where(low[None, :], -sin, zero)), jnp.asarray(np.where(low[None, :], zero, sin))


def _axial_tables(n_tokens, head_dim):
    rows = n_tokens // GRID_W
    row = np.repeat(np.arange(rows, dtype=np.int32), GRID_W)
    col = np.tile(np.arange(GRID_W, dtype=np.int32), rows)
    half = head_dim // 2
    ang = np.concatenate([_rope_angles(row, half), _rope_angles(col, half)], axis=-1)
    ang = np.tile(ang, (1, LANES // head_dim))
    sa, sb = _signed_sin(np.sin(ang), half // 2)
    return jnp.asarray(np.cos(ang)), sa, sb, half // 2


def _ret_tables(n_tokens):
    ang = _rope_angles(np.arange(n_tokens), RET_KEY)
    sa, sb = _signed_sin(np.sin(ang), RET_KEY // 2)
    return jnp.asarray(np.cos(ang)), sa, sb, RET_KEY // 2


def kernel(x, c, ctx, c_ctx, mod_w, mod_b, norm_g, da_w_in, da_w_out, da_lambda, da_subln_g, wa_w_in, wa_w_out,
           wa_sink, ret_w_in, ret_w_out, ret_decay_logit, ffn_w_gu, ffn_w_down, moe_router, moe_w_gu, moe_w_down):
    b, l, d = x.shape
    lc = ctx.shape[1]
    depth = mod_w.shape[0]
    t, tc = b * l, b * lc
    tm = _pick(l, (1024, 512, 256, 128))
    tmc = _pick(tc, (1024, 512, 256, 128))

    n_cond = -(-(b + 1) // 8) * 8
    cs = jnp.zeros((n_cond, d), F32).at[:b].set(c).at[b].set(c_ctx)
    mod_all = _mod_vectors(cs, mod_w, mod_b)

    xs = x.reshape(t, d)
    hs = ctx.reshape(tc, d)
    mixer_count = [0] * N_MIXERS
    i_dense = 0
    i_moe = 0
    for layer in range(depth):
        last = layer == depth - 1
        mod = mod_all[layer, :b].reshape(b, 1, 6 * d)
        mod_c = mod_all[layer, b:b + 1].reshape(1, 1, 6 * d)
        ng = norm_g[layer]
        kind = layer % N_MIXERS
        jm = mixer_count[kind]
        mixer_count[kind] += 1

        if kind == 0:
            lam_init = 0.8 - 0.6 * math.exp(-0.3 * layer)
            w_in = da_w_in[jm].astype(BF16)
            cos, sa, sb, shift = _axial_tables(l, DA_HEAD)
            qkv = _norm_proj(xs, ng[0], mod, 0, l, w_in, tm=tm, tn=3 * d, rot=(cos, sa, sb, 2 * d, shift, l),
                             name="da_in_proj").reshape(b, l, 3 * d)
            qkv_c = _norm_proj(hs, ng[0], mod_c, 0, None, w_in, tm=tmc, tn=3 * d, name="da_in_proj_ctx").reshape(b, lc, 3 * d)
            lp = da_lambda[jm].astype(F32)
            lam = jnp.exp(jnp.sum(lp[0] * lp[1])) - jnp.exp(jnp.sum(lp[2] * lp[3])) + lam_init
            o = _diff_attention(qkv, [qkv_c, qkv], lam, da_subln_g[jm], lam_init, tq=_pick(l, (DA_TQ, 128)), name="diff_attn")
            w_out = da_w_out[jm].astype(BF16)
            xs = _resid(xs, ng[1], mod, 2, l, tm=tm, a=o.reshape(t, d), w=w_out, name="da_out_proj")
            if not last:
                oc = _diff_attention(qkv_c, [qkv_c], lam, da_subln_g[jm], lam_init, tq=_pick(lc, (DA_TQ, 128)), name="diff_attn_ctx")
                hs = _resid(hs, ng[1], mod_c, 2, None, tm=tmc, a=oc.reshape(tc, d), w=w_out, name="da_out_proj_ctx")
        elif kind == 1:
            w_in = wa_w_in[jm].astype(BF16)
            n_qkv = w_in.shape[1]
            cos, sa, sb, shift = _axial_tables(l, WA_HEAD)
            n_rot = (WA_Q_HEADS + WA_KV_HEADS) * WA_HEAD
            qkv = _norm_proj(xs, ng[0], mod, 0, l, w_in, tm=tm, tn=n_qkv, rot=(cos, sa, sb, n_rot, shift, l),
                             name="wa_in_proj").reshape(b, l, n_qkv)
            qkv_c = _norm_proj(hs, ng[0], mod_c, 0, None, w_in, tm=tmc, tn=n_qkv, name="wa_in_proj_ctx").reshape(b, lc, n_qkv)
            o = _window_attention(qkv, qkv_c, wa_sink[jm], windowed=True, name="window_attn")
            w_out = wa_w_out[jm].astype(BF16)
            xs = _resid(xs, ng[1], mod, 2, l, tm=tm, a=o.reshape(t, d), w=w_out, name="wa_out_proj")
            if not last:
                oc = _window_attention(qkv_c, qkv_c, wa_sink[jm], windowed=False, name="window_attn_ctx")
                hs = _resid(hs, ng[1], mod_c, 2, None, tm=tmc, a=oc.reshape(tc, d), w=w_out, name="wa_out_proj_ctx")
        else:
            w_in = ret_w_in[jm].astype(BF16)
            n_qkv = w_in.shape[1]
            cos, sa, sb, shift = _ret_tables(l)
            n_rot = 2 * RET_HEADS * RET_KEY
            qkv = _norm_proj(xs, ng[0], mod, 0, l, w_in, tm=tm, tn=n_qkv // 2, rot=(cos, sa, sb, n_rot, shift, l),
                             name="ret_in_proj")
            qkv_c = _norm_proj(hs, ng[0], mod_c, 0, None, w_in, tm=_pick(tc, (512, 256, 128)), tn=n_qkv // 2,
                               name="ret_in_proj_ctx")
            log_g = jax.nn.log_sigmoid(ret_decay_logit[jm].astype(F32))
            ocf, olf, ocb, olb = _retention(qkv_c.reshape(b, lc, n_qkv), qkv.reshape(b, l, n_qkv), log_g)
            w_out = ret_w_out[jm].astype(BF16)
            vw = RET_HEADS * RET_VAL
            xs = _resid(xs, ng[1], mod, 2, l, tm=_pick(l, (512, 256, 128)), w=w_out,
                        ret=(olf.reshape(t, vw), olb.reshape(t, vw), qkv), name="ret_out_proj")
            if not last:
                hs = _resid(hs, ng[1], mod_c, 2, None, tm=_pick(tc, (512, 256, 128)), w=w_out,
                            ret=(ocf.reshape(tc, vw), ocb.reshape(tc, vw), qkv_c), name="ret_out_proj_ctx")

        if layer % 2 == 0:
            w_gu = ffn_w_gu[i_dense].astype(BF16)
            w_dn = ffn_w_down[i_dense].astype(BF16)
            i_dense += 1
            f = w_dn.shape[0]
            w_g, w_u = w_gu[:, :f], w_gu[:, f:]
            tn = f
            tf = _pick(l, (512, 256, 128))
            act = _norm_proj(xs, ng[2], mod, 3, l, (w_g, w_u), tm=tf, tn=tn, name="ffn_up")
            xs = _resid(xs, ng[3], mod, 5, l, tm=tf, a=act, w=w_dn, name="ffn_down")
            if not last:
                tfc = _pick(tc, (512, 256, 128))
                act_c = _norm_proj(hs, ng[2], mod_c, 3, None, (w_g, w_u), tm=tfc, tn=tn, name="ffn_up_ctx")
                hs = _resid(hs, ng[3], mod_c, 5, None, tm=tfc, a=act_c, w=w_dn, name="ffn_down_ctx")
        else:
            w_r = jnp.zeros((d, ROUTER_PAD), F32).at[:, :N_EXPERTS].set(moe_router[i_moe])
            moe_layer = i_moe
            i_moe += 1
            pool = t if last else t + tc
            a_all, route = _router(xs, ng[2], mod, 3, l, w_r, tm=tm, name="moe_router", pool_rows=pool)
            if not last:
                a_all, route = _router(hs, ng[2], mod_c, 3, None, w_r, tm=tmc, name="moe_router_ctx", pool_rows=pool,
                                       row_offset=t, prev=(a_all, route))
            route_x, route_c = route[:t], route[t:]
            yb, dest = _moe(a_all, route[:, 2:4].astype(jnp.int32), moe_w_gu, moe_w_down, moe_layer)
            xs = _resid(xs, ng[3], mod, 5, l, tm=tm, mix=(yb[dest[:t, 0]], yb[dest[:t, 1]], route_x), name="moe_combine")
            if not last:
                hs = _resid(hs, ng[3], mod_c, 5, None, tm=tmc, mix=(yb[dest[t:, 0]], yb[dest[t:, 1]], route_c),
                            name="moe_combine_ctx")
    return xs.reshape(b, l, d)
```
